```python
import math
import jax, jax.numpy as jnp
from jax import lax
import numpy as np

D_MODEL = 1024
BATCH = 4
SEQ = 8192
DEPTH = 1

CHUNK = 64
Q_BLOCK = 128
SB_HEADS = 8
SB_HEAD_DIM = 64
SB_WIDTH = SB_HEADS * SB_HEAD_DIM
CA_HEADS = 8
CA_HEAD_DIM = 64
CA_WIDTH = CA_HEADS * CA_HEAD_DIM
CA_LEFT_CHUNKS = 8
CA_BAND = (CA_LEFT_CHUNKS + 1) * CHUNK
CA_PAD = CA_LEFT_CHUNKS * CHUNK
REL_CLIP = 128
N_REL = 2 * REL_CLIP + 1
IN_COLS = 3 * SB_WIDTH + 3 * CA_WIDTH + 2 * D_MODEL
PEER_HEADS = 8
PEER_N_KEYS = 128
PEER_N_EXPERTS = PEER_N_KEYS * PEER_N_KEYS
PEER_QUERY_DIM = 256
PEER_HALF = PEER_QUERY_DIM // 2
PEER_TOPK = 16
PEER_TOKEN_BLOCK = 128
RMS_EPS = 1e-6
NEG_INF = -1e30

kernel_name = "hybrid_stickbreak_chunkattn_peer"


def rmsnorm(x, g):
    xf = x.astype(jnp.float32)
    y = xf * lax.rsqrt(jnp.mean(xf * xf, axis=-1, keepdims=True) + RMS_EPS)
    return (y * g.astype(jnp.float32)).astype(x.dtype)


def split_in_proj(p):
    widths = (SB_WIDTH,) * 3 + (CA_WIDTH,) * 3 + (D_MODEL, D_MODEL)
    cuts = np.cumsum(widths)[:-1].tolist()
    return jnp.split(p, cuts, axis=-1)


def stick_breaking_attention(q, k, v):
    B, S, H, Dh = q.shape
    scale = 1.0 / math.sqrt(Dh)
    kf = k.astype(jnp.float32)
    vf = v.astype(jnp.float32)
    key_pos = jnp.arange(S)

    def one_block(i):
        start = i * Q_BLOCK
        qb = lax.dynamic_slice_in_dim(q, start, Q_BLOCK, axis=1).astype(jnp.float32)
        z = jnp.einsum('bqhd,bkhd->bhqk', qb, kf) * scale
        qpos = start + jnp.arange(Q_BLOCK)
        causal = (key_pos[None, :] < qpos[:, None])[None, None]
        log_1m = jnp.where(causal, jax.nn.log_sigmoid(-z), 0.0)
        after = lax.cumsum(log_1m, axis=3, reverse=True) - log_1m
        w = jnp.where(causal, jnp.exp(jax.nn.log_sigmoid(z) + after), 0.0)
        return jnp.einsum('bhqk,bkhd->bqhd', w, vf)

    out = lax.map(one_block, jnp.arange(S // Q_BLOCK))
    return jnp.moveaxis(out, 0, 1).reshape(B, S, H, Dh)


def chunked_rel_attention(q, k, v, rel_table):
    B, S, H, Dh = q.shape
    scale = 1.0 / math.sqrt(Dh)
    kp = jnp.pad(k, ((0, 0), (CA_PAD, 0), (0, 0), (0, 0)))
    vp = jnp.pad(v, ((0, 0), (CA_PAD, 0), (0, 0), (0, 0)))
    rel = jnp.arange(CHUNK)[:, None] + CA_PAD - jnp.arange(CA_BAND)[None, :]
    bias = rel_table.astype(jnp.float32)[:, jnp.clip(rel, -REL_CLIP, REL_CLIP) + REL_CLIP]

    def one_chunk(c):
        start = c * CHUNK
        qc = lax.dynamic_slice_in_dim(q, start, CHUNK, axis=1).astype(jnp.float32)
        kb = lax.dynamic_slice_in_dim(kp, start, CA_BAND, axis=1).astype(jnp.float32)
        vb = lax.dynamic_slice_in_dim(vp, start, CA_BAND, axis=1).astype(jnp.float32)
        s = jnp.einsum('bqhd,bkhd->bhqk', qc, kb) * scale + bias[None]
        valid = (start - CA_PAD + jnp.arange(CA_BAND)) >= 0
        s = jnp.where(valid[None, None, None, :], s, NEG_INF)
        p = jax.nn.softmax(s, axis=-1)
        return jnp.einsum('bhqk,bkhd->bqhd', p, vb)

    out = lax.map(one_chunk, jnp.arange(S // CHUNK))
    return jnp.moveaxis(out, 0, 1).reshape(B, S, H, Dh)


def peer_ffn(h, w_query, sub_keys, expert_u, expert_v):
    B, S, D = h.shape
    tokens = h.reshape(B * S // PEER_TOKEN_BLOCK, PEER_TOKEN_BLOCK, D)
    keys_f = sub_keys.astype(jnp.float32)

    def one_block(xb):
        xf = xb.astype(jnp.float32)
        q = (xb @ w_query).astype(jnp.float32).reshape(PEER_TOKEN_BLOCK, PEER_HEADS, 2, PEER_HALF)
        scores = jnp.einsum('thpc,hpnc->thpn', q, keys_f)
        top_v, top_i = lax.top_k(scores, PEER_TOPK)
        cand_v = (top_v[:, :, 0, :, None] + top_v[:, :, 1, None, :]).reshape(
            PEER_TOKEN_BLOCK, PEER_HEADS, PEER_TOPK * PEER_TOPK)
        cand_i = (top_i[:, :, 0, :, None] * PEER_N_KEYS + top_i[:, :, 1, None, :]).reshape(
            PEER_TOKEN_BLOCK, PEER_HEADS, PEER_TOPK * PEER_TOPK)
        best_v, best_pos = lax.top_k(cand_v, PEER_TOPK)
        expert_idx = jnp.take_along_axis(cand_i, best_pos, axis=-1)
        g = jax.nn.softmax(best_v, axis=-1)
        u = expert_u[expert_idx].astype(jnp.float32)
        a = jax.nn.gelu(jnp.einsum('thkd,td->thk', u, xf), approximate=False)
        vsel = expert_v[expert_idx].astype(jnp.float32)
        return jnp.einsum('thk,thkd->td', g * a, vsel).astype(xb.dtype)

    out = lax.map(one_block, tokens)
    return out.reshape(B, S, D)


def setup_inputs(seed: int = 0) -> dict:
    key = jax.random.key(seed)
    ks = jax.random.split(key, 16)
    f32 = jnp.float32
    nrm = lambda k, shape, s: jax.random.normal(k, shape, f32) * s
    return {
        "x": nrm(ks[0], (BATCH, SEQ, D_MODEL), 1.0),
        "norm_mix": 1.0 + nrm(ks[1], (DEPTH, D_MODEL), 0.02),
        "w_in": nrm(ks[2], (DEPTH, D_MODEL, IN_COLS), D_MODEL ** -0.5),
        "w_sb_proj": nrm(ks[3], (DEPTH, SB_WIDTH, D_MODEL), SB_WIDTH ** -0.5),
        "w_ca_proj": nrm(ks[4], (DEPTH, CA_WIDTH, D_MODEL), CA_WIDTH ** -0.5),
        "ca_rel_bias": nrm(ks[5], (DEPTH, CA_HEADS, N_REL), 0.5),
        "w_out": nrm(ks[6], (DEPTH, D_MODEL, D_MODEL), D_MODEL ** -0.5),
        "norm_ffn": 1.0 + nrm(ks[7], (DEPTH, D_MODEL), 0.02),
        "peer_w_query": nrm(ks[8], (DEPTH, D_MODEL, PEER_HEADS * PEER_QUERY_DIM), D_MODEL ** -0.5),
        "peer_sub_keys": nrm(ks[9], (DEPTH, PEER_HEADS, 2, PEER_N_KEYS, PEER_HALF), PEER_HALF ** -0.5),
        "peer_u": nrm(ks[10], (DEPTH, PEER_N_EXPERTS, D_MODEL), D_MODEL ** -0.5),
        "peer_v": nrm(ks[11], (DEPTH, PEER_N_EXPERTS, D_MODEL), 0.5),
        "norm_final": 1.0 + nrm(ks[12], (D_MODEL,), 0.02),
    }


def reference(x, norm_mix, w_in, w_sb_proj, w_ca_proj, ca_rel_bias, w_out, norm_ffn,
              peer_w_query, peer_sub_keys, peer_u, peer_v, norm_final):
    B, S, _ = x.shape
    for layer in range(DEPTH):
        h = rmsnorm(x, norm_mix[layer])
        proj = h @ w_in[layer]
        q_sb, k_sb, v_sb, q_ca, k_ca, v_ca, gate_sb, gate_ca = split_in_proj(proj)
        sb4 = lambda t: t.reshape(B, S, SB_HEADS, SB_HEAD_DIM)
        ca4 = lambda t: t.reshape(B, S, CA_HEADS, CA_HEAD_DIM)
        o_sb = stick_breaking_attention(sb4(q_sb), sb4(k_sb), sb4(v_sb)).reshape(B, S, SB_WIDTH)
        o_ca = chunked_rel_attention(ca4(q_ca), ca4(k_ca), ca4(v_ca),
                                     ca_rel_bias[layer]).reshape(B, S, CA_WIDTH)
        br_sb = (o_sb.astype(x.dtype) @ w_sb_proj[layer]).astype(jnp.float32)
        br_ca = (o_ca.astype(x.dtype) @ w_ca_proj[layer]).astype(jnp.float32)
        merged = (jax.nn.sigmoid(gate_sb.astype(jnp.float32)) * br_sb
                  + jax.nn.sigmoid(gate_ca.astype(jnp.float32)) * br_ca)
        x = x + merged.astype(x.dtype) @ w_out[layer]
        h2 = rmsnorm(x, norm_ffn[layer])
        x = x + peer_ffn(h2, peer_w_query[layer], peer_sub_keys[layer],
                         peer_u[layer], peer_v[layer])
    return rmsnorm(x, norm_final)
```

```python
import functools
import math

import numpy as np
import jax
import jax.numpy as jnp
from jax import lax
from jax.experimental import pallas as pl
from jax.experimental.pallas import tpu as pltpu

D_MODEL = 1024
HEAD_DIM = 64
N_HEADS = 8
ATT_WIDTH = N_HEADS * HEAD_DIM
IN_COLS = 6 * ATT_WIDTH + 2 * D_MODEL
CHUNK = 64
CA_LEFT_CHUNKS = 8
CA_PAD = CA_LEFT_CHUNKS * CHUNK
REL_CLIP = 128
PEER_HEADS = 8
PEER_N_KEYS = 128
PEER_HALF = 128
PEER_TOPK = 16
PEER_SLOTS = PEER_HEADS * PEER_TOPK
RMS_EPS = 1e-6
NEG_INF = -1e30

LANES = 128
SUBLANES = 8
PACK_ROWS = D_MODEL // 2 // LANES

VMEM_LIMIT = 56 * 1024 * 1024

F32 = jnp.float32
BF16 = jnp.bfloat16
NT_DIMS = (((1,), (1,)), ((), ()))


def _params(sem, vmem=VMEM_LIMIT):
    return pltpu.CompilerParams(dimension_semantics=sem, vmem_limit_bytes=vmem)


def _inproj_kernel(x_ref, g_ref, w_ref, o_ref):
    x = x_ref[...]
    ms = jnp.mean(x * x, axis=-1, keepdims=True)
    h = (x * lax.rsqrt(ms + RMS_EPS) * g_ref[...]).astype(BF16)
    for c in range(IN_COLS // D_MODEL):
        cols = slice(c * D_MODEL, (c + 1) * D_MODEL)
        o_ref[:, cols] = jnp.dot(h, w_ref[:, cols], preferred_element_type=F32).astype(o_ref.dtype)


def _inproj(x2, g, w_bf, tm):
    n = x2.shape[0]
    return pl.pallas_call(
        _inproj_kernel,
        grid=(n // tm,),
        in_specs=[
            pl.BlockSpec((tm, D_MODEL), lambda i: (i, 0)),
            pl.BlockSpec((1, D_MODEL), lambda i: (0, 0)),
            pl.BlockSpec((D_MODEL, IN_COLS), lambda i: (0, 0)),
        ],
        out_specs=pl.BlockSpec((tm, IN_COLS), lambda i: (i, 0)),
        out_shape=jax.ShapeDtypeStruct((n, IN_COLS), BF16),
        compiler_params=_params(("arbitrary",)),
        name="inproj",
    )(x2, g, w_bf)


def _sb_kernel(q_ref, k_ref, v_ref, o_ref, acc_ref, r_ref, *, tq, rc):
    i = pl.program_id(2)
    lane = lax.broadcasted_iota(jnp.int32, (1, LANES), 1)
    head_mask = (lane < HEAD_DIM, lane >= HEAD_DIM)
    kk = lax.broadcasted_iota(jnp.int32, (tq, tq), 0)
    ss = lax.broadcasted_iota(jnp.int32, (tq, tq), 1)
    stri = jnp.where(kk > ss, 1.0, 0.0).astype(BF16)
    ones = jnp.ones((tq, LANES), BF16)

    acc_ref[...] = jnp.zeros_like(acc_ref)
    r_ref[...] = jnp.zeros_like(r_ref)

    def block(j, diag):
        start = pl.multiple_of(j * tq, tq)
        k = k_ref[0, pl.ds(start, tq), :]
        v = v_ref[0, pl.ds(start, tq), :]
        for h in range(2):
            kh = jnp.where(head_mask[h], k, jnp.zeros_like(k))
            vh = jnp.where(head_mask[h], v, jnp.zeros_like(v))
            for c in range(tq // rc):
                rows = slice(c * rc, (c + 1) * rc)
                q = q_ref[0, rows, :]
                z = lax.dot_general(q, kh, NT_DIMS, preferred_element_type=F32)
                sp = jnp.maximum(z, 0.0) + jnp.log(1.0 + jnp.exp(-jnp.abs(z)))
                if diag:
                    rr = lax.broadcasted_iota(jnp.int32, (rc, tq), 0) + c * rc
                    cc = lax.broadcasted_iota(jnp.int32, (rc, tq), 1)
                    causal = cc < rr
                    sp = jnp.where(causal, sp, 0.0)
                spb = sp.astype(BF16)
                later = jnp.dot(spb, stri, preferred_element_type=F32)
                r = r_ref[h, rows, :]
                logw = (z - sp) - later - jnp.concatenate([r] * (tq // LANES), axis=1)
                if diag:
                    logw = jnp.where(causal, logw, NEG_INF)
                w = jnp.exp(logw).astype(BF16)
                acc_ref[rows, :] += jnp.dot(w, vh, preferred_element_type=F32)
                r_ref[h, rows, :] = r + jnp.dot(spb, ones, preferred_element_type=F32)

    block(i, True)

    def body(jj, carry):
        block(i - 1 - jj, False)
        return carry

    lax.fori_loop(0, i, body, 0)
    o_ref[0] = acc_ref[...].astype(o_ref.dtype)


def _sb_attention(proj3, tq, rc):
    b, s, _ = proj3.shape
    n_pairs = ATT_WIDTH // LANES
    kern = functools.partial(_sb_kernel, tq=tq, rc=rc)
    return pl.pallas_call(
        kern,
        grid=(b, n_pairs, s // tq),
        in_specs=[
            pl.BlockSpec((1, tq, LANES), lambda bi, p, i: (bi, i, p)),
            pl.BlockSpec((1, s, LANES), lambda bi, p, i: (bi, 0, n_pairs + p)),
            pl.BlockSpec((1, s, LANES), lambda bi, p, i: (bi, 0, 2 * n_pairs + p)),
        ],
        out_specs=pl.BlockSpec((1, tq, LANES), lambda bi, p, i: (bi, i, p)),
        out_shape=jax.ShapeDtypeStruct((b, s, ATT_WIDTH), BF16),
        scratch_shapes=[pltpu.VMEM((tq, LANES), F32), pltpu.VMEM((2, tq, LANES), F32)],
        compiler_params=_params(("arbitrary", "arbitrary", "arbitrary")),
        name="sb_attention",
    )(proj3, proj3, proj3)


def _ca_kernel(q_ref, k_ref, v_ref, b_ref, o_ref, *, tq, win, rc):
    i = pl.program_id(2)
    ws = pl.multiple_of(jnp.maximum(i * tq - CA_PAD, 0), tq)
    lane = lax.broadcasted_iota(jnp.int32, (1, LANES), 1)
    head_mask = (lane < HEAD_DIM, lane >= HEAD_DIM)
    k = k_ref[0, pl.ds(ws, win), :]
    v = v_ref[0, pl.ds(ws, win), :]
    for c in range(tq // rc):
        rows = slice(c * rc, (c + 1) * rc)
        q = q_ref[0, rows, :]
        out = None
        for h in range(2):
            kh = jnp.where(head_mask[h], k, jnp.zeros_like(k))
            vh = jnp.where(head_mask[h], v, jnp.zeros_like(v))
            z = lax.dot_general(q, kh, NT_DIMS, preferred_element_type=F32) + b_ref[0, h, rows, :]
            m = jnp.max(z, axis=1, keepdims=True)
            p = jnp.exp(z - m)
            l = jnp.sum(p, axis=1, keepdims=True)
            o = jnp.dot(p.astype(BF16), vh, preferred_element_type=F32) / l
            out = o if out is None else out + o
        o_ref[0, rows, :] = out.astype(o_ref.dtype)


def _ca_bias(rel_table, tq, win):
    r = np.arange(tq)[:, None]
    c = np.arange(win)[None, :]
    idx, valid = [], []
    for var in range(3):
        off = var * tq - max(var * tq - CA_PAD, 0)
        rel = off + r - c
        relc = (off + r) // CHUNK - c // CHUNK
        idx.append(np.clip(rel, -REL_CLIP, REL_CLIP) + REL_CLIP)
        valid.append((relc >= 0) & (relc <= CA_LEFT_CHUNKS))
    idx = np.stack(idx)
    valid = np.stack(valid)
    bias = jnp.transpose(rel_table.astype(F32)[:, idx], (1, 0, 2, 3))
    return jnp.where(valid[:, None], bias, NEG_INF)


def _ca_attention(proj3, bias, tq, win, rc):
    b, s, _ = proj3.shape
    n_pairs = ATT_WIDTH // LANES
    kern = functools.partial(_ca_kernel, tq=tq, win=win, rc=rc)
    return pl.pallas_call(
        kern,
        grid=(b, n_pairs, s // tq),
        in_specs=[
            pl.BlockSpec((1, tq, LANES), lambda bi, p, i: (bi, i, 3 * n_pairs + p)),
            pl.BlockSpec((1, s, LANES), lambda bi, p, i: (bi, 0, 4 * n_pairs + p)),
            pl.BlockSpec((1, s, LANES), lambda bi, p, i: (bi, 0, 5 * n_pairs + p)),
            pl.BlockSpec((1, 2, tq, win), lambda bi, p, i: (jnp.minimum(i, 2), p, 0, 0)),
        ],
        out_specs=pl.BlockSpec((1, tq, LANES), lambda bi, p, i: (bi, i, p)),
        out_shape=jax.ShapeDtypeStruct((b, s, ATT_WIDTH), BF16),
        compiler_params=_params(("arbitrary", "arbitrary", "arbitrary")),
        name="ca_attention",
    )(proj3, proj3, proj3, bias)


def _merge_kernel(osb_ref, oca_ref, gsb_ref, gca_ref, x_ref, psb_ref, pca_ref, wout_ref, nf_ref,
                  x1_ref, h2_ref):
    br_sb = jnp.dot(osb_ref[...], psb_ref[...], preferred_element_type=F32)
    br_ca = jnp.dot(oca_ref[...], pca_ref[...], preferred_element_type=F32)
    merged = (jax.nn.sigmoid(gsb_ref[...].astype(F32)) * br_sb
              + jax.nn.sigmoid(gca_ref[...].astype(F32)) * br_ca)
    x1 = x_ref[...] + jnp.dot(merged.astype(BF16), wout_ref[...], preferred_element_type=F32)
    x1_ref[...] = x1
    ms = jnp.mean(x1 * x1, axis=-1, keepdims=True)
    h2_ref[...] = x1 * lax.rsqrt(ms + RMS_EPS) * nf_ref[...]


def _merge(o_sb, o_ca, proj, x2, psb, pca, wout, nf, tm):
    n = x2.shape[0]
    gate_blk = 3 * ATT_WIDTH * 2 // D_MODEL
    return pl.pallas_call(
        _merge_kernel,
        grid=(n // tm,),
        in_specs=[
            pl.BlockSpec((tm, ATT_WIDTH), lambda i: (i, 0)),
            pl.BlockSpec((tm, ATT_WIDTH), lambda i: (i, 0)),
            pl.BlockSpec((tm, D_MODEL), lambda i: (i, gate_blk)),
            pl.BlockSpec((tm, D_MODEL), lambda i: (i, gate_blk + 1)),
            pl.BlockSpec((tm, D_MODEL), lambda i: (i, 0)),
            pl.BlockSpec((ATT_WIDTH, D_MODEL), lambda i: (0, 0)),
            pl.BlockSpec((ATT_WIDTH, D_MODEL), lambda i: (0, 0)),
            pl.BlockSpec((D_MODEL, D_MODEL), lambda i: (0, 0)),
            pl.BlockSpec((1, D_MODEL), lambda i: (0, 0)),
        ],
        out_specs=[
            pl.BlockSpec((tm, D_MODEL), lambda i: (i, 0)),
            pl.BlockSpec((tm, D_MODEL), lambda i: (i, 0)),
        ],
        out_shape=[jax.ShapeDtypeStruct((n, D_MODEL), F32), jax.ShapeDtypeStruct((n, D_MODEL), F32)],
        compiler_params=_params(("arbitrary",)),
        name="merge",
    )(o_sb, o_ca, proj, proj, x2, psb, pca, wout, nf)


def _peer_candidates():
    return [(a, b) for a in range(PEER_TOPK) for b in range(PEER_TOPK) if (a + 1) * (b + 1) <= PEER_TOPK]


def _extract_top(s, iota, n_rounds, sentinel):
    vals, ids = [], []
    for _ in range(n_rounds):
        m = jnp.max(s, axis=0, keepdims=True)
        am = jnp.min(jnp.where(s == m, iota, sentinel), axis=0, keepdims=True)
        vals.append(m)
        ids.append(am)
        s = jnp.where(iota == am, -jnp.inf, s)
    return vals, ids


def _route_kernel(h_ref, wq_ref, keys_ref, idx_ref, g_ref, cv_scr, ci_scr, ri_scr, rg_scr, *, tb):
    cands = _peer_candidates()
    n_cand = len(cands)
    n_rows = cv_scr.shape[0]
    h = h_ref[...].astype(BF16)
    iota_n = lax.broadcasted_iota(jnp.int32, (PEER_N_KEYS, tb), 0)
    iota_c = lax.broadcasted_iota(jnp.int32, (n_rows, tb), 0)
    cv_scr[n_cand:, :] = jnp.full((n_rows - n_cand, tb), -jnp.inf, F32)
    ci_scr[n_cand:, :] = jnp.zeros((n_rows - n_cand, tb), jnp.int32)
    for hd in range(PEER_HEADS):
        tops = []
        for p in range(2):
            hp = hd * 2 + p
            q = jnp.dot(h, wq_ref[:, hp * PEER_HALF:(hp + 1) * PEER_HALF],
                        preferred_element_type=F32).astype(BF16)
            s = lax.dot_general(keys_ref[hp], q, NT_DIMS, preferred_element_type=F32)
            tops.append(_extract_top(s, iota_n, PEER_TOPK, PEER_N_KEYS))
        (v1, i1), (v2, i2) = tops
        i1s = [x * PEER_N_KEYS for x in i1]
        for r, (a, b) in enumerate(cands):
            cv_scr[r:r + 1, :] = v1[a] + v2[b]
            ci_scr[r:r + 1, :] = i1s[a] + i2[b]
        cv = cv_scr[...]
        ci = ci_scr[...]
        best = []
        for r in range(PEER_TOPK):
            m = jnp.max(cv, axis=0, keepdims=True)
            am = jnp.min(jnp.where(cv == m, iota_c, n_rows), axis=0, keepdims=True)
            hit = iota_c == am
            e = jnp.sum(jnp.where(hit, ci, 0), axis=0, keepdims=True)
            cv = jnp.where(hit, -jnp.inf, cv)
            best.append(m)
            ri_scr[hd * PEER_TOPK + r: hd * PEER_TOPK + r + 1, :] = e * PACK_ROWS
        ex = [jnp.exp(bv - best[0]) for bv in best]
        tot = ex[0]
        for t in ex[1:]:
            tot = tot + t
        inv = 1.0 / tot
        for r in range(PEER_TOPK):
            rg_scr[hd * PEER_TOPK + r: hd * PEER_TOPK + r + 1, :] = ex[r] * inv
    idx_ref[...] = ri_scr[...].T
    g_ref[...] = rg_scr[...].T


def _route(h2, wq_bf, keys_bf, tb):
    n = h2.shape[0]
    n_rows = -(-len(_peer_candidates()) // SUBLANES) * SUBLANES
    kern = functools.partial(_route_kernel, tb=tb)
    return pl.pallas_call(
        kern,
        grid=(n // tb,),
        in_specs=[
            pl.BlockSpec((tb, D_MODEL), lambda i: (i, 0)),
            pl.BlockSpec((D_MODEL, 2 * PEER_HEADS * PEER_HALF), lambda i: (0, 0)),
            pl.BlockSpec((2 * PEER_HEADS, PEER_N_KEYS, PEER_HALF), lambda i: (0, 0, 0)),
        ],
        out_specs=[
            pl.BlockSpec((tb, PEER_SLOTS), lambda i: (i, 0)),
            pl.BlockSpec((tb, PEER_SLOTS), lambda i: (i, 0)),
        ],
        out_shape=[jax.ShapeDtypeStruct((n, PEER_SLOTS), jnp.int32),
                   jax.ShapeDtypeStruct((n, PEER_SLOTS), F32)],
        scratch_shapes=[
            pltpu.VMEM((n_rows, tb), F32),
            pltpu.VMEM((n_rows, tb), jnp.int32),
            pltpu.VMEM((PEER_SLOTS, tb), jnp.int32),
            pltpu.VMEM((PEER_SLOTS, tb), F32),
        ],
        compiler_params=_params(("arbitrary",)),
        name="peer_route",
    )(h2, wq_bf, keys_bf)


def _pack_table(t):
    e = t.shape[0]
    bits = lax.bitcast_convert_type(t.astype(BF16), jnp.uint16).astype(jnp.uint32)
    half = D_MODEL // 2
    packed = (bits[:, :half] << 16) | bits[:, half:]
    return packed.reshape(e * PACK_ROWS, LANES)


def _unpack_row(w):
    hi = lax.bitcast_convert_type(w & jnp.uint32(0xFFFF0000), F32)
    lo = lax.bitcast_convert_type(w << 16, F32)
    return hi, lo


def _table_spec(n_rows):
    return pl.BlockSpec((n_rows, LANES), lambda i: (0, 0), pipeline_mode=pl.Buffered(1))


def _peer_dot_kernel(idx_ref, h_ref, g_ref, tab_ref, c_ref, p_scr, a_scr, *, tb):
    ones = jnp.ones((SUBLANES, LANES), F32)

    def tok(t, carry):
        hv = h_ref[pl.ds(pl.multiple_of(t * SUBLANES, SUBLANES), SUBLANES), :]
        h_hi = hv[:PACK_ROWS]
        h_lo = hv[PACK_ROWS:]
        for k in range(PEER_SLOTS):
            e = pl.multiple_of(idx_ref[t, k], PACK_ROWS)
            hi, lo = _unpack_row(tab_ref[pl.ds(e, PACK_ROWS), :])
            p_scr[k * PACK_ROWS:(k + 1) * PACK_ROWS, :] = hi * h_hi + lo * h_lo
        part = p_scr[pl.ds(0, PEER_SLOTS, stride=PACK_ROWS), :]
        for s in range(1, PACK_ROWS):
            part = part + p_scr[pl.ds(s, PEER_SLOTS, stride=PACK_ROWS), :]
        a_rows = lax.dot_general(ones, part, NT_DIMS, preferred_element_type=F32,
                                 precision=lax.Precision.HIGHEST)
        a_scr[pl.ds(t, 1), :] = a_rows[0:1]
        return carry

    lax.fori_loop(0, tb, tok, 0)
    a = a_scr[...]
    c_ref[...] = g_ref[...] * (0.5 * a * (1.0 + lax.erf(a * (1.0 / math.sqrt(2.0)))))


def _peer_dot(idx, h2r, gate, tab, tb):
    n = idx.shape[0]
    kern = functools.partial(_peer_dot_kernel, tb=tb)
    return pl.pallas_call(
        kern,
        grid=(n // tb,),
        in_specs=[
            pl.BlockSpec((tb, PEER_SLOTS), lambda i: (i, 0), memory_space=pltpu.SMEM),
            pl.BlockSpec((tb * SUBLANES, LANES), lambda i: (i, 0)),
            pl.BlockSpec((tb, PEER_SLOTS), lambda i: (i, 0)),
            _table_spec(tab.shape[0]),
        ],
        out_specs=pl.BlockSpec((tb, PEER_SLOTS), lambda i: (i, 0)),
        out_shape=jax.ShapeDtypeStruct((n, PEER_SLOTS), F32),
        scratch_shapes=[pltpu.VMEM((PEER_SLOTS * PACK_ROWS, LANES), F32),
                        pltpu.VMEM((tb, PEER_SLOTS), F32)],
        compiler_params=_params(("arbitrary",)),
        name="peer_dot",
    )(idx, h2r, gate, tab)


def _peer_axpy_kernel(idx_ref, c_ref, x_ref, gf_ref, tab_ref, y_ref, *, tb):
    n_acc = 2

    def tok(t, carry):
        acc_hi = [jnp.zeros((PACK_ROWS, LANES), F32) for _ in range(n_acc)]
        acc_lo = [jnp.zeros((PACK_ROWS, LANES), F32) for _ in range(n_acc)]
        for k in range(PEER_SLOTS):
            e = pl.multiple_of(idx_ref[t, k], PACK_ROWS)
            ck = c_ref[t, k]
            hi, lo = _unpack_row(tab_ref[pl.ds(e, PACK_ROWS), :])
            acc_hi[k % n_acc] = acc_hi[k % n_acc] + ck * hi
            acc_lo[k % n_acc] = acc_lo[k % n_acc] + ck * lo
        peer = jnp.concatenate([acc_hi[0] + acc_hi[1], acc_lo[0] + acc_lo[1]], axis=0)
        row = pl.ds(pl.multiple_of(t * SUBLANES, SUBLANES), SUBLANES)
        x2 = x_ref[row, :] + peer
        ss = jnp.sum(jnp.sum(x2 * x2, axis=1, keepdims=True), axis=0, keepdims=True)
        y_ref[row, :] = x2 * lax.rsqrt(ss * (1.0 / D_MODEL) + RMS_EPS) * gf_ref[...]
        return carry

    lax.fori_loop(0, tb, tok, 0)


def _peer_axpy(idx, c, x1r, gfr, tab, tb):
    n = idx.shape[0]
    kern = functools.partial(_peer_axpy_kernel, tb=tb)
    return pl.pallas_call(
        kern,
        grid=(n // tb,),
        in_specs=[
            pl.BlockSpec((tb, PEER_SLOTS), lambda i: (i, 0), memory_space=pltpu.SMEM),
            pl.BlockSpec((tb, PEER_SLOTS), lambda i: (i, 0), memory_space=pltpu.SMEM),
            pl.BlockSpec((tb * SUBLANES, LANES), lambda i: (i, 0)),
            pl.BlockSpec((SUBLANES, LANES), lambda i: (0, 0)),
            _table_spec(tab.shape[0]),
        ],
        out_specs=pl.BlockSpec((tb * SUBLANES, LANES), lambda i: (i, 0)),
        out_shape=jax.ShapeDtypeStruct((n * SUBLANES, LANES), F32),
        compiler_params=_params(("arbitrary",)),
        name="peer_axpy",
    )(idx, c, x1r, gfr, tab)


def _row_tile(n, want):
    t = min(want, n)
    assert n % t == 0
    return t


def _layer(x2, b, s, norm_mix, w_in, w_sb_proj, w_ca_proj, ca_rel_bias, w_out, norm_ffn,
           peer_w_query, peer_sub_keys, peer_u, peer_v, final_gain):
    n = b * s
    scale = 1.0 / math.sqrt(HEAD_DIM)
    col = np.arange(IN_COLS)
    is_q = (col < ATT_WIDTH) | ((col >= 3 * ATT_WIDTH) & (col < 4 * ATT_WIDTH))
    w_in_bf = (w_in * jnp.asarray(np.where(is_q, scale, 1.0), F32)[None, :]).astype(BF16)

    proj = _inproj(x2, norm_mix.reshape(1, D_MODEL), w_in_bf, _row_tile(n, 512))
    proj3 = proj.reshape(b, s, IN_COLS)

    tq = 256
    assert s % tq == 0 and s >= tq + CA_PAD
    o_sb = _sb_attention(proj3, tq, 128)
    win = tq + CA_PAD
    o_ca = _ca_attention(proj3, _ca_bias(ca_rel_bias, tq, win), tq, win, 128)

    x1, h2 = _merge(o_sb.reshape(n, ATT_WIDTH), o_ca.reshape(n, ATT_WIDTH), proj, x2,
                    w_sb_proj.astype(BF16), w_ca_proj.astype(BF16), w_out.astype(BF16),
                    norm_ffn.reshape(1, D_MODEL), _row_tile(n, 512))

    keys_bf = peer_sub_keys.reshape(2 * PEER_HEADS, PEER_N_KEYS, PEER_HALF).astype(BF16)
    idx, gate = _route(h2, peer_w_query.astype(BF16), keys_bf, _row_tile(n, 128))

    tbg = _row_tile(n, 64)
    c = _peer_dot(idx, h2.reshape(n * SUBLANES, LANES), gate, _pack_table(peer_u), tbg)
    y = _peer_axpy(idx, c, x1.reshape(n * SUBLANES, LANES), final_gain.reshape(SUBLANES, LANES),
                   _pack_table(peer_v), tbg)
    return y.reshape(n, D_MODEL)


def kernel(x, norm_mix, w_in, w_sb_proj, w_ca_proj, ca_rel_bias, w_out, norm_ffn,
           peer_w_query, peer_sub_keys, peer_u, peer_v, norm_final):
    b, s, d = x.shape
    depth = norm_mix.shape[0]
    assert d == D_MODEL and depth == 1
    y = _layer(x.reshape(b * s, d), b, s, norm_mix[0], w_in[0], w_sb_proj[0], w_ca_proj[0],
               ca_rel_bias[0], w_out[0], norm_ffn[0], peer_w_query[0], peer_sub_keys[0],
               peer_u[0], peer_v[0], norm_final)
    return y.reshape(b, s, d)
```

```python
import functools
import math

import numpy as np
import jax
import jax.numpy as jnp
from jax import lax
from jax.experimental import pallas as pl
from jax.experimental.pallas import tpu as pltpu

D_MODEL = 1024
HEAD_DIM = 64
N_HEADS = 8
ATT_WIDTH = N_HEADS * HEAD_DIM
IN_COLS = 6 * ATT_WIDTH + 2 * D_MODEL
CHUNK = 64
CA_LEFT_CHUNKS = 8
CA_PAD = CA_LEFT_CHUNKS * CHUNK
REL_CLIP = 128
PEER_HEADS = 8
PEER_N_KEYS = 128
PEER_HALF = 128
PEER_TOPK = 16
PEER_SLOTS = PEER_HEADS * PEER_TOPK
RMS_EPS = 1e-6
NEG_INF = -1e30

LANES = 128
SUBLANES = 8
PACK_ROWS = D_MODEL // 2 // LANES

VMEM_LIMIT = 56 * 1024 * 1024

F32 = jnp.float32
BF16 = jnp.bfloat16
NT_DIMS = (((1,), (1,)), ((), ()))


def _params(sem, vmem=VMEM_LIMIT):
    return pltpu.CompilerParams(dimension_semantics=sem, vmem_limit_bytes=vmem)


def _inproj_kernel(x_ref, g_ref, w_ref, o_ref):
    x = x_ref[...]
    ms = jnp.mean(x * x, axis=-1, keepdims=True)
    h = (x * lax.rsqrt(ms + RMS_EPS) * g_ref[...]).astype(BF16)
    for c in range(IN_COLS // D_MODEL):
        cols = slice(c * D_MODEL, (c + 1) * D_MODEL)
        o_ref[:, cols] = jnp.dot(h, w_ref[:, cols], preferred_element_type=F32).astype(o_ref.dtype)


def _inproj(x2, g, w_bf, tm):
    n = x2.shape[0]
    return pl.pallas_call(
        _inproj_kernel,
        grid=(n // tm,),
        in_specs=[
            pl.BlockSpec((tm, D_MODEL), lambda i: (i, 0)),
            pl.BlockSpec((1, D_MODEL), lambda i: (0, 0)),
            pl.BlockSpec((D_MODEL, IN_COLS), lambda i: (0, 0)),
        ],
        out_specs=pl.BlockSpec((tm, IN_COLS), lambda i: (i, 0)),
        out_shape=jax.ShapeDtypeStruct((n, IN_COLS), BF16),
        compiler_params=_params(("arbitrary",)),
        name="inproj",
    )(x2, g, w_bf)


def _sb_kernel(q_ref, k_ref, v_ref, o_ref, acc_ref, r_ref, z_scr, lat_scr, *, tq, tk):
    i = pl.program_id(2)
    n_chunks = tq // tk
    lane = lax.broadcasted_iota(jnp.int32, (1, LANES), 1)
    head_mask = (lane < HEAD_DIM, lane >= HEAD_DIM)
    kk = lax.broadcasted_iota(jnp.int32, (tk, tk), 0)
    ss = lax.broadcasted_iota(jnp.int32, (tk, tk), 1)
    stri = jnp.where(kk > ss, 1.0, 0.0).astype(BF16)
    causal = ss < kk
    sign = jnp.uint32(0x80000000)
    rs_col = 2 * tk

    acc_ref[...] = jnp.zeros_like(acc_ref)
    r_ref[...] = jnp.zeros_like(r_ref)

    def rows(c):
        return slice(c * tk, (c + 1) * tk)

    def sweep(j, chunks, diag_chunk):
        start = pl.multiple_of(j * tk, tk)
        k = k_ref[0, pl.ds(start, tk), :]
        v = v_ref[0, pl.ds(start, tk), :]
        zero = jnp.zeros_like(k)
        k2 = jnp.concatenate([jnp.where(head_mask[0], k, zero), jnp.where(head_mask[1], k, zero)], axis=0)
        v2 = jnp.concatenate([jnp.where(head_mask[0], v, zero), jnp.where(head_mask[1], v, zero)], axis=0)
        for c in chunks:
            z_scr[c] = lax.dot_general(q_ref[0, rows(c), :], k2, NT_DIMS, preferred_element_type=F32)
        for c in chunks:
            for h in range(2):
                cols = slice(h * tk, (h + 1) * tk)
                z = z_scr[c, :, cols]
                neg_abs = lax.bitcast_convert_type(lax.bitcast_convert_type(z, jnp.uint32) | sign, F32)
                sp = jnp.maximum(z, 0.0) + jnp.log2(1.0 + jnp.exp2(neg_abs))
                if c == diag_chunk:
                    sp = jnp.where(causal, sp, 0.0)
                z_scr[c, :, cols] = z - sp
                later = jnp.dot(sp.astype(BF16), stri, preferred_element_type=F32)
                lat_scr[c, :, cols] = later
                rsum = later[:, 0:1] + sp[:, 0:1]
                lat_scr[c, :, rs_col + h * LANES: rs_col + (h + 1) * LANES] = jnp.broadcast_to(rsum, (tk, LANES))
        for c in chunks:
            ws = []
            for h in range(2):
                cols = slice(h * tk, (h + 1) * tk)
                r = r_ref[h, rows(c), :]
                logw = z_scr[c, :, cols] - lat_scr[c, :, cols] - jnp.concatenate([r] * (tk // LANES), axis=1)
                if c == diag_chunk:
                    logw = jnp.where(causal, logw, NEG_INF)
                ws.append(jnp.exp2(logw).astype(BF16))
                r_ref[h, rows(c), :] = r + lat_scr[c, :, rs_col + h * LANES: rs_col + (h + 1) * LANES]
            acc_ref[rows(c), :] += jnp.dot(jnp.concatenate(ws, axis=1), v2, preferred_element_type=F32)

    base = i * n_chunks
    for d in range(n_chunks - 1, -1, -1):
        sweep(base + d, list(range(d, n_chunks)), d)

    def body(jj, carry):
        sweep(base - 1 - jj, list(range(n_chunks)), None)
        return carry

    lax.fori_loop(0, base, body, 0)
    o_ref[0] = acc_ref[...].astype(o_ref.dtype)


def _sb_attention(proj3, tq, tk):
    b, s, _ = proj3.shape
    n_pairs = ATT_WIDTH // LANES
    kern = functools.partial(_sb_kernel, tq=tq, tk=tk)
    return pl.pallas_call(
        kern,
        grid=(b, n_pairs, s // tq),
        in_specs=[
            pl.BlockSpec((1, tq, LANES), lambda bi, p, i: (bi, i, p)),
            pl.BlockSpec((1, s, LANES), lambda bi, p, i: (bi, 0, n_pairs + p)),
            pl.BlockSpec((1, s, LANES), lambda bi, p, i: (bi, 0, 2 * n_pairs + p)),
        ],
        out_specs=pl.BlockSpec((1, tq, LANES), lambda bi, p, i: (bi, i, p)),
        out_shape=jax.ShapeDtypeStruct((b, s, ATT_WIDTH), BF16),
        scratch_shapes=[
            pltpu.VMEM((tq, LANES), F32),
            pltpu.VMEM((2, tq, LANES), F32),
            pltpu.VMEM((tq // tk, tk, 2 * tk), F32),
            pltpu.VMEM((tq // tk, tk, 2 * tk + 2 * LANES), F32),
        ],
        compiler_params=_params(("arbitrary", "arbitrary", "arbitrary")),
        name="sb_attention",
    )(proj3, proj3, proj3)


def _ca_kernel(q_ref, k_ref, v_ref, b_ref, o_ref, *, tq, win, rc):
    i = pl.program_id(2)
    ws = pl.multiple_of(jnp.maximum(i * tq - CA_PAD, 0), tq)
    lane = lax.broadcasted_iota(jnp.int32, (1, LANES), 1)
    head_mask = (lane < HEAD_DIM, lane >= HEAD_DIM)
    k = k_ref[0, pl.ds(ws, win), :]
    v = v_ref[0, pl.ds(ws, win), :]
    for c in range(tq // rc):
        rows = slice(c * rc, (c + 1) * rc)
        q = q_ref[0, rows, :]
        out = None
        for h in range(2):
            kh = jnp.where(head_mask[h], k, jnp.zeros_like(k))
            vh = jnp.where(head_mask[h], v, jnp.zeros_like(v))
            z = lax.dot_general(q, kh, NT_DIMS, preferred_element_type=F32) + b_ref[0, h, rows, :]
            m = jnp.max(z, axis=1, keepdims=True)
            p = jnp.exp(z - m)
            l = jnp.sum(p, axis=1, keepdims=True)
            o = jnp.dot(p.astype(BF16), vh, preferred_element_type=F32) / l
            out = o if out is None else out + o
        o_ref[0, rows, :] = out.astype(o_ref.dtype)


def _ca_bias(rel_table, tq, win):
    r = np.arange(tq)[:, None]
    c = np.arange(win)[None, :]
    n_heads = rel_table.shape[0]
    span = tq + win - 1
    table = rel_table.astype(F32)
    out = []
    for var in range(3):
        off = var * tq - max(var * tq - CA_PAD, 0)
        relc = (off + r) // CHUNK - c // CHUNK
        valid = (relc >= 0) & (relc <= CA_LEFT_CHUNKS)
        dist = np.concatenate([off - np.arange(win), off + np.arange(tq - 1, 0, -1)])
        ring = table[:, np.clip(dist, -REL_CLIP, REL_CLIP) + REL_CLIP]
        toep = jnp.tile(ring, (1, tq))[:, :tq * (span - 1)].reshape(n_heads, tq, span - 1)[:, :, :win]
        out.append(jnp.where(valid[None], toep, NEG_INF))
    return jnp.stack(out)


def _ca_attention(proj3, bias, tq, win, rc):
    b, s, _ = proj3.shape
    n_pairs = ATT_WIDTH // LANES
    kern = functools.partial(_ca_kernel, tq=tq, win=win, rc=rc)
    return pl.pallas_call(
        kern,
        grid=(b, n_pairs, s // tq),
        in_specs=[
            pl.BlockSpec((1, tq, LANES), lambda bi, p, i: (bi, i, 3 * n_pairs + p)),
            pl.BlockSpec((1, s, LANES), lambda bi, p, i: (bi, 0, 4 * n_pairs + p)),
            pl.BlockSpec((1, s, LANES), lambda bi, p, i: (bi, 0, 5 * n_pairs + p)),
            pl.BlockSpec((1, 2, tq, win), lambda bi, p, i: (jnp.minimum(i, 2), p, 0, 0)),
        ],
        out_specs=pl.BlockSpec((1, tq, LANES), lambda bi, p, i: (bi, i, p)),
        out_shape=jax.ShapeDtypeStruct((b, s, ATT_WIDTH), BF16),
        compiler_params=_params(("arbitrary", "arbitrary", "arbitrary")),
        name="ca_attention",
    )(proj3, proj3, proj3, bias)


def _merge_kernel(osb_ref, oca_ref, gsb_ref, gca_ref, x_ref, psb_ref, pca_ref, wout_ref, nf_ref,
                  x1_ref, h2_ref):
    br_sb = jnp.dot(osb_ref[...], psb_ref[...], preferred_element_type=F32)
    br_ca = jnp.dot(oca_ref[...], pca_ref[...], preferred_element_type=F32)
    merged = (jax.nn.sigmoid(gsb_ref[...].astype(F32)) * br_sb
              + jax.nn.sigmoid(gca_ref[...].astype(F32)) * br_ca)
    x1 = x_ref[...] + jnp.dot(merged.astype(BF16), wout_ref[...], preferred_element_type=F32)
    x1_ref[...] = x1
    ms = jnp.mean(x1 * x1, axis=-1, keepdims=True)
    h2_ref[...] = x1 * lax.rsqrt(ms + RMS_EPS) * nf_ref[...]


def _merge(o_sb, o_ca, proj, x2, psb, pca, wout, nf, tm):
    n = x2.shape[0]
    gate_blk = 3 * ATT_WIDTH * 2 // D_MODEL
    return pl.pallas_call(
        _merge_kernel,
        grid=(n // tm,),
        in_specs=[
            pl.BlockSpec((tm, ATT_WIDTH), lambda i: (i, 0)),
            pl.BlockSpec((tm, ATT_WIDTH), lambda i: (i, 0)),
            pl.BlockSpec((tm, D_MODEL), lambda i: (i, gate_blk)),
            pl.BlockSpec((tm, D_MODEL), lambda i: (i, gate_blk + 1)),
            pl.BlockSpec((tm, D_MODEL), lambda i: (i, 0)),
            pl.BlockSpec((ATT_WIDTH, D_MODEL), lambda i: (0, 0)),
            pl.BlockSpec((ATT_WIDTH, D_MODEL), lambda i: (0, 0)),
            pl.BlockSpec((D_MODEL, D_MODEL), lambda i: (0, 0)),
            pl.BlockSpec((1, D_MODEL), lambda i: (0, 0)),
        ],
        out_specs=[
            pl.BlockSpec((tm, D_MODEL), lambda i: (i, 0)),
            pl.BlockSpec((tm, D_MODEL), lambda i: (i, 0)),
        ],
        out_shape=[jax.ShapeDtypeStruct((n, D_MODEL), F32), jax.ShapeDtypeStruct((n, D_MODEL), F32)],
        compiler_params=_params(("arbitrary",)),
        name="merge",
    )(o_sb, o_ca, proj, proj, x2, psb, pca, wout, nf)


def _peer_candidates():
    return [(a, b) for a in range(PEER_TOPK) for b in range(PEER_TOPK) if (a + 1) * (b + 1) <= PEER_TOPK]


def _extract_top(s, iota, n_rounds, sentinel):
    vals, ids = [], []
    for _ in range(n_rounds):
        m = jnp.max(s, axis=0, keepdims=True)
        am = jnp.min(jnp.where(s == m, iota, sentinel), axis=0, keepdims=True)
        vals.append(m)
        ids.append(am)
        s = jnp.where(iota == am, -jnp.inf, s)
    return vals, ids


def _route_kernel(h_ref, wq_ref, keys_ref, idx_ref, g_ref, cv_scr, ci_scr, ri_scr, rg_scr, *, tb):
    cands = _peer_candidates()
    n_cand = len(cands)
    n_rows = cv_scr.shape[0]
    h = h_ref[...].astype(BF16)
    iota_n = lax.broadcasted_iota(jnp.int32, (PEER_N_KEYS, tb), 0)
    iota_c = lax.broadcasted_iota(jnp.int32, (n_rows, tb), 0)
    cv_scr[n_cand:, :] = jnp.full((n_rows - n_cand, tb), -jnp.inf, F32)
    ci_scr[n_cand:, :] = jnp.zeros((n_rows - n_cand, tb), jnp.int32)
    for hd in range(PEER_HEADS):
        tops = []
        for p in range(2):
            hp = hd * 2 + p
            q = jnp.dot(h, wq_ref[:, hp * PEER_HALF:(hp + 1) * PEER_HALF],
                        preferred_element_type=F32).astype(BF16)
            s = lax.dot_general(keys_ref[hp], q, NT_DIMS, preferred_element_type=F32)
            tops.append(_extract_top(s, iota_n, PEER_TOPK, PEER_N_KEYS))
        (v1, i1), (v2, i2) = tops
        i1s = [x * PEER_N_KEYS for x in i1]
        for r, (a, b) in enumerate(cands):
            cv_scr[r:r + 1, :] = v1[a] + v2[b]
            ci_scr[r:r + 1, :] = i1s[a] + i2[b]
        cv = cv_scr[...]
        ci = ci_scr[...]
        best = []
        for r in range(PEER_TOPK):
            m = jnp.max(cv, axis=0, keepdims=True)
            am = jnp.min(jnp.where(cv == m, iota_c, n_rows), axis=0, keepdims=True)
            hit = iota_c == am
            e = jnp.sum(jnp.where(hit, ci, 0), axis=0, keepdims=True)
            cv = jnp.where(hit, -jnp.inf, cv)
            best.append(m)
            ri_scr[hd * PEER_TOPK + r: hd * PEER_TOPK + r + 1, :] = e * PACK_ROWS
        ex = [jnp.exp(bv - best[0]) for bv in best]
        tot = ex[0]
        for t in ex[1:]:
            tot = tot + t
        inv = 1.0 / tot
        for r in range(PEER_TOPK):
            rg_scr[hd * PEER_TOPK + r: hd * PEER_TOPK + r + 1, :] = ex[r] * inv
    idx_ref[...] = ri_scr[...].T
    g_ref[...] = rg_scr[...].T


def _route(h2, wq_bf, keys_bf, tb):
    n = h2.shape[0]
    n_rows = -(-len(_peer_candidates()) // SUBLANES) * SUBLANES
    kern = functools.partial(_route_kernel, tb=tb)
    return pl.pallas_call(
        kern,
        grid=(n // tb,),
        in_specs=[
            pl.BlockSpec((tb, D_MODEL), lambda i: (i, 0)),
            pl.BlockSpec((D_MODEL, 2 * PEER_HEADS * PEER_HALF), lambda i: (0, 0)),
            pl.BlockSpec((2 * PEER_HEADS, PEER_N_KEYS, PEER_HALF), lambda i: (0, 0, 0)),
        ],
        out_specs=[
            pl.BlockSpec((tb, PEER_SLOTS), lambda i: (i, 0)),
            pl.BlockSpec((tb, PEER_SLOTS), lambda i: (i, 0)),
        ],
        out_shape=[jax.ShapeDtypeStruct((n, PEER_SLOTS), jnp.int32),
                   jax.ShapeDtypeStruct((n, PEER_SLOTS), F32)],
        scratch_shapes=[
            pltpu.VMEM((n_rows, tb), F32),
            pltpu.VMEM((n_rows, tb), jnp.int32),
            pltpu.VMEM((PEER_SLOTS, tb), jnp.int32),
            pltpu.VMEM((PEER_SLOTS, tb), F32),
        ],
        compiler_params=_params(("arbitrary",)),
        name="peer_route",
    )(h2, wq_bf, keys_bf)


def _pack_table(t):
    e = t.shape[0]
    bits = lax.bitcast_convert_type(t.astype(BF16), jnp.uint16).astype(jnp.uint32)
    bits = bits.reshape(e, PACK_ROWS, 2, LANES)
    packed = (bits[:, :, 1, :] << 16) | bits[:, :, 0, :]
    return packed.reshape(e * PACK_ROWS, LANES)


def _table_spec(n_rows):
    return pl.BlockSpec((n_rows, LANES), lambda i: (0, 0), pipeline_mode=pl.Buffered(1))


def _gather_rows(idx_ref, t, tab_ref, buf):
    for k in range(PEER_SLOTS):
        e = pl.multiple_of(idx_ref[t, k], PACK_ROWS)
        buf[k * PACK_ROWS:(k + 1) * PACK_ROWS, :] = tab_ref[pl.ds(e, PACK_ROWS), :]


def _gathered_bf16(buf):
    return pltpu.bitcast(buf[...], BF16)


def _chunk_pattern():
    shape = (2 * SUBLANES, PEER_SLOTS * SUBLANES)
    row = lax.broadcasted_iota(jnp.int32, shape, 0)
    col = lax.broadcasted_iota(jnp.int32, shape, 1)
    return (row & (SUBLANES - 1)) == (col & (SUBLANES - 1))


def _split_bf16(x):
    hi = x.astype(BF16)
    return hi, (x - hi.astype(F32)).astype(BF16)


def _token_loop(idx_ref, tab_ref, bufs, tb, unroll, compute):
    _gather_rows(idx_ref, 0, tab_ref, bufs[0])

    def body(i, carry):
        for u in range(unroll):
            t = i * unroll + u
            _gather_rows(idx_ref, jnp.minimum(t + 1, tb - 1), tab_ref, bufs[(u + 1) % 2])
            compute(t, bufs[u % 2])
        return carry

    lax.fori_loop(0, tb // unroll, body, 0)


def _peer_dot_kernel(idx_ref, h_ref, g_ref, tab_ref, c_ref, buf0, buf1, part_scr, *, tb, unroll):
    pattern = _chunk_pattern()

    def compute(t, buf):
        row = pl.ds(pl.multiple_of(t * SUBLANES, SUBLANES), SUBLANES)
        h16 = jnp.concatenate(_split_bf16(h_ref[row, :]), axis=0)
        r = lax.dot_general(h16, _gathered_bf16(buf), NT_DIMS, preferred_element_type=F32)
        r = jnp.where(pattern, r, 0.0)
        part_scr[row, :] = r[:SUBLANES] + r[SUBLANES:]

    _token_loop(idx_ref, tab_ref, (buf0, buf1), tb, unroll, compute)

    n_part = PEER_SLOTS * SUBLANES
    slot_of = lax.broadcasted_iota(jnp.int32, (n_part, PEER_SLOTS), 0) // SUBLANES
    fold = jnp.where(slot_of == lax.broadcasted_iota(jnp.int32, (n_part, PEER_SLOTS), 1), 1.0, 0.0).astype(BF16)
    p_hi, p_lo = _split_bf16(part_scr[...])
    a8 = (jnp.dot(p_hi, fold, preferred_element_type=F32)
          + jnp.dot(p_lo, fold, preferred_element_type=F32))
    a = jnp.sum(a8.reshape(tb, SUBLANES, PEER_SLOTS), axis=1)
    c_ref[...] = g_ref[...] * (0.5 * a * (1.0 + lax.erf(a * (1.0 / math.sqrt(2.0)))))


def _peer_dot(idx, h2r, gate, tab, tb, unroll):
    n = idx.shape[0]
    kern = functools.partial(_peer_dot_kernel, tb=tb, unroll=unroll)
    return pl.pallas_call(
        kern,
        grid=(n // tb,),
        in_specs=[
            pl.BlockSpec((tb, PEER_SLOTS), lambda i: (i, 0), memory_space=pltpu.SMEM),
            pl.BlockSpec((tb * SUBLANES, LANES), lambda i: (i, 0)),
            pl.BlockSpec((tb, PEER_SLOTS), lambda i: (i, 0)),
            _table_spec(tab.shape[0]),
        ],
        out_specs=pl.BlockSpec((tb, PEER_SLOTS), lambda i: (i, 0)),
        out_shape=jax.ShapeDtypeStruct((n, PEER_SLOTS), F32),
        scratch_shapes=[pltpu.VMEM((PEER_SLOTS * PACK_ROWS, LANES), jnp.uint32),
                        pltpu.VMEM((PEER_SLOTS * PACK_ROWS, LANES), jnp.uint32),
                        pltpu.VMEM((tb * SUBLANES, PEER_SLOTS * SUBLANES), F32)],
        compiler_params=_params(("arbitrary",)),
        name="peer_dot",
    )(idx, h2r, gate, tab)


def _peer_axpy_kernel(idx_ref, c_ref, x_ref, gf_ref, tab_ref, y_ref, buf0, buf1, cexp_scr, *, tb, unroll):
    pattern = _chunk_pattern()
    n_part = PEER_SLOTS * SUBLANES
    slot_of = lax.broadcasted_iota(jnp.int32, (PEER_SLOTS, n_part), 1) // SUBLANES
    spread = jnp.where(slot_of == lax.broadcasted_iota(jnp.int32, (PEER_SLOTS, n_part), 0), 1.0, 0.0).astype(BF16)
    c_hi, c_lo = _split_bf16(c_ref[...])
    cexp_scr[0:tb, :] = jnp.dot(c_hi, spread, preferred_element_type=F32)
    cexp_scr[tb:2 * tb, :] = jnp.dot(c_lo, spread, preferred_element_type=F32)

    def compute(t, buf):
        row = pl.ds(pl.multiple_of(t * SUBLANES, SUBLANES), SUBLANES)
        coef = jnp.concatenate([jnp.broadcast_to(cexp_scr[pl.ds(t, 1), :], (SUBLANES, n_part)),
                                jnp.broadcast_to(cexp_scr[pl.ds(tb + t, 1), :], (SUBLANES, n_part))], axis=0)
        coef = jnp.where(pattern, coef, 0.0).astype(BF16)
        o16 = jnp.dot(coef, _gathered_bf16(buf), preferred_element_type=F32)
        x2 = x_ref[row, :] + (o16[:SUBLANES] + o16[SUBLANES:])
        ss = jnp.sum(jnp.sum(x2 * x2, axis=1, keepdims=True), axis=0, keepdims=True)
        y_ref[row, :] = x2 * lax.rsqrt(ss * (1.0 / D_MODEL) + RMS_EPS) * gf_ref[...]

    _token_loop(idx_ref, tab_ref, (buf0, buf1), tb, unroll, compute)


def _peer_axpy(idx, c, x1r, gfr, tab, tb, unroll):
    n = idx.shape[0]
    kern = functools.partial(_peer_axpy_kernel, tb=tb, unroll=unroll)
    return pl.pallas_call(
        kern,
        grid=(n // tb,),
        in_specs=[
            pl.BlockSpec((tb, PEER_SLOTS), lambda i: (i, 0), memory_space=pltpu.SMEM),
            pl.BlockSpec((tb, PEER_SLOTS), lambda i: (i, 0)),
            pl.BlockSpec((tb * SUBLANES, LANES), lambda i: (i, 0)),
            pl.BlockSpec((SUBLANES, LANES), lambda i: (0, 0)),
            _table_spec(tab.shape[0]),
        ],
        out_specs=pl.BlockSpec((tb * SUBLANES, LANES), lambda i: (i, 0)),
        out_shape=jax.ShapeDtypeStruct((n * SUBLANES, LANES), F32),
        scratch_shapes=[pltpu.VMEM((PEER_SLOTS * PACK_ROWS, LANES), jnp.uint32),
                        pltpu.VMEM((PEER_SLOTS * PACK_ROWS, LANES), jnp.uint32),
                        pltpu.VMEM((2 * tb, PEER_SLOTS * SUBLANES), F32)],
        compiler_params=_params(("arbitrary",)),
        name="peer_axpy",
    )(idx, c, x1r, gfr, tab)


def _row_tile(n, want):
    t = min(want, n)
    assert n % t == 0
    return t


def _layer(x2, b, s, norm_mix, w_in, w_sb_proj, w_ca_proj, ca_rel_bias, w_out, norm_ffn,
           peer_w_query, peer_sub_keys, peer_u, peer_v, final_gain):
    n = b * s
    scale = 1.0 / math.sqrt(HEAD_DIM)
    col = np.arange(IN_COLS)
    col_scale = np.where(col < ATT_WIDTH, scale * math.log2(math.e),
                         np.where((col >= 3 * ATT_WIDTH) & (col < 4 * ATT_WIDTH), scale, 1.0))
    w_in_bf = (w_in * jnp.asarray(col_scale, F32)[None, :]).astype(BF16)

    proj = _inproj(x2, norm_mix.reshape(1, D_MODEL), w_in_bf, _row_tile(n, 512))
    proj3 = proj.reshape(b, s, IN_COLS)

    tq = 256
    assert s % tq == 0 and s >= tq + CA_PAD
    sb_tq = min(1024, s)
    assert s % sb_tq == 0
    o_sb = _sb_attention(proj3, sb_tq, 256)
    win = tq + CA_PAD
    o_ca = _ca_attention(proj3, _ca_bias(ca_rel_bias, tq, win), tq, win, 128)

    x1, h2 = _merge(o_sb.reshape(n, ATT_WIDTH), o_ca.reshape(n, ATT_WIDTH), proj, x2,
                    w_sb_proj.astype(BF16), w_ca_proj.astype(BF16), w_out.astype(BF16),
                    norm_ffn.reshape(1, D_MODEL), _row_tile(n, 512))

    keys_bf = peer_sub_keys.reshape(2 * PEER_HEADS, PEER_N_KEYS, PEER_HALF).astype(BF16)
    idx, gate = _route(h2, peer_w_query.astype(BF16), keys_bf, _row_tile(n, 128))

    tbg = _row_tile(n, 64)
    c = _peer_dot(idx, h2.reshape(n * SUBLANES, LANES), gate, _pack_table(peer_u), tbg, 4)
    y = _peer_axpy(idx, c, x1.reshape(n * SUBLANES, LANES), final_gain.reshape(SUBLANES, LANES),
                   _pack_table(peer_v), tbg, 4)
    return y.reshape(n, D_MODEL)


def kernel(x, norm_mix, w_in, w_sb_proj, w_ca_proj, ca_rel_bias, w_out, norm_ffn,
           peer_w_query, peer_sub_keys, peer_u, peer_v, norm_final):
    b, s, d = x.shape
    depth = norm_mix.shape[0]
    assert d == D_MODEL and depth == 1
    y = _layer(x.reshape(b * s, d), b, s, norm_mix[0], w_in[0], w_sb_proj[0], w_ca_proj[0],
               ca_rel_bias[0], w_out[0], norm_ffn[0], peer_w_query[0], peer_sub_keys[0],
               peer_u[0], peer_v[0], norm_final)
    return y.reshape(b, s, d)
```

```python
import functools
import math

import numpy as np
import jax
import jax.numpy as jnp
from jax import lax
from jax.experimental import pallas as pl
from jax.experimental.pallas import tpu as pltpu

D_MODEL = 1024
HEAD_DIM = 64
N_HEADS = 8
ATT_WIDTH = N_HEADS * HEAD_DIM
IN_COLS = 6 * ATT_WIDTH + 2 * D_MODEL
CHUNK = 64
CA_LEFT_CHUNKS = 8
CA_PAD = CA_LEFT_CHUNKS * CHUNK
REL_CLIP = 128
PEER_HEADS = 8
PEER_N_KEYS = 128
PEER_HALF = 128
PEER_TOPK = 16
PEER_SLOTS = PEER_HEADS * PEER_TOPK
RMS_EPS = 1e-6
NEG_INF = -1e30

LANES = 128
SUBLANES = 8
PACK_ROWS = D_MODEL // 2 // LANES

VMEM_LIMIT = 56 * 1024 * 1024

F32 = jnp.float32
BF16 = jnp.bfloat16
NT_DIMS = (((1,), (1,)), ((), ()))


def _params(sem, vmem=VMEM_LIMIT):
    return pltpu.CompilerParams(dimension_semantics=sem, vmem_limit_bytes=vmem)


def _inproj_kernel(x_ref, g_ref, w_ref, o_ref):
    x = x_ref[...]
    ms = jnp.mean(x * x, axis=-1, keepdims=True)
    h = (x * lax.rsqrt(ms + RMS_EPS) * g_ref[...]).astype(BF16)
    for c in range(IN_COLS // D_MODEL):
        cols = slice(c * D_MODEL, (c + 1) * D_MODEL)
        o_ref[:, cols] = jnp.dot(h, w_ref[:, cols], preferred_element_type=F32).astype(o_ref.dtype)


def _inproj(x2, g, w_bf, tm):
    n = x2.shape[0]
    return pl.pallas_call(
        _inproj_kernel,
        grid=(n // tm,),
        in_specs=[
            pl.BlockSpec((tm, D_MODEL), lambda i: (i, 0)),
            pl.BlockSpec((1, D_MODEL), lambda i: (0, 0)),
            pl.BlockSpec((D_MODEL, IN_COLS), lambda i: (0, 0)),
        ],
        out_specs=pl.BlockSpec((tm, IN_COLS), lambda i: (i, 0)),
        out_shape=jax.ShapeDtypeStruct((n, IN_COLS), BF16),
        compiler_params=_params(("arbitrary",)),
        name="inproj",
    )(x2, g, w_bf)


def _sb_kernel(q_ref, k_ref, v_ref, o_ref, acc_ref, r_ref, z_scr, lat_scr, *, tq, tk):
    i = pl.program_id(2)
    n_chunks = tq // tk
    lane = lax.broadcasted_iota(jnp.int32, (1, LANES), 1)
    head_mask = (lane < HEAD_DIM, lane >= HEAD_DIM)
    kk = lax.broadcasted_iota(jnp.int32, (tk, tk), 0)
    ss = lax.broadcasted_iota(jnp.int32, (tk, tk), 1)
    stri = jnp.where(kk > ss, 1.0, 0.0).astype(BF16)
    causal = ss < kk
    sign = jnp.uint32(0x80000000)

    acc_ref[...] = jnp.zeros_like(acc_ref)
    r_ref[...] = jnp.zeros_like(r_ref)

    def rows(c):
        return slice(c * tk, (c + 1) * tk)

    def sweep(j, chunks, diag_chunk):
        start = pl.multiple_of(j * tk, tk)
        k = k_ref[0, pl.ds(start, tk), :]
        v = v_ref[0, pl.ds(start, tk), :]
        zero = jnp.zeros_like(k)
        k2 = jnp.concatenate([jnp.where(head_mask[0], k, zero), jnp.where(head_mask[1], k, zero)], axis=0)
        v2 = jnp.concatenate([jnp.where(head_mask[0], v, zero), jnp.where(head_mask[1], v, zero)], axis=0)
        for c in chunks:
            z_scr[c] = lax.dot_general(q_ref[0, rows(c), :], k2, NT_DIMS, preferred_element_type=F32)
        for c in chunks:
            for h in range(2):
                cols = slice(h * tk, (h + 1) * tk)
                z = z_scr[c, :, cols]
                neg_abs = lax.bitcast_convert_type(lax.bitcast_convert_type(z, jnp.uint32) | sign, F32)
                sp = jnp.maximum(z, 0.0) + jnp.log2(1.0 + jnp.exp2(neg_abs))
                if c == diag_chunk:
                    sp = jnp.where(causal, sp, 0.0)
                z_scr[c, :, cols] = z - sp
                r = r_ref[h, rows(c), :]
                later = jnp.concatenate([r] * (tk // LANES), axis=1) + jnp.dot(
                    sp.astype(BF16), stri, preferred_element_type=F32)
                lat_scr[c, :, cols] = later
                r_ref[h, rows(c), :] = jnp.broadcast_to(later[:, 0:1] + sp[:, 0:1], (tk, LANES))
        for c in chunks:
            ws = []
            for h in range(2):
                cols = slice(h * tk, (h + 1) * tk)
                logw = z_scr[c, :, cols] - lat_scr[c, :, cols]
                if c == diag_chunk:
                    logw = jnp.where(causal, logw, NEG_INF)
                ws.append(jnp.exp2(logw).astype(BF16))
            acc_ref[rows(c), :] += jnp.dot(jnp.concatenate(ws, axis=1), v2, preferred_element_type=F32)

    base = i * n_chunks
    for d in range(n_chunks - 1, -1, -1):
        sweep(base + d, list(range(d, n_chunks)), d)

    def body(jj, carry):
        sweep(base - 1 - jj, list(range(n_chunks)), None)
        return carry

    lax.fori_loop(0, base, body, 0)
    o_ref[0] = acc_ref[...].astype(o_ref.dtype)


def _sb_attention(proj3, tq, tk):
    b, s, _ = proj3.shape
    n_pairs = ATT_WIDTH // LANES
    kern = functools.partial(_sb_kernel, tq=tq, tk=tk)
    return pl.pallas_call(
        kern,
        grid=(b, n_pairs, s // tq),
        in_specs=[
            pl.BlockSpec((1, tq, LANES), lambda bi, p, i: (bi, i, p)),
            pl.BlockSpec((1, s, LANES), lambda bi, p, i: (bi, 0, n_pairs + p)),
            pl.BlockSpec((1, s, LANES), lambda bi, p, i: (bi, 0, 2 * n_pairs + p)),
        ],
        out_specs=pl.BlockSpec((1, tq, LANES), lambda bi, p, i: (bi, i, p)),
        out_shape=jax.ShapeDtypeStruct((b, s, ATT_WIDTH), BF16),
        scratch_shapes=[
            pltpu.VMEM((tq, LANES), F32),
            pltpu.VMEM((2, tq, LANES), F32),
            pltpu.VMEM((tq // tk, tk, 2 * tk), F32),
            pltpu.VMEM((tq // tk, tk, 2 * tk), F32),
        ],
        compiler_params=_params(("arbitrary", "arbitrary", "arbitrary")),
        name="sb_attention",
    )(proj3, proj3, proj3)


def _ca_kernel(q_ref, k_ref, v_ref, b_ref, o_ref, z_scr, p_scr, *, tq, win, rc):
    i = pl.program_id(2)
    ws = pl.multiple_of(jnp.maximum(i * tq - CA_PAD, 0), tq)
    lane = lax.broadcasted_iota(jnp.int32, (1, LANES), 1)
    head_mask = (lane < HEAD_DIM, lane >= HEAD_DIM)
    k = k_ref[0, pl.ds(ws, win), :]
    v = v_ref[0, pl.ds(ws, win), :]
    zero = jnp.zeros_like(k)
    k2 = jnp.concatenate([jnp.where(head_mask[0], k, zero), jnp.where(head_mask[1], k, zero)], axis=0)
    z_scr[...] = lax.dot_general(q_ref[0], k2, NT_DIMS, preferred_element_type=F32)
    inv = [[None] * (tq // rc) for _ in range(2)]
    for h in range(2):
        for c in range(tq // rc):
            rows = slice(c * rc, (c + 1) * rc)
            cols = slice(h * win, (h + 1) * win)
            z = z_scr[rows, cols] + b_ref[0, h, rows, :]
            p = jnp.exp2(z - jnp.max(z, axis=1, keepdims=True))
            inv[h][c] = 1.0 / jnp.sum(p, axis=1, keepdims=True)
            p_scr[rows, cols] = p.astype(BF16)
    outs = [jnp.dot(p_scr[:, h * win:(h + 1) * win], jnp.where(head_mask[h], v, zero),
                    preferred_element_type=F32) for h in range(2)]
    o_ref[0] = (outs[0] * jnp.concatenate(inv[0], axis=0)
                + outs[1] * jnp.concatenate(inv[1], axis=0)).astype(o_ref.dtype)


def _ca_bias(rel_table, tq, win):
    r = np.arange(tq)[:, None]
    c = np.arange(win)[None, :]
    n_heads = rel_table.shape[0]
    span = tq + win - 1
    table = rel_table.astype(F32)
    out = []
    for var in range(3):
        off = var * tq - max(var * tq - CA_PAD, 0)
        relc = (off + r) // CHUNK - c // CHUNK
        valid = (relc >= 0) & (relc <= CA_LEFT_CHUNKS)
        dist = np.concatenate([off - np.arange(win), off + np.arange(tq - 1, 0, -1)])
        ring = table[:, np.clip(dist, -REL_CLIP, REL_CLIP) + REL_CLIP]
        toep = jnp.tile(ring, (1, tq))[:, :tq * (span - 1)].reshape(n_heads, tq, span - 1)[:, :, :win]
        out.append(jnp.where(valid[None], toep * math.log2(math.e), NEG_INF))
    return jnp.stack(out)


def _ca_attention(proj3, bias, tq, win, rc):
    b, s, _ = proj3.shape
    n_pairs = ATT_WIDTH // LANES
    kern = functools.partial(_ca_kernel, tq=tq, win=win, rc=rc)
    return pl.pallas_call(
        kern,
        grid=(b, n_pairs, s // tq),
        in_specs=[
            pl.BlockSpec((1, tq, LANES), lambda bi, p, i: (bi, i, 3 * n_pairs + p)),
            pl.BlockSpec((1, s, LANES), lambda bi, p, i: (bi, 0, 4 * n_pairs + p)),
            pl.BlockSpec((1, s, LANES), lambda bi, p, i: (bi, 0, 5 * n_pairs + p)),
            pl.BlockSpec((1, 2, tq, win), lambda bi, p, i: (jnp.minimum(i, 2), p, 0, 0)),
        ],
        out_specs=pl.BlockSpec((1, tq, LANES), lambda bi, p, i: (bi, i, p)),
        out_shape=jax.ShapeDtypeStruct((b, s, ATT_WIDTH), BF16),
        scratch_shapes=[pltpu.VMEM((tq, 2 * win), F32), pltpu.VMEM((tq, 2 * win), BF16)],
        compiler_params=_params(("arbitrary", "arbitrary", "arbitrary")),
        name="ca_attention",
    )(proj3, proj3, proj3, bias)


def _merge_kernel(osb_ref, oca_ref, gsb_ref, gca_ref, x_ref, psb_ref, pca_ref, wout_ref, nf_ref,
                  x1_ref, h2_ref):
    br_sb = jnp.dot(osb_ref[...], psb_ref[...], preferred_element_type=F32)
    br_ca = jnp.dot(oca_ref[...], pca_ref[...], preferred_element_type=F32)
    merged = (jax.nn.sigmoid(gsb_ref[...].astype(F32)) * br_sb
              + jax.nn.sigmoid(gca_ref[...].astype(F32)) * br_ca)
    x1 = x_ref[...] + jnp.dot(merged.astype(BF16), wout_ref[...], preferred_element_type=F32)
    x1_ref[...] = x1
    ms = jnp.mean(x1 * x1, axis=-1, keepdims=True)
    h2_ref[...] = x1 * lax.rsqrt(ms + RMS_EPS) * nf_ref[...]


def _merge(o_sb, o_ca, proj, x2, psb, pca, wout, nf, tm):
    n = x2.shape[0]
    gate_blk = 3 * ATT_WIDTH * 2 // D_MODEL
    return pl.pallas_call(
        _merge_kernel,
        grid=(n // tm,),
        in_specs=[
            pl.BlockSpec((tm, ATT_WIDTH), lambda i: (i, 0)),
            pl.BlockSpec((tm, ATT_WIDTH), lambda i: (i, 0)),
            pl.BlockSpec((tm, D_MODEL), lambda i: (i, gate_blk)),
            pl.BlockSpec((tm, D_MODEL), lambda i: (i, gate_blk + 1)),
            pl.BlockSpec((tm, D_MODEL), lambda i: (i, 0)),
            pl.BlockSpec((ATT_WIDTH, D_MODEL), lambda i: (0, 0)),
            pl.BlockSpec((ATT_WIDTH, D_MODEL), lambda i: (0, 0)),
            pl.BlockSpec((D_MODEL, D_MODEL), lambda i: (0, 0)),
            pl.BlockSpec((1, D_MODEL), lambda i: (0, 0)),
        ],
        out_specs=[
            pl.BlockSpec((tm, D_MODEL), lambda i: (i, 0)),
            pl.BlockSpec((tm, D_MODEL), lambda i: (i, 0)),
        ],
        out_shape=[jax.ShapeDtypeStruct((n, D_MODEL), F32), jax.ShapeDtypeStruct((n, D_MODEL), F32)],
        compiler_params=_params(("arbitrary",)),
        name="merge",
    )(o_sb, o_ca, proj, proj, x2, psb, pca, wout, nf)


def _peer_candidates():
    return [(a, b) for a in range(PEER_TOPK) for b in range(PEER_TOPK) if (a + 1) * (b + 1) <= PEER_TOPK]


def _extract_top(s, iota, n_rounds, sentinel):
    vals, ids = [], []
    for _ in range(n_rounds):
        m = jnp.max(s, axis=0, keepdims=True)
        cand = jnp.where(s == m, iota, sentinel)
        am = jnp.min(cand, axis=0, keepdims=True)
        vals.append(m)
        ids.append(am)
        s = jnp.where(cand == am, -jnp.inf, s)
    return vals, ids


def _route_kernel(h_ref, wq_ref, keys_ref, idx_ref, g_ref, cv_scr, ci_scr, ri_scr, rg_scr, *, tb):
    cands = _peer_candidates()
    n_cand = len(cands)
    n_rows = cv_scr.shape[0]
    h = h_ref[...].astype(BF16)
    iota_n = lax.broadcasted_iota(jnp.int32, (PEER_N_KEYS, tb), 0).astype(F32)
    iota_c = lax.broadcasted_iota(jnp.int32, (n_rows, tb), 0).astype(F32)
    cv_scr[n_cand:, :] = jnp.full((n_rows - n_cand, tb), -jnp.inf, F32)
    ci_scr[n_cand:, :] = jnp.zeros((n_rows - n_cand, tb), F32)
    for hd in range(PEER_HEADS):
        tops = []
        for p in range(2):
            hp = hd * 2 + p
            q = jnp.dot(h, wq_ref[:, hp * PEER_HALF:(hp + 1) * PEER_HALF],
                        preferred_element_type=F32).astype(BF16)
            s = lax.dot_general(keys_ref[hp], q, NT_DIMS, preferred_element_type=F32)
            tops.append(_extract_top(s, iota_n, PEER_TOPK, float(PEER_N_KEYS)))
        (v1, i1), (v2, i2) = tops
        i1s = [x * float(PEER_N_KEYS * PACK_ROWS) for x in i1]
        i2s = [x * float(PACK_ROWS) for x in i2]
        for r, (a, b) in enumerate(cands):
            cv_scr[r:r + 1, :] = v1[a] + v2[b]
            ci_scr[r:r + 1, :] = i1s[a] + i2s[b]
        cv = cv_scr[...]
        ci = ci_scr[...]
        best = []
        for r in range(PEER_TOPK):
            m = jnp.max(cv, axis=0, keepdims=True)
            cand = jnp.where(cv == m, iota_c, float(n_rows))
            hit = cand == jnp.min(cand, axis=0, keepdims=True)
            e = jnp.sum(jnp.where(hit, ci, 0.0), axis=0, keepdims=True)
            cv = jnp.where(hit, -jnp.inf, cv)
            best.append(m)
            ri_scr[hd * PEER_TOPK + r: hd * PEER_TOPK + r + 1, :] = e
        ex = [jnp.exp(bv - best[0]) for bv in best]
        tot = ex[0]
        for t in ex[1:]:
            tot = tot + t
        inv = 1.0 / tot
        for r in range(PEER_TOPK):
            rg_scr[hd * PEER_TOPK + r: hd * PEER_TOPK + r + 1, :] = ex[r] * inv
    idx_ref[...] = ri_scr[...].T.astype(jnp.int32)
    g_ref[...] = rg_scr[...].T


def _route(h2, wq_bf, keys_bf, tb):
    n = h2.shape[0]
    n_rows = -(-len(_peer_candidates()) // SUBLANES) * SUBLANES
    kern = functools.partial(_route_kernel, tb=tb)
    return pl.pallas_call(
        kern,
        grid=(n // tb,),
        in_specs=[
            pl.BlockSpec((tb, D_MODEL), lambda i: (i, 0)),
            pl.BlockSpec((D_MODEL, 2 * PEER_HEADS * PEER_HALF), lambda i: (0, 0)),
            pl.BlockSpec((2 * PEER_HEADS, PEER_N_KEYS, PEER_HALF), lambda i: (0, 0, 0)),
        ],
        out_specs=[
            pl.BlockSpec((tb, PEER_SLOTS), lambda i: (i, 0)),
            pl.BlockSpec((tb, PEER_SLOTS), lambda i: (i, 0)),
        ],
        out_shape=[jax.ShapeDtypeStruct((n, PEER_SLOTS), jnp.int32),
                   jax.ShapeDtypeStruct((n, PEER_SLOTS), F32)],
        scratch_shapes=[
            pltpu.VMEM((n_rows, tb), F32),
            pltpu.VMEM((n_rows, tb), F32),
            pltpu.VMEM((PEER_SLOTS, tb), F32),
            pltpu.VMEM((PEER_SLOTS, tb), F32),
        ],
        compiler_params=_params(("arbitrary",)),
        name="peer_route",
    )(h2, wq_bf, keys_bf)


def _pack_table(t):
    e = t.shape[0]
    bits = lax.bitcast_convert_type(t.astype(BF16), jnp.uint16).astype(jnp.uint32)
    bits = bits.reshape(e, PACK_ROWS, 2, LANES)
    packed = (bits[:, :, 1, :] << 16) | bits[:, :, 0, :]
    return packed.reshape(e * PACK_ROWS, LANES)


def _table_spec(n_rows):
    return pl.BlockSpec((n_rows, LANES), lambda i: (0, 0), pipeline_mode=pl.Buffered(1))


def _gather_rows(idx_ref, t, tab_ref, buf):
    for k in range(PEER_SLOTS):
        e = pl.multiple_of(idx_ref[t, k], PACK_ROWS)
        buf[k * PACK_ROWS:(k + 1) * PACK_ROWS, :] = tab_ref[pl.ds(e, PACK_ROWS), :]


def _gathered_bf16(buf):
    return pltpu.bitcast(buf[...], BF16)


def _chunk_pattern():
    shape = (2 * SUBLANES, PEER_SLOTS * SUBLANES)
    row = lax.broadcasted_iota(jnp.int32, shape, 0)
    col = lax.broadcasted_iota(jnp.int32, shape, 1)
    return (row & (SUBLANES - 1)) == (col & (SUBLANES - 1))


def _token_rows(t):
    if isinstance(t, int):
        return slice(t * SUBLANES, (t + 1) * SUBLANES)
    return pl.ds(pl.multiple_of(t * SUBLANES, SUBLANES), SUBLANES)


def _load_token_chunks(ref, t):
    row = ref[pl.ds(t, 1), :]
    return jnp.concatenate([row[:, j * LANES:(j + 1) * LANES] for j in range(SUBLANES)], axis=0)


def _store_token_chunks(ref, t, val):
    ref[pl.ds(t, 1), :] = jnp.concatenate([val[j:j + 1, :] for j in range(SUBLANES)], axis=1)


def _split_bf16(x):
    hi = x.astype(BF16)
    return hi, (x - hi.astype(F32)).astype(BF16)


def _token_loop(idx_ref, tab_ref, bufs, tb, unroll, compute):
    _gather_rows(idx_ref, 0, tab_ref, bufs[0])
    if unroll == tb:
        for t in range(tb):
            _gather_rows(idx_ref, min(t + 1, tb - 1), tab_ref, bufs[(t + 1) % 2])
            compute(t, bufs[t % 2])
        return

    def body(i, carry):
        for u in range(unroll):
            t = i * unroll + u
            _gather_rows(idx_ref, jnp.minimum(t + 1, tb - 1), tab_ref, bufs[(u + 1) % 2])
            compute(t, bufs[u % 2])
        return carry

    lax.fori_loop(0, tb // unroll, body, 0)


def _peer_dot_kernel(idx_ref, h_ref, g_ref, tab_ref, c_ref, buf0, buf1, part_scr, *, tb, unroll):
    pattern = _chunk_pattern()

    def compute(t, buf):
        row = _token_rows(t)
        h16 = jnp.concatenate(_split_bf16(h_ref[row, :]), axis=0)
        r = lax.dot_general(h16, _gathered_bf16(buf), NT_DIMS, preferred_element_type=F32)
        r = jnp.where(pattern, r, 0.0)
        part_scr[row, :] = r[:SUBLANES] + r[SUBLANES:]

    _token_loop(idx_ref, tab_ref, (buf0, buf1), tb, unroll, compute)

    n_part = PEER_SLOTS * SUBLANES
    slot_of = lax.broadcasted_iota(jnp.int32, (n_part, PEER_SLOTS), 0) // SUBLANES
    fold = jnp.where(slot_of == lax.broadcasted_iota(jnp.int32, (n_part, PEER_SLOTS), 1), 1.0, 0.0).astype(BF16)
    p_hi, p_lo = _split_bf16(part_scr[...])
    a8 = (jnp.dot(p_hi, fold, preferred_element_type=F32)
          + jnp.dot(p_lo, fold, preferred_element_type=F32))
    a = jnp.sum(a8.reshape(tb, SUBLANES, PEER_SLOTS), axis=1)
    c_ref[...] = g_ref[...] * (0.5 * a * (1.0 + lax.erf(a * (1.0 / math.sqrt(2.0)))))


def _peer_dot(idx, h2r, gate, tab, tb, unroll):
    n = idx.shape[0]
    kern = functools.partial(_peer_dot_kernel, tb=tb, unroll=unroll)
    return pl.pallas_call(
        kern,
        grid=(n // tb,),
        in_specs=[
            pl.BlockSpec((tb, PEER_SLOTS), lambda i: (i, 0), memory_space=pltpu.SMEM),
            pl.BlockSpec((tb * SUBLANES, LANES), lambda i: (i, 0)),
            pl.BlockSpec((tb, PEER_SLOTS), lambda i: (i, 0)),
            _table_spec(tab.shape[0]),
        ],
        out_specs=pl.BlockSpec((tb, PEER_SLOTS), lambda i: (i, 0)),
        out_shape=jax.ShapeDtypeStruct((n, PEER_SLOTS), F32),
        scratch_shapes=[pltpu.VMEM((PEER_SLOTS * PACK_ROWS, LANES), jnp.uint32),
                        pltpu.VMEM((PEER_SLOTS * PACK_ROWS, LANES), jnp.uint32),
                        pltpu.VMEM((tb * SUBLANES, PEER_SLOTS * SUBLANES), F32)],
        compiler_params=_params(("arbitrary",)),
        name="peer_dot",
    )(idx, h2r, gate, tab)


def _peer_axpy_kernel(idx_ref, c_ref, x_ref, gf_ref, tab_ref, y_ref, buf0, buf1, cexp_scr, *, tb, unroll):
    pattern = _chunk_pattern()
    n_part = PEER_SLOTS * SUBLANES
    slot_of = lax.broadcasted_iota(jnp.int32, (PEER_SLOTS, n_part), 1) // SUBLANES
    spread = jnp.where(slot_of == lax.broadcasted_iota(jnp.int32, (PEER_SLOTS, n_part), 0), 1.0, 0.0).astype(BF16)
    c_hi, c_lo = _split_bf16(c_ref[...])
    cexp_scr[0:tb, :] = jnp.dot(c_hi, spread, preferred_element_type=F32)
    cexp_scr[tb:2 * tb, :] = jnp.dot(c_lo, spread, preferred_element_type=F32)

    def compute(t, buf):
        coef = jnp.concatenate([jnp.broadcast_to(cexp_scr[pl.ds(t, 1), :], (SUBLANES, n_part)),
                                jnp.broadcast_to(cexp_scr[pl.ds(tb + t, 1), :], (SUBLANES, n_part))], axis=0)
        coef = jnp.where(pattern, coef, 0.0).astype(BF16)
        o16 = jnp.dot(coef, _gathered_bf16(buf), preferred_element_type=F32)
        x2 = _load_token_chunks(x_ref, t) + (o16[:SUBLANES] + o16[SUBLANES:])
        ss = jnp.sum(jnp.sum(x2 * x2, axis=1, keepdims=True), axis=0, keepdims=True)
        _store_token_chunks(y_ref, t, x2 * lax.rsqrt(ss * (1.0 / D_MODEL) + RMS_EPS) * gf_ref[...])

    _token_loop(idx_ref, tab_ref, (buf0, buf1), tb, unroll, compute)


def _peer_axpy(idx, c, x1r, gfr, tab, tb, unroll):
    n = idx.shape[0]
    kern = functools.partial(_peer_axpy_kernel, tb=tb, unroll=unroll)
    return pl.pallas_call(
        kern,
        grid=(n // tb,),
        in_specs=[
            pl.BlockSpec((tb, PEER_SLOTS), lambda i: (i, 0), memory_space=pltpu.SMEM),
            pl.BlockSpec((tb, PEER_SLOTS), lambda i: (i, 0)),
            pl.BlockSpec((tb, D_MODEL), lambda i: (i, 0)),
            pl.BlockSpec((SUBLANES, LANES), lambda i: (0, 0)),
            _table_spec(tab.shape[0]),
        ],
        out_specs=pl.BlockSpec((tb, D_MODEL), lambda i: (i, 0)),
        out_shape=jax.ShapeDtypeStruct((n, D_MODEL), F32),
        scratch_shapes=[pltpu.VMEM((PEER_SLOTS * PACK_ROWS, LANES), jnp.uint32),
                        pltpu.VMEM((PEER_SLOTS * PACK_ROWS, LANES), jnp.uint32),
                        pltpu.VMEM((2 * tb, PEER_SLOTS * SUBLANES), F32)],
        compiler_params=_params(("arbitrary",)),
        name="peer_axpy",
    )(idx, c, x1r, gfr, tab)


def _row_tile(n, want):
    t = min(want, n)
    assert n % t == 0
    return t


def _layer(x2, b, s, norm_mix, w_in, w_sb_proj, w_ca_proj, ca_rel_bias, w_out, norm_ffn,
           peer_w_query, peer_sub_keys, peer_u, peer_v, final_gain):
    n = b * s
    col = np.arange(IN_COLS)
    is_q = (col < ATT_WIDTH) | ((col >= 3 * ATT_WIDTH) & (col < 4 * ATT_WIDTH))
    col_scale = np.where(is_q, math.log2(math.e) / math.sqrt(HEAD_DIM), 1.0)
    w_in_bf = (w_in * jnp.asarray(col_scale, F32)[None, :]).astype(BF16)

    proj = _inproj(x2, norm_mix.reshape(1, D_MODEL), w_in_bf, _row_tile(n, 512))
    proj3 = proj.reshape(b, s, IN_COLS)

    tq = 256
    assert s % tq == 0 and s >= tq + CA_PAD
    sb_tq = min(2048, s)
    assert s % sb_tq == 0
    o_sb = _sb_attention(proj3, sb_tq, 256)
    win = tq + CA_PAD
    o_ca = _ca_attention(proj3, _ca_bias(ca_rel_bias, tq, win), tq, win, 64)

    x1, h2 = _merge(o_sb.reshape(n, ATT_WIDTH), o_ca.reshape(n, ATT_WIDTH), proj, x2,
                    w_sb_proj.astype(BF16), w_ca_proj.astype(BF16), w_out.astype(BF16),
                    norm_ffn.reshape(1, D_MODEL), _row_tile(n, 512))

    keys_bf = peer_sub_keys.reshape(2 * PEER_HEADS, PEER_N_KEYS, PEER_HALF).astype(BF16)
    idx, gate = _route(h2, peer_w_query.astype(BF16), keys_bf, _row_tile(n, 128))

    tbg = _row_tile(n, 64)
    c = _peer_dot(idx, h2.reshape(n * SUBLANES, LANES), gate, _pack_table(peer_u), tbg, 16)
    return _peer_axpy(idx, c, x1, final_gain.reshape(SUBLANES, LANES), _pack_table(peer_v), tbg, 16)


def kernel(x, norm_mix, w_in, w_sb_proj, w_ca_proj, ca_rel_bias, w_out, norm_ffn,
           peer_w_query, peer_sub_keys, peer_u, peer_v, norm_final):
    b, s, d = x.shape
    depth = norm_mix.shape[0]
    assert d == D_MODEL and depth == 1
    y = _layer(x.reshape(b * s, d), b, s, norm_mix[0], w_in[0], w_sb_proj[0], w_ca_proj[0],
               ca_rel_bias[0], w_out[0], norm_ffn[0], peer_w_query[0], peer_sub_keys[0],
               peer_u[0], peer_v[0], norm_final)
    return y.reshape(b, s, d)
```

```python
import functools
import math

import numpy as np
import jax
import jax.numpy as jnp
from jax import lax
from jax.experimental import pallas as pl
from jax.experimental.pallas import tpu as pltpu

D_MODEL = 1024
HEAD_DIM = 64
N_HEADS = 8
ATT_WIDTH = N_HEADS * HEAD_DIM
IN_COLS = 6 * ATT_WIDTH + 2 * D_MODEL
CHUNK = 64
CA_LEFT_CHUNKS = 8
CA_PAD = CA_LEFT_CHUNKS * CHUNK
REL_CLIP = 128
PEER_HEADS = 8
PEER_N_KEYS = 128
PEER_HALF = 128
PEER_TOPK = 16
PEER_SLOTS = PEER_HEADS * PEER_TOPK
RMS_EPS = 1e-6
NEG_INF = -1e30

LANES = 128
SUBLANES = 8
PACK_ROWS = D_MODEL // 2 // LANES

VMEM_LIMIT = 56 * 1024 * 1024

F32 = jnp.float32
BF16 = jnp.bfloat16
NT_DIMS = (((1,), (1,)), ((), ()))


def _params(sem, vmem=VMEM_LIMIT):
    return pltpu.CompilerParams(dimension_semantics=sem, vmem_limit_bytes=vmem)


def _inproj_kernel(x_ref, g_ref, w_ref, o_ref):
    x = x_ref[...]
    ms = jnp.mean(x * x, axis=-1, keepdims=True)
    h = (x * lax.rsqrt(ms + RMS_EPS) * g_ref[...]).astype(BF16)
    for c in range(IN_COLS // D_MODEL):
        cols = slice(c * D_MODEL, (c + 1) * D_MODEL)
        o_ref[:, cols] = jnp.dot(h, w_ref[:, cols], preferred_element_type=F32).astype(o_ref.dtype)


def _inproj(x2, g, w_bf, tm):
    n = x2.shape[0]
    return pl.pallas_call(
        _inproj_kernel,
        grid=(n // tm,),
        in_specs=[
            pl.BlockSpec((tm, D_MODEL), lambda i: (i, 0)),
            pl.BlockSpec((1, D_MODEL), lambda i: (0, 0)),
            pl.BlockSpec((D_MODEL, IN_COLS), lambda i: (0, 0)),
        ],
        out_specs=pl.BlockSpec((tm, IN_COLS), lambda i: (i, 0)),
        out_shape=jax.ShapeDtypeStruct((n, IN_COLS), BF16),
        compiler_params=_params(("arbitrary",)),
        name="inproj",
    )(x2, g, w_bf)


def _sb_kernel(q_ref, k_ref, v_ref, o_ref, acc_ref, r_ref, z_scr, lat_scr, *, tq, tk):
    i = pl.program_id(2)
    n_chunks = tq // tk
    lane = lax.broadcasted_iota(jnp.int32, (1, LANES), 1)
    head_mask = (lane < HEAD_DIM, lane >= HEAD_DIM)
    kk = lax.broadcasted_iota(jnp.int32, (tk, tk), 0)
    ss = lax.broadcasted_iota(jnp.int32, (tk, tk), 1)
    stri = jnp.where(kk > ss, 1.0, 0.0).astype(BF16)
    causal = ss < kk
    sign = jnp.uint32(0x80000000)

    acc_ref[...] = jnp.zeros_like(acc_ref)
    r_ref[...] = jnp.zeros_like(r_ref)

    def rows(c):
        return slice(c * tk, (c + 1) * tk)

    def sweep(j, chunks, diag_chunk):
        start = pl.multiple_of(j * tk, tk)
        k = k_ref[0, pl.ds(start, tk), :]
        v = v_ref[0, pl.ds(start, tk), :]
        zero = jnp.zeros_like(k)
        k2 = jnp.concatenate([jnp.where(head_mask[0], k, zero), jnp.where(head_mask[1], k, zero)], axis=0)
        v2 = jnp.concatenate([jnp.where(head_mask[0], v, zero), jnp.where(head_mask[1], v, zero)], axis=0)
        for c in chunks:
            z_scr[c] = lax.dot_general(q_ref[0, rows(c), :], k2, NT_DIMS, preferred_element_type=F32)
        for c in chunks:
            for h in range(2):
                cols = slice(h * tk, (h + 1) * tk)
                z = z_scr[c, :, cols]
                neg_abs = lax.bitcast_convert_type(lax.bitcast_convert_type(z, jnp.uint32) | sign, F32)
                sp = jnp.maximum(z, 0.0) + jnp.log2(1.0 + jnp.exp2(neg_abs))
                if c == diag_chunk:
                    sp = jnp.where(causal, sp, 0.0)
                z_scr[c, :, cols] = z - sp
                r = r_ref[h, rows(c), :]
                later = jnp.concatenate([r] * (tk // LANES), axis=1) + jnp.dot(
                    sp.astype(BF16), stri, preferred_element_type=F32)
                lat_scr[c, :, cols] = later
                r_ref[h, rows(c), :] = jnp.broadcast_to(later[:, 0:1] + sp[:, 0:1], (tk, LANES))
        for c in chunks:
            ws = []
            for h in range(2):
                cols = slice(h * tk, (h + 1) * tk)
                logw = z_scr[c, :, cols] - lat_scr[c, :, cols]
                if c == diag_chunk:
                    logw = jnp.where(causal, logw, NEG_INF)
                ws.append(jnp.exp2(logw).astype(BF16))
            acc_ref[rows(c), :] += jnp.dot(jnp.concatenate(ws, axis=1), v2, preferred_element_type=F32)

    base = i * n_chunks
    for d in range(n_chunks - 1, -1, -1):
        sweep(base + d, list(range(d, n_chunks)), d)

    def body(jj, carry):
        sweep(base - 1 - jj, list(range(n_chunks)), None)
        return carry

    lax.fori_loop(0, base, body, 0)
    o_ref[0] = acc_ref[...].astype(o_ref.dtype)


def _sb_attention(proj3, tq, tk):
    b, s, _ = proj3.shape
    n_pairs = ATT_WIDTH // LANES
    kern = functools.partial(_sb_kernel, tq=tq, tk=tk)
    return pl.pallas_call(
        kern,
        grid=(b, n_pairs, s // tq),
        in_specs=[
            pl.BlockSpec((1, tq, LANES), lambda bi, p, i: (bi, i, p)),
            pl.BlockSpec((1, s, LANES), lambda bi, p, i: (bi, 0, n_pairs + p)),
            pl.BlockSpec((1, s, LANES), lambda bi, p, i: (bi, 0, 2 * n_pairs + p)),
        ],
        out_specs=pl.BlockSpec((1, tq, LANES), lambda bi, p, i: (bi, i, p)),
        out_shape=jax.ShapeDtypeStruct((b, s, ATT_WIDTH), BF16),
        scratch_shapes=[
            pltpu.VMEM((tq, LANES), F32),
            pltpu.VMEM((2, tq, LANES), F32),
            pltpu.VMEM((tq // tk, tk, 2 * tk), F32),
            pltpu.VMEM((tq // tk, tk, 2 * tk), F32),
        ],
        compiler_params=_params(("arbitrary", "arbitrary", "arbitrary")),
        name="sb_attention",
    )(proj3, proj3, proj3)


def _ca_kernel(q_ref, k_ref, v_ref, b_ref, o_ref, z_scr, p_scr, *, tq, win, rc):
    i = pl.program_id(2)
    ws = pl.multiple_of(jnp.maximum(i * tq - CA_PAD, 0), tq)
    lane = lax.broadcasted_iota(jnp.int32, (1, LANES), 1)
    head_mask = (lane < HEAD_DIM, lane >= HEAD_DIM)
    k = k_ref[0, pl.ds(ws, win), :]
    v = v_ref[0, pl.ds(ws, win), :]
    zero = jnp.zeros_like(k)
    k2 = jnp.concatenate([jnp.where(head_mask[0], k, zero), jnp.where(head_mask[1], k, zero)], axis=0)
    z_scr[...] = lax.dot_general(q_ref[0], k2, NT_DIMS, preferred_element_type=F32)
    inv = [[None] * (tq // rc) for _ in range(2)]
    for h in range(2):
        for c in range(tq // rc):
            rows = slice(c * rc, (c + 1) * rc)
            cols = slice(h * win, (h + 1) * win)
            z = z_scr[rows, cols] + b_ref[0, h, rows, :]
            p = jnp.exp2(z - jnp.max(z, axis=1, keepdims=True))
            inv[h][c] = 1.0 / jnp.sum(p, axis=1, keepdims=True)
            p_scr[rows, cols] = p.astype(BF16)
    outs = [jnp.dot(p_scr[:, h * win:(h + 1) * win], jnp.where(head_mask[h], v, zero),
                    preferred_element_type=F32) for h in range(2)]
    o_ref[0] = (outs[0] * jnp.concatenate(inv[0], axis=0)
                + outs[1] * jnp.concatenate(inv[1], axis=0)).astype(o_ref.dtype)


def _ca_bias(rel_table, tq, win):
    r = np.arange(tq)[:, None]
    c = np.arange(win)[None, :]
    n_heads = rel_table.shape[0]
    span = tq + win - 1
    table = rel_table.astype(F32)
    out = []
    for var in range(3):
        off = var * tq - max(var * tq - CA_PAD, 0)
        relc = (off + r) // CHUNK - c // CHUNK
        valid = (relc >= 0) & (relc <= CA_LEFT_CHUNKS)
        dist = np.concatenate([off - np.arange(win), off + np.arange(tq - 1, 0, -1)])
        ring = table[:, np.clip(dist, -REL_CLIP, REL_CLIP) + REL_CLIP]
        toep = jnp.tile(ring, (1, tq))[:, :tq * (span - 1)].reshape(n_heads, tq, span - 1)[:, :, :win]
        out.append(jnp.where(valid[None], toep * math.log2(math.e), NEG_INF))
    return jnp.stack(out)


def _ca_attention(proj3, bias, tq, win, rc):
    b, s, _ = proj3.shape
    n_pairs = ATT_WIDTH // LANES
    kern = functools.partial(_ca_kernel, tq=tq, win=win, rc=rc)
    return pl.pallas_call(
        kern,
        grid=(b, n_pairs, s // tq),
        in_specs=[
            pl.BlockSpec((1, tq, LANES), lambda bi, p, i: (bi, i, 3 * n_pairs + p)),
            pl.BlockSpec((1, s, LANES), lambda bi, p, i: (bi, 0, 4 * n_pairs + p)),
            pl.BlockSpec((1, s, LANES), lambda bi, p, i: (bi, 0, 5 * n_pairs + p)),
            pl.BlockSpec((1, 2, tq, win), lambda bi, p, i: (jnp.minimum(i, 2), p, 0, 0)),
        ],
        out_specs=pl.BlockSpec((1, tq, LANES), lambda bi, p, i: (bi, i, p)),
        out_shape=jax.ShapeDtypeStruct((b, s, ATT_WIDTH), BF16),
        scratch_shapes=[pltpu.VMEM((tq, 2 * win), F32), pltpu.VMEM((tq, 2 * win), BF16)],
        compiler_params=_params(("arbitrary", "arbitrary", "arbitrary")),
        name="ca_attention",
    )(proj3, proj3, proj3, bias)


def _merge_kernel(osb_ref, oca_ref, gsb_ref, gca_ref, x_ref, psb_ref, pca_ref, wout_ref, nf_ref,
                  x1_ref, h2_ref):
    br_sb = jnp.dot(osb_ref[...], psb_ref[...], preferred_element_type=F32)
    br_ca = jnp.dot(oca_ref[...], pca_ref[...], preferred_element_type=F32)
    merged = (jax.nn.sigmoid(gsb_ref[...].astype(F32)) * br_sb
              + jax.nn.sigmoid(gca_ref[...].astype(F32)) * br_ca)
    x1 = x_ref[...] + jnp.dot(merged.astype(BF16), wout_ref[...], preferred_element_type=F32)
    x1_ref[...] = x1
    ms = jnp.mean(x1 * x1, axis=-1, keepdims=True)
    h2_ref[...] = x1 * lax.rsqrt(ms + RMS_EPS) * nf_ref[...]


def _merge(o_sb, o_ca, proj, x2, psb, pca, wout, nf, tm):
    n = x2.shape[0]
    gate_blk = 3 * ATT_WIDTH * 2 // D_MODEL
    return pl.pallas_call(
        _merge_kernel,
        grid=(n // tm,),
        in_specs=[
            pl.BlockSpec((tm, ATT_WIDTH), lambda i: (i, 0)),
            pl.BlockSpec((tm, ATT_WIDTH), lambda i: (i, 0)),
            pl.BlockSpec((tm, D_MODEL), lambda i: (i, gate_blk)),
            pl.BlockSpec((tm, D_MODEL), lambda i: (i, gate_blk + 1)),
            pl.BlockSpec((tm, D_MODEL), lambda i: (i, 0)),
            pl.BlockSpec((ATT_WIDTH, D_MODEL), lambda i: (0, 0)),
            pl.BlockSpec((ATT_WIDTH, D_MODEL), lambda i: (0, 0)),
            pl.BlockSpec((D_MODEL, D_MODEL), lambda i: (0, 0)),
            pl.BlockSpec((1, D_MODEL), lambda i: (0, 0)),
        ],
        out_specs=[
            pl.BlockSpec((tm, D_MODEL), lambda i: (i, 0)),
            pl.BlockSpec((tm, D_MODEL), lambda i: (i, 0)),
        ],
        out_shape=[jax.ShapeDtypeStruct((n, D_MODEL), F32), jax.ShapeDtypeStruct((n, D_MODEL), F32)],
        compiler_params=_params(("arbitrary",)),
        name="merge",
    )(o_sb, o_ca, proj, proj, x2, psb, pca, wout, nf)


def _peer_candidates():
    return [(a, b) for a in range(PEER_TOPK) for b in range(PEER_TOPK) if (a + 1) * (b + 1) <= PEER_TOPK]


def _extract_top(s, iota, n_rounds, sentinel):
    vals, ids = [], []
    for _ in range(n_rounds):
        m = jnp.max(s, axis=0, keepdims=True)
        cand = jnp.where(s == m, iota, sentinel)
        am = jnp.min(cand, axis=0, keepdims=True)
        vals.append(m)
        ids.append(am)
        s = jnp.where(cand == am, -jnp.inf, s)
    return vals, ids


ROUTE_TOKENS = 128


def _route_scratch():
    n_rows = -(-len(_peer_candidates()) // SUBLANES) * SUBLANES
    return [
        pltpu.VMEM((n_rows, ROUTE_TOKENS), F32),
        pltpu.VMEM((n_rows, ROUTE_TOKENS), F32),
        pltpu.VMEM((PEER_SLOTS, ROUTE_TOKENS), F32),
        pltpu.VMEM((PEER_SLOTS, ROUTE_TOKENS), F32),
    ]


def _slot_row(hd, r):
    if isinstance(hd, int):
        return slice(hd * PEER_TOPK + r, hd * PEER_TOPK + r + 1)
    return pl.ds(hd * PEER_TOPK + r, 1)


def _route_head(h_ref, wq_ref, keys_ref, hd, cv_scr, ci_scr, ri_scr, rg_scr):
    cands = _peer_candidates()
    n_cand = len(cands)
    n_rows = cv_scr.shape[0]
    tb = ROUTE_TOKENS
    h = h_ref[...].astype(BF16)
    iota_n = lax.broadcasted_iota(jnp.int32, (PEER_N_KEYS, tb), 0).astype(F32)
    iota_c = lax.broadcasted_iota(jnp.int32, (n_rows, tb), 0).astype(F32)
    cv_scr[n_cand:, :] = jnp.full((n_rows - n_cand, tb), -jnp.inf, F32)
    ci_scr[n_cand:, :] = jnp.zeros((n_rows - n_cand, tb), F32)
    tops = []
    for p in range(2):
        hp = hd * 2 + p
        q = jnp.dot(h, wq_ref[hp], preferred_element_type=F32).astype(BF16)
        s = lax.dot_general(keys_ref[hp], q, NT_DIMS, preferred_element_type=F32)
        tops.append(_extract_top(s, iota_n, PEER_TOPK, float(PEER_N_KEYS)))
    (v1, i1), (v2, i2) = tops
    i1s = [x * float(PEER_N_KEYS * PACK_ROWS) for x in i1]
    i2s = [x * float(PACK_ROWS) for x in i2]
    for r, (a, b) in enumerate(cands):
        cv_scr[r:r + 1, :] = v1[a] + v2[b]
        ci_scr[r:r + 1, :] = i1s[a] + i2s[b]
    cv = cv_scr[...]
    ci = ci_scr[...]
    best = []
    for r in range(PEER_TOPK):
        m = jnp.max(cv, axis=0, keepdims=True)
        cand = jnp.where(cv == m, iota_c, float(n_rows))
        hit = cand == jnp.min(cand, axis=0, keepdims=True)
        e = jnp.sum(jnp.where(hit, ci, 0.0), axis=0, keepdims=True)
        cv = jnp.where(hit, -jnp.inf, cv)
        best.append(m)
        ri_scr[_slot_row(hd, r), :] = e
    ex = [jnp.exp(bv - best[0]) for bv in best]
    tot = ex[0]
    for t in ex[1:]:
        tot = tot + t
    inv = 1.0 / tot
    for r in range(PEER_TOPK):
        rg_scr[_slot_row(hd, r), :] = ex[r] * inv


def _route_emit(idx_ref, g_ref, ri_scr, rg_scr):
    idx_ref[...] = ri_scr[...].T.astype(jnp.int32)
    g_ref[...] = rg_scr[...].T


def _route_kernel(h_ref, wq_ref, keys_ref, idx_ref, g_ref, cv_scr, ci_scr, ri_scr, rg_scr):
    for hd in range(PEER_HEADS):
        _route_head(h_ref, wq_ref, keys_ref, hd, cv_scr, ci_scr, ri_scr, rg_scr)
    _route_emit(idx_ref, g_ref, ri_scr, rg_scr)


def _route_in_specs(block_index):
    return [
        pl.BlockSpec((ROUTE_TOKENS, D_MODEL), block_index),
        pl.BlockSpec((2 * PEER_HEADS, D_MODEL, PEER_HALF), lambda i: (0, 0, 0)),
        pl.BlockSpec((2 * PEER_HEADS, PEER_N_KEYS, PEER_HALF), lambda i: (0, 0, 0)),
    ]


def _route_out(n, block_index):
    specs = [pl.BlockSpec((ROUTE_TOKENS, PEER_SLOTS), block_index),
             pl.BlockSpec((ROUTE_TOKENS, PEER_SLOTS), block_index)]
    shapes = [jax.ShapeDtypeStruct((n, PEER_SLOTS), jnp.int32), jax.ShapeDtypeStruct((n, PEER_SLOTS), F32)]
    return specs, shapes


def _route(h2, wq3, keys_bf, n, tok_off=0):
    out_specs, out_shape = _route_out(n, lambda i: (i, 0))
    return pl.pallas_call(
        _route_kernel,
        grid=(n // ROUTE_TOKENS,),
        in_specs=_route_in_specs(lambda i: (i + tok_off // ROUTE_TOKENS, 0)),
        out_specs=out_specs,
        out_shape=out_shape,
        scratch_shapes=_route_scratch(),
        compiler_params=_params(("arbitrary",)),
        name="peer_route",
    )(h2, wq3, keys_bf)


def _pack_table(t):
    e = t.shape[0]
    bits = lax.bitcast_convert_type(t.astype(BF16), jnp.uint16).astype(jnp.uint32)
    bits = bits.reshape(e, PACK_ROWS, 2, LANES)
    packed = (bits[:, :, 1, :] << 16) | bits[:, :, 0, :]
    return packed.reshape(e * PACK_ROWS, LANES)


def _table_spec(n_rows):
    return pl.BlockSpec((n_rows, LANES), lambda i: (0, 0), pipeline_mode=pl.Buffered(1))


def _gather_rows(idx_ref, t, tab_ref, buf):
    for k in range(PEER_SLOTS):
        e = pl.multiple_of(idx_ref[t, k], PACK_ROWS)
        buf[k * PACK_ROWS:(k + 1) * PACK_ROWS, :] = tab_ref[pl.ds(e, PACK_ROWS), :]


def _gathered_bf16(buf):
    return pltpu.bitcast(buf[...], BF16)


def _chunk_pattern():
    shape = (2 * SUBLANES, PEER_SLOTS * SUBLANES)
    row = lax.broadcasted_iota(jnp.int32, shape, 0)
    col = lax.broadcasted_iota(jnp.int32, shape, 1)
    return (row & (SUBLANES - 1)) == (col & (SUBLANES - 1))


def _token_rows(t):
    if isinstance(t, int):
        return slice(t * SUBLANES, (t + 1) * SUBLANES)
    return pl.ds(pl.multiple_of(t * SUBLANES, SUBLANES), SUBLANES)


def _load_token_chunks(ref, t):
    row = ref[pl.ds(t, 1), :]
    return jnp.concatenate([row[:, j * LANES:(j + 1) * LANES] for j in range(SUBLANES)], axis=0)


def _store_token_chunks(ref, t, val):
    ref[pl.ds(t, 1), :] = jnp.concatenate([val[j:j + 1, :] for j in range(SUBLANES)], axis=1)


def _split_bf16(x):
    hi = x.astype(BF16)
    return hi, (x - hi.astype(F32)).astype(BF16)


def _token_loop(idx_ref, tab_ref, bufs, tb, unroll, compute, side_work=None):
    _gather_rows(idx_ref, 0, tab_ref, bufs[0])

    def body(i, carry):
        if side_work is not None:
            side_work(i)
        for u in range(unroll):
            t = i * unroll + u
            _gather_rows(idx_ref, jnp.minimum(t + 1, tb - 1), tab_ref, bufs[(u + 1) % 2])
            compute(t, bufs[u % 2])
        return carry

    lax.fori_loop(0, tb // unroll, body, 0)


def _peer_dot_body(idx_ref, h_ref, g_ref, tab_ref, c_ref, buf0, buf1, part_scr, tb, unroll, side_work):
    pattern = _chunk_pattern()

    def compute(t, buf):
        h16 = jnp.concatenate(_split_bf16(h_ref[_token_rows(t), :]), axis=0)
        r = lax.dot_general(h16, _gathered_bf16(buf), NT_DIMS, preferred_element_type=F32)
        r = jnp.where(pattern, r, 0.0)
        part_scr[pl.ds(t, 1), :] = jnp.sum(r[:SUBLANES] + r[SUBLANES:], axis=0, keepdims=True)

    _token_loop(idx_ref, tab_ref, (buf0, buf1), tb, unroll, compute, side_work)

    n_part = PEER_SLOTS * SUBLANES
    slot_of = lax.broadcasted_iota(jnp.int32, (n_part, PEER_SLOTS), 0) // SUBLANES
    fold = jnp.where(slot_of == lax.broadcasted_iota(jnp.int32, (n_part, PEER_SLOTS), 1), 1.0, 0.0).astype(BF16)
    p_hi, p_lo = _split_bf16(part_scr[...])
    a = (jnp.dot(p_hi, fold, preferred_element_type=F32)
         + jnp.dot(p_lo, fold, preferred_element_type=F32))
    c_ref[...] = g_ref[...] * (0.5 * a * (1.0 + lax.erf(a * (1.0 / math.sqrt(2.0)))))


def _peer_dot_kernel(idx_ref, h_ref, g_ref, tab_ref, c_ref, buf0, buf1, part_scr, *, tb, unroll):
    _peer_dot_body(idx_ref, h_ref, g_ref, tab_ref, c_ref, buf0, buf1, part_scr, tb, unroll, None)


def _peer_dot_route_kernel(idx_ref, h_ref, g_ref, tab_ref, rh_ref, wq_ref, keys_ref,
                           c_ref, ridx_ref, rg_ref, buf0, buf1, part_scr, cv_scr, ci_scr, ri_scr, rg_scr,
                           *, tb, unroll):
    trips = tb // unroll
    phase = pl.program_id(0) % (PEER_HEADS // trips)

    def side_work(i):
        _route_head(rh_ref, wq_ref, keys_ref, phase * trips + i, cv_scr, ci_scr, ri_scr, rg_scr)

    _peer_dot_body(idx_ref, h_ref, g_ref, tab_ref, c_ref, buf0, buf1, part_scr, tb, unroll, side_work)

    @pl.when(phase == PEER_HEADS // trips - 1)
    def _():
        _route_emit(ridx_ref, rg_ref, ri_scr, rg_scr)


def _peer_dot_specs(tab, tb, h_block_off):
    in_specs = [
        pl.BlockSpec((tb, PEER_SLOTS), lambda i: (i, 0), memory_space=pltpu.SMEM),
        pl.BlockSpec((tb * SUBLANES, LANES), lambda i: (i + h_block_off, 0)),
        pl.BlockSpec((tb, PEER_SLOTS), lambda i: (i, 0)),
        _table_spec(tab.shape[0]),
    ]
    scratch = [pltpu.VMEM((PEER_SLOTS * PACK_ROWS, LANES), jnp.uint32),
               pltpu.VMEM((PEER_SLOTS * PACK_ROWS, LANES), jnp.uint32),
               pltpu.VMEM((tb, PEER_SLOTS * SUBLANES), F32)]
    return in_specs, scratch


def _peer_dot(idx, h2r, gate, tab, tb, unroll, tok_off=0):
    n = idx.shape[0]
    in_specs, scratch = _peer_dot_specs(tab, tb, tok_off // tb)
    return pl.pallas_call(
        functools.partial(_peer_dot_kernel, tb=tb, unroll=unroll),
        grid=(n // tb,),
        in_specs=in_specs,
        out_specs=pl.BlockSpec((tb, PEER_SLOTS), lambda i: (i, 0)),
        out_shape=jax.ShapeDtypeStruct((n, PEER_SLOTS), F32),
        scratch_shapes=scratch,
        compiler_params=_params(("arbitrary",)),
        name="peer_dot",
    )(idx, h2r, gate, tab)


def _peer_dot_route(idx, h2r, gate, tab, h2, wq3, keys_bf, tb, unroll, tok_off, route_off):
    n = idx.shape[0]
    assert PEER_HEADS % (tb // unroll) == 0
    steps_per_block = PEER_HEADS // (tb // unroll)
    assert steps_per_block * tb == ROUTE_TOKENS and route_off % ROUTE_TOKENS == 0
    in_specs, scratch = _peer_dot_specs(tab, tb, tok_off // tb)
    route_in_block = lambda i: (i // steps_per_block + route_off // ROUTE_TOKENS, 0)
    route_block = lambda i: (i // steps_per_block, 0)
    route_out_specs, route_out_shape = _route_out(n, route_block)
    return pl.pallas_call(
        functools.partial(_peer_dot_route_kernel, tb=tb, unroll=unroll),
        grid=(n // tb,),
        in_specs=in_specs + _route_in_specs(route_in_block),
        out_specs=[pl.BlockSpec((tb, PEER_SLOTS), lambda i: (i, 0))] + route_out_specs,
        out_shape=[jax.ShapeDtypeStruct((n, PEER_SLOTS), F32)] + route_out_shape,
        scratch_shapes=scratch + _route_scratch(),
        compiler_params=_params(("arbitrary",)),
        name="peer_dot_route",
    )(idx, h2r, gate, tab, h2, wq3, keys_bf)


def _peer_axpy_kernel(idx_ref, c_ref, x_ref, gf_ref, tab_ref, y_ref, buf0, buf1, cexp_scr, *, tb, unroll):
    pattern = _chunk_pattern()
    n_part = PEER_SLOTS * SUBLANES
    slot_of = lax.broadcasted_iota(jnp.int32, (PEER_SLOTS, n_part), 1) // SUBLANES
    spread = jnp.where(slot_of == lax.broadcasted_iota(jnp.int32, (PEER_SLOTS, n_part), 0), 1.0, 0.0).astype(BF16)
    c_hi, c_lo = _split_bf16(c_ref[...])
    cexp_scr[0:tb, :] = jnp.dot(c_hi, spread, preferred_element_type=F32)
    cexp_scr[tb:2 * tb, :] = jnp.dot(c_lo, spread, preferred_element_type=F32)

    def compute(t, buf):
        coef = jnp.concatenate([jnp.broadcast_to(cexp_scr[pl.ds(t, 1), :], (SUBLANES, n_part)),
                                jnp.broadcast_to(cexp_scr[pl.ds(tb + t, 1), :], (SUBLANES, n_part))], axis=0)
        coef = jnp.where(pattern, coef, 0.0).astype(BF16)
        o16 = jnp.dot(coef, _gathered_bf16(buf), preferred_element_type=F32)
        x2 = _load_token_chunks(x_ref, t) + (o16[:SUBLANES] + o16[SUBLANES:])
        ss = jnp.sum(jnp.sum(x2 * x2, axis=1, keepdims=True), axis=0, keepdims=True)
        _store_token_chunks(y_ref, t, x2 * lax.rsqrt(ss * (1.0 / D_MODEL) + RMS_EPS) * gf_ref[...])

    _token_loop(idx_ref, tab_ref, (buf0, buf1), tb, unroll, compute)


def _peer_axpy(idx, c, x1r, gfr, tab, tb, unroll):
    n = idx.shape[0]
    kern = functools.partial(_peer_axpy_kernel, tb=tb, unroll=unroll)
    return pl.pallas_call(
        kern,
        grid=(n // tb,),
        in_specs=[
            pl.BlockSpec((tb, PEER_SLOTS), lambda i: (i, 0), memory_space=pltpu.SMEM),
            pl.BlockSpec((tb, PEER_SLOTS), lambda i: (i, 0)),
            pl.BlockSpec((tb, D_MODEL), lambda i: (i, 0)),
            pl.BlockSpec((SUBLANES, LANES), lambda i: (0, 0)),
            _table_spec(tab.shape[0]),
        ],
        out_specs=pl.BlockSpec((tb, D_MODEL), lambda i: (i, 0)),
        out_shape=jax.ShapeDtypeStruct((n, D_MODEL), F32),
        scratch_shapes=[pltpu.VMEM((PEER_SLOTS * PACK_ROWS, LANES), jnp.uint32),
                        pltpu.VMEM((PEER_SLOTS * PACK_ROWS, LANES), jnp.uint32),
                        pltpu.VMEM((2 * tb, PEER_SLOTS * SUBLANES), F32)],
        compiler_params=_params(("arbitrary",)),
        name="peer_axpy",
    )(idx, c, x1r, gfr, tab)


def _row_tile(n, want):
    t = min(want, n)
    assert n % t == 0
    return t


def _layer(x2, b, s, norm_mix, w_in, w_sb_proj, w_ca_proj, ca_rel_bias, w_out, norm_ffn,
           peer_w_query, peer_sub_keys, peer_u, peer_v, final_gain):
    n = b * s
    col = np.arange(IN_COLS)
    is_q = (col < ATT_WIDTH) | ((col >= 3 * ATT_WIDTH) & (col < 4 * ATT_WIDTH))
    col_scale = np.where(is_q, math.log2(math.e) / math.sqrt(HEAD_DIM), 1.0)
    w_in_bf = (w_in * jnp.asarray(col_scale, F32)[None, :]).astype(BF16)

    proj = _inproj(x2, norm_mix.reshape(1, D_MODEL), w_in_bf, _row_tile(n, 512))
    proj3 = proj.reshape(b, s, IN_COLS)

    tq = 256
    assert s % tq == 0 and s >= tq + CA_PAD
    sb_tq = min(2048, s)
    assert s % sb_tq == 0
    o_sb = _sb_attention(proj3, sb_tq, 256)
    win = tq + CA_PAD
    o_ca = _ca_attention(proj3, _ca_bias(ca_rel_bias, tq, win), tq, win, 64)

    x1, h2 = _merge(o_sb.reshape(n, ATT_WIDTH), o_ca.reshape(n, ATT_WIDTH), proj, x2,
                    w_sb_proj.astype(BF16), w_ca_proj.astype(BF16), w_out.astype(BF16),
                    norm_ffn.reshape(1, D_MODEL), _row_tile(n, 512))

    keys_bf = peer_sub_keys.reshape(2 * PEER_HEADS, PEER_N_KEYS, PEER_HALF).astype(BF16)
    wq3 = jnp.transpose(peer_w_query.astype(BF16).reshape(D_MODEL, 2 * PEER_HEADS, PEER_HALF), (1, 0, 2))
    tab_u = _pack_table(peer_u)
    h2r = h2.reshape(n * SUBLANES, LANES)
    tbg, unroll = 64, 16
    n_seg = b if (b > 1 and s % ROUTE_TOKENS == 0) else 1
    seg = n // n_seg
    assert seg % ROUTE_TOKENS == 0
    idx_j, gate_j = _route(h2, wq3, keys_bf, seg)
    idx_parts, c_parts = [], []
    for j in range(n_seg):
        idx_parts.append(idx_j)
        if j + 1 < n_seg:
            c_j, idx_j, gate_j = _peer_dot_route(idx_j, h2r, gate_j, tab_u, h2, wq3, keys_bf, tbg, unroll,
                                                 j * seg, (j + 1) * seg)
        else:
            c_j = _peer_dot(idx_j, h2r, gate_j, tab_u, tbg, unroll, j * seg)
        c_parts.append(c_j)
    idx = jnp.concatenate(idx_parts, axis=0)
    c = jnp.concatenate(c_parts, axis=0)
    return _peer_axpy(idx, c, x1, final_gain.reshape(SUBLANES, LANES), _pack_table(peer_v), tbg, unroll)


def kernel(x, norm_mix, w_in, w_sb_proj, w_ca_proj, ca_rel_bias, w_out, norm_ffn,
           peer_w_query, peer_sub_keys, peer_u, peer_v, norm_final):
    b, s, d = x.shape
    depth = norm_mix.shape[0]
    assert d == D_MODEL and depth == 1
    y = _layer(x.reshape(b * s, d), b, s, norm_mix[0], w_in[0], w_sb_proj[0], w_ca_proj[0],
               ca_rel_bias[0], w_out[0], norm_ffn[0], peer_w_query[0], peer_sub_keys[0],
               peer_u[0], peer_v[0], norm_final)
    return y.reshape(b, s, d)
```

```python
import functools
import math

import numpy as np
import jax
import jax.numpy as jnp
from jax import lax
from jax.experimental import pallas as pl
from jax.experimental.pallas import tpu as pltpu

D_MODEL = 1024
HEAD_DIM = 64
N_HEADS = 8
ATT_WIDTH = N_HEADS * HEAD_DIM
IN_COLS = 6 * ATT_WIDTH + 2 * D_MODEL
CHUNK = 64
CA_LEFT_CHUNKS = 8
CA_PAD = CA_LEFT_CHUNKS * CHUNK
REL_CLIP = 128
PEER_HEADS = 8
PEER_N_KEYS = 128
PEER_HALF = 128
PEER_TOPK = 16
PEER_SLOTS = PEER_HEADS * PEER_TOPK
RMS_EPS = 1e-6
NEG_INF = -1e30

LANES = 128
SUBLANES = 8
PACK_ROWS = D_MODEL // 2 // LANES

VMEM_LIMIT = 56 * 1024 * 1024

F32 = jnp.float32
BF16 = jnp.bfloat16
NT_DIMS = (((1,), (1,)), ((), ()))


def _params(sem, vmem=VMEM_LIMIT):
    return pltpu.CompilerParams(dimension_semantics=sem, vmem_limit_bytes=vmem)


def _inproj_kernel(x_ref, g_ref, w_ref, o_ref):
    x = x_ref[...]
    ms = jnp.mean(x * x, axis=-1, keepdims=True)
    h = (x * lax.rsqrt(ms + RMS_EPS) * g_ref[...]).astype(BF16)
    for c in range(IN_COLS // D_MODEL):
        cols = slice(c * D_MODEL, (c + 1) * D_MODEL)
        o_ref[:, cols] = jnp.dot(h, w_ref[:, cols], preferred_element_type=F32).astype(o_ref.dtype)


def _inproj(x2, g, w_bf, tm):
    n = x2.shape[0]
    return pl.pallas_call(
        _inproj_kernel,
        grid=(n // tm,),
        in_specs=[
            pl.BlockSpec((tm, D_MODEL), lambda i: (i, 0)),
            pl.BlockSpec((1, D_MODEL), lambda i: (0, 0)),
            pl.BlockSpec((D_MODEL, IN_COLS), lambda i: (0, 0)),
        ],
        out_specs=pl.BlockSpec((tm, IN_COLS), lambda i: (i, 0)),
        out_shape=jax.ShapeDtypeStruct((n, IN_COLS), BF16),
        compiler_params=_params(("arbitrary",)),
        name="inproj",
    )(x2, g, w_bf)


def _sb_kernel(q_ref, k_ref, v_ref, o_ref, acc_ref, r_ref, z_scr, lat_scr, *, tq, tk):
    i = pl.program_id(2)
    n_chunks = tq // tk
    lane = lax.broadcasted_iota(jnp.int32, (1, LANES), 1)
    head_mask = (lane < HEAD_DIM, lane >= HEAD_DIM)
    kk = lax.broadcasted_iota(jnp.int32, (tk, tk), 0)
    ss = lax.broadcasted_iota(jnp.int32, (tk, tk), 1)
    stri = jnp.where(kk > ss, 1.0, 0.0).astype(BF16)
    causal = ss < kk
    sign = jnp.uint32(0x80000000)

    acc_ref[...] = jnp.zeros_like(acc_ref)
    r_ref[...] = jnp.zeros_like(r_ref)

    def rows(c):
        return slice(c * tk, (c + 1) * tk)

    def both_heads(x):
        zero = jnp.zeros_like(x)
        return jnp.concatenate([jnp.where(head_mask[0], x, zero), jnp.where(head_mask[1], x, zero)], axis=0)

    def logits(j, chunks):
        k2 = both_heads(k_ref[0, pl.ds(pl.multiple_of(j * tk, tk), tk), :])
        for c in chunks:
            z_scr[c] = lax.dot_general(q_ref[0, rows(c), :], k2, NT_DIMS, preferred_element_type=F32)

    def later_sums(chunks, diag_chunk):
        for c in chunks:
            for h in range(2):
                cols = slice(h * tk, (h + 1) * tk)
                z = z_scr[c, :, cols]
                neg_abs = lax.bitcast_convert_type(lax.bitcast_convert_type(z, jnp.uint32) | sign, F32)
                sp = jnp.maximum(z, 0.0) + jnp.log2(1.0 + jnp.exp2(neg_abs))
                if c == diag_chunk:
                    sp = jnp.where(causal, sp, 0.0)
                z_scr[c, :, cols] = z - sp
                r = r_ref[h, rows(c), :]
                later = jnp.concatenate([r] * (tk // LANES), axis=1) + jnp.dot(
                    sp.astype(BF16), stri, preferred_element_type=F32)
                lat_scr[c, :, cols] = later
                r_ref[h, rows(c), :] = jnp.broadcast_to(later[:, 0:1] + sp[:, 0:1], (tk, LANES))

    def weigh(j, chunks, diag_chunk):
        v2 = both_heads(v_ref[0, pl.ds(pl.multiple_of(j * tk, tk), tk), :])
        for c in chunks:
            ws = []
            for h in range(2):
                cols = slice(h * tk, (h + 1) * tk)
                logw = z_scr[c, :, cols] - lat_scr[c, :, cols]
                if c == diag_chunk:
                    logw = jnp.where(causal, logw, NEG_INF)
                ws.append(jnp.exp2(logw).astype(BF16))
            acc_ref[rows(c), :] += jnp.dot(jnp.concatenate(ws, axis=1), v2, preferred_element_type=F32)

    all_chunks = list(range(n_chunks))
    base = i * n_chunks
    for d in range(n_chunks - 1, -1, -1):
        chunks = list(range(d, n_chunks))
        logits(base + d, chunks)
        later_sums(chunks, d)
        weigh(base + d, chunks, d)

    def body(jj, carry):
        j = base - 1 - jj
        logits(j, all_chunks)
        later_sums(all_chunks, None)
        weigh(j, all_chunks, None)
        return carry

    lax.fori_loop(0, base, body, 0)
    o_ref[0] = acc_ref[...].astype(o_ref.dtype)


def _sb_attention(proj3, tq, tk):
    b, s, _ = proj3.shape
    n_pairs = ATT_WIDTH // LANES
    kern = functools.partial(_sb_kernel, tq=tq, tk=tk)
    return pl.pallas_call(
        kern,
        grid=(b, n_pairs, s // tq),
        in_specs=[
            pl.BlockSpec((1, tq, LANES), lambda bi, p, i: (bi, i, p)),
            pl.BlockSpec((1, s, LANES), lambda bi, p, i: (bi, 0, n_pairs + p)),
            pl.BlockSpec((1, s, LANES), lambda bi, p, i: (bi, 0, 2 * n_pairs + p)),
        ],
        out_specs=pl.BlockSpec((1, tq, LANES), lambda bi, p, i: (bi, i, p)),
        out_shape=jax.ShapeDtypeStruct((b, s, ATT_WIDTH), BF16),
        scratch_shapes=[
            pltpu.VMEM((tq, LANES), F32),
            pltpu.VMEM((2, tq, LANES), F32),
            pltpu.VMEM((tq // tk, tk, 2 * tk), F32),
            pltpu.VMEM((tq // tk, tk, 2 * tk), F32),
        ],
        compiler_params=_params(("arbitrary", "arbitrary", "arbitrary")),
        name="sb_attention",
    )(proj3, proj3, proj3)


def _ca_kernel(q_ref, k_ref, v_ref, ring_ref, o_ref, z_scr, p_scr, b_scr, *, tq, win, rc):
    i = pl.program_id(2)
    ws = pl.multiple_of(jnp.maximum(i * tq - CA_PAD, 0), tq)

    @pl.when(i <= 2)
    def _():
        off = i * tq - ws
        r = lax.broadcasted_iota(jnp.int32, (tq, win), 0)
        c = lax.broadcasted_iota(jnp.int32, (tq, win), 1)
        relc = ((off + r) >> 6) - (c >> 6)
        valid = (relc >= 0) & (relc <= CA_LEFT_CHUNKS)
        for h in range(2):
            ring = jnp.broadcast_to(ring_ref[0, 0, h:h + 1, :], (tq, ring_ref.shape[-1]))
            toep = pltpu.roll(ring, 0, 1, stride=1, stride_axis=0)
            b_scr[h] = jnp.where(valid, toep[:, :win], NEG_INF)

    lane = lax.broadcasted_iota(jnp.int32, (1, LANES), 1)
    head_mask = (lane < HEAD_DIM, lane >= HEAD_DIM)
    k = k_ref[0, pl.ds(ws, win), :]
    v = v_ref[0, pl.ds(ws, win), :]
    zero = jnp.zeros_like(k)
    k2 = jnp.concatenate([jnp.where(head_mask[0], k, zero), jnp.where(head_mask[1], k, zero)], axis=0)
    z_scr[...] = lax.dot_general(q_ref[0], k2, NT_DIMS, preferred_element_type=F32)
    inv = [[None] * (tq // rc) for _ in range(2)]
    for h in range(2):
        for c in range(tq // rc):
            rows = slice(c * rc, (c + 1) * rc)
            cols = slice(h * win, (h + 1) * win)
            z = z_scr[rows, cols] + b_scr[h, rows, :]
            p = jnp.exp2(z - jnp.max(z, axis=1, keepdims=True))
            inv[h][c] = 1.0 / jnp.sum(p, axis=1, keepdims=True)
            p_scr[rows, cols] = p.astype(BF16)
    outs = [jnp.dot(p_scr[:, h * win:(h + 1) * win], jnp.where(head_mask[h], v, zero),
                    preferred_element_type=F32) for h in range(2)]
    o_ref[0] = (outs[0] * jnp.concatenate(inv[0], axis=0)
                + outs[1] * jnp.concatenate(inv[1], axis=0)).astype(o_ref.dtype)


def _ca_rings(rel_table, tq, win):
    ring_len = tq + win
    out = []
    for var in range(3):
        off = var * tq - max(var * tq - CA_PAD, 0)
        dist = np.concatenate([off - np.arange(win), [0], off + np.arange(tq - 1, 0, -1)])
        out.append(rel_table.astype(F32)[:, np.clip(dist, -REL_CLIP, REL_CLIP) + REL_CLIP])
    rings = jnp.stack(out) * math.log2(math.e)
    return rings.reshape(3, rel_table.shape[0] // 2, 2, ring_len)


def _ca_attention(proj3, rings, tq, win, rc):
    b, s, _ = proj3.shape
    n_pairs = ATT_WIDTH // LANES
    assert CHUNK == 64 and (tq + win) % LANES == 0
    kern = functools.partial(_ca_kernel, tq=tq, win=win, rc=rc)
    return pl.pallas_call(
        kern,
        grid=(b, n_pairs, s // tq),
        in_specs=[
            pl.BlockSpec((1, tq, LANES), lambda bi, p, i: (bi, i, 3 * n_pairs + p)),
            pl.BlockSpec((1, s, LANES), lambda bi, p, i: (bi, 0, 4 * n_pairs + p)),
            pl.BlockSpec((1, s, LANES), lambda bi, p, i: (bi, 0, 5 * n_pairs + p)),
            pl.BlockSpec((1, 1, 2, tq + win), lambda bi, p, i: (jnp.minimum(i, 2), p, 0, 0)),
        ],
        out_specs=pl.BlockSpec((1, tq, LANES), lambda bi, p, i: (bi, i, p)),
        out_shape=jax.ShapeDtypeStruct((b, s, ATT_WIDTH), BF16),
        scratch_shapes=[pltpu.VMEM((tq, 2 * win), F32), pltpu.VMEM((tq, 2 * win), BF16),
                        pltpu.VMEM((2, tq, win), F32)],
        compiler_params=_params(("arbitrary", "arbitrary", "arbitrary")),
        name="ca_attention",
    )(proj3, proj3, proj3, rings)


def _merge_kernel(osb_ref, oca_ref, gsb_ref, gca_ref, x_ref, psb_ref, pca_ref, wout_ref, nf_ref,
                  x1_ref, h2_ref):
    br_sb = jnp.dot(osb_ref[...], psb_ref[...], preferred_element_type=F32)
    br_ca = jnp.dot(oca_ref[...], pca_ref[...], preferred_element_type=F32)
    merged = (jax.nn.sigmoid(gsb_ref[...].astype(F32)) * br_sb
              + jax.nn.sigmoid(gca_ref[...].astype(F32)) * br_ca)
    x1 = x_ref[...] + jnp.dot(merged.astype(BF16), wout_ref[...], preferred_element_type=F32)
    x1_ref[...] = x1
    ms = jnp.mean(x1 * x1, axis=-1, keepdims=True)
    h2_ref[...] = x1 * lax.rsqrt(ms + RMS_EPS) * nf_ref[...]


def _merge(o_sb, o_ca, proj, x2, psb, pca, wout, nf, tm):
    n = x2.shape[0]
    gate_blk = 3 * ATT_WIDTH * 2 // D_MODEL
    return pl.pallas_call(
        _merge_kernel,
        grid=(n // tm,),
        in_specs=[
            pl.BlockSpec((tm, ATT_WIDTH), lambda i: (i, 0)),
            pl.BlockSpec((tm, ATT_WIDTH), lambda i: (i, 0)),
            pl.BlockSpec((tm, D_MODEL), lambda i: (i, gate_blk)),
            pl.BlockSpec((tm, D_MODEL), lambda i: (i, gate_blk + 1)),
            pl.BlockSpec((tm, D_MODEL), lambda i: (i, 0)),
            pl.BlockSpec((ATT_WIDTH, D_MODEL), lambda i: (0, 0)),
            pl.BlockSpec((ATT_WIDTH, D_MODEL), lambda i: (0, 0)),
            pl.BlockSpec((D_MODEL, D_MODEL), lambda i: (0, 0)),
            pl.BlockSpec((1, D_MODEL), lambda i: (0, 0)),
        ],
        out_specs=[
            pl.BlockSpec((tm, D_MODEL), lambda i: (i, 0)),
            pl.BlockSpec((tm, D_MODEL), lambda i: (i, 0)),
        ],
        out_shape=[jax.ShapeDtypeStruct((n, D_MODEL), F32), jax.ShapeDtypeStruct((n, D_MODEL), F32)],
        compiler_params=_params(("arbitrary",)),
        name="merge",
    )(o_sb, o_ca, proj, proj, x2, psb, pca, wout, nf)


def _peer_candidates():
    return [(a, b) for a in range(PEER_TOPK) for b in range(PEER_TOPK) if (a + 1) * (b + 1) <= PEER_TOPK]


def _extract_top(s_scr, iota_scr, n_rounds, sentinel):
    vals, ids = [], []
    for _ in range(n_rounds):
        s = s_scr[...]
        m = jnp.max(s, axis=0, keepdims=True)
        cand = jnp.where(s == m, iota_scr[...], sentinel)
        am = jnp.min(cand, axis=0, keepdims=True)
        vals.append(m)
        ids.append(am)
        s_scr[...] = jnp.where(cand == am, -jnp.inf, s)
    return vals, ids


ROUTE_TOKENS = 128


def _route_scratch():
    n_rows = -(-len(_peer_candidates()) // SUBLANES) * SUBLANES
    return [
        pltpu.VMEM((n_rows, ROUTE_TOKENS), F32),
        pltpu.VMEM((n_rows, ROUTE_TOKENS), F32),
        pltpu.VMEM((PEER_SLOTS, ROUTE_TOKENS), F32),
        pltpu.VMEM((PEER_SLOTS, ROUTE_TOKENS), F32),
        pltpu.VMEM((2, PEER_N_KEYS, ROUTE_TOKENS), F32),
        pltpu.VMEM((PEER_N_KEYS, ROUTE_TOKENS), F32),
    ]


def _route_init(scr):
    cv_scr, ci_scr, _, _, _, iota_scr = scr
    n_cand = len(_peer_candidates())
    n_rows = cv_scr.shape[0]
    cv_scr[n_cand:, :] = jnp.full((n_rows - n_cand, ROUTE_TOKENS), -jnp.inf, F32)
    ci_scr[n_cand:, :] = jnp.zeros((n_rows - n_cand, ROUTE_TOKENS), F32)
    iota_scr[...] = lax.broadcasted_iota(jnp.int32, iota_scr.shape, 0).astype(F32)


def _slot_row(hd, r):
    if isinstance(hd, int):
        return slice(hd * PEER_TOPK + r, hd * PEER_TOPK + r + 1)
    return pl.ds(hd * PEER_TOPK + r, 1)


def _route_head(h_ref, wq_ref, keys_ref, hd, scr):
    cv_scr, ci_scr, ri_scr, rg_scr, s_scr, iota_scr = scr
    cands = _peer_candidates()
    n_rows = cv_scr.shape[0]
    h = h_ref[...].astype(BF16)
    iota_c = lax.broadcasted_iota(jnp.int32, (n_rows, ROUTE_TOKENS), 0).astype(F32)
    for p in range(2):
        hp = hd * 2 + p
        q = jnp.dot(h, wq_ref[hp], preferred_element_type=F32).astype(BF16)
        s_scr[p] = lax.dot_general(keys_ref[hp], q, NT_DIMS, preferred_element_type=F32)
    tops = [_extract_top(s_scr.at[p], iota_scr, PEER_TOPK, float(PEER_N_KEYS)) for p in range(2)]
    (v1, i1), (v2, i2) = tops
    i1s = [x * float(PEER_N_KEYS * PACK_ROWS) for x in i1]
    i2s = [x * float(PACK_ROWS) for x in i2]
    for r, (a, b) in enumerate(cands):
        cv_scr[r:r + 1, :] = v1[a] + v2[b]
        ci_scr[r:r + 1, :] = i1s[a] + i2s[b]
    cv = cv_scr[...]
    ci = ci_scr[...]
    best = []
    for r in range(PEER_TOPK):
        m = jnp.max(cv, axis=0, keepdims=True)
        cand = jnp.where(cv == m, iota_c, float(n_rows))
        hit = cand == jnp.min(cand, axis=0, keepdims=True)
        e = jnp.sum(jnp.where(hit, ci, 0.0), axis=0, keepdims=True)
        cv = jnp.where(hit, -jnp.inf, cv)
        best.append(m)
        ri_scr[_slot_row(hd, r), :] = e
    ex = [jnp.exp(bv - best[0]) for bv in best]
    tot = ex[0]
    for t in ex[1:]:
        tot = tot + t
    inv = 1.0 / tot
    for r in range(PEER_TOPK):
        rg_scr[_slot_row(hd, r), :] = ex[r] * inv


def _route_emit(idx_ref, g_ref, ri_scr, rg_scr):
    idx_ref[...] = ri_scr[...].T.astype(jnp.int32)
    g_ref[...] = rg_scr[...].T


def _route_kernel(h_ref, wq_ref, keys_ref, idx_ref, g_ref, *scr):
    _route_init(scr)
    for hd in range(PEER_HEADS):
        _route_head(h_ref, wq_ref, keys_ref, hd, scr)
    _route_emit(idx_ref, g_ref, scr[2], scr[3])


def _route_in_specs(block_index):
    return [
        pl.BlockSpec((ROUTE_TOKENS, D_MODEL), block_index),
        pl.BlockSpec((2 * PEER_HEADS, D_MODEL, PEER_HALF), lambda i: (0, 0, 0)),
        pl.BlockSpec((2 * PEER_HEADS, PEER_N_KEYS, PEER_HALF), lambda i: (0, 0, 0)),
    ]


def _route_out(n, block_index):
    specs = [pl.BlockSpec((ROUTE_TOKENS, PEER_SLOTS), block_index),
             pl.BlockSpec((ROUTE_TOKENS, PEER_SLOTS), block_index)]
    shapes = [jax.ShapeDtypeStruct((n, PEER_SLOTS), jnp.int32), jax.ShapeDtypeStruct((n, PEER_SLOTS), F32)]
    return specs, shapes


def _route(h2, wq3, keys_bf, n, tok_off=0):
    out_specs, out_shape = _route_out(n, lambda i: (i, 0))
    return pl.pallas_call(
        _route_kernel,
        grid=(n // ROUTE_TOKENS,),
        in_specs=_route_in_specs(lambda i: (i + tok_off // ROUTE_TOKENS, 0)),
        out_specs=out_specs,
        out_shape=out_shape,
        scratch_shapes=_route_scratch(),
        compiler_params=_params(("arbitrary",)),
        name="peer_route",
    )(h2, wq3, keys_bf)


def _pack_table(t):
    e = t.shape[0]
    pairs = jnp.swapaxes(t.astype(BF16).reshape(e, PACK_ROWS, 2, LANES), 2, 3)
    return lax.bitcast_convert_type(pairs, jnp.uint32).reshape(e * PACK_ROWS, LANES)


def _table_spec(n_rows):
    return pl.BlockSpec((n_rows, LANES), lambda i: (0, 0), pipeline_mode=pl.Buffered(1))


def _gather_rows(idx_ref, t, tab_ref, buf):
    for k in range(PEER_SLOTS):
        e = pl.multiple_of(idx_ref[t, k], PACK_ROWS)
        buf[k * PACK_ROWS:(k + 1) * PACK_ROWS, :] = tab_ref[pl.ds(e, PACK_ROWS), :]


def _gathered_bf16(buf):
    return pltpu.bitcast(buf[...], BF16)


def _chunk_pattern():
    shape = (2 * SUBLANES, PEER_SLOTS * SUBLANES)
    row = lax.broadcasted_iota(jnp.int32, shape, 0)
    col = lax.broadcasted_iota(jnp.int32, shape, 1)
    return (row & (SUBLANES - 1)) == (col & (SUBLANES - 1))


def _token_rows(t):
    if isinstance(t, int):
        return slice(t * SUBLANES, (t + 1) * SUBLANES)
    return pl.ds(pl.multiple_of(t * SUBLANES, SUBLANES), SUBLANES)


def _load_token_chunks(ref, t):
    row = ref[pl.ds(t, 1), :]
    return jnp.concatenate([row[:, j * LANES:(j + 1) * LANES] for j in range(SUBLANES)], axis=0)


def _store_token_chunks(ref, t, val):
    ref[pl.ds(t, 1), :] = jnp.concatenate([val[j:j + 1, :] for j in range(SUBLANES)], axis=1)


def _split_bf16(x):
    hi = x.astype(BF16)
    return hi, (x - hi.astype(F32)).astype(BF16)


def _token_loop(idx_ref, tab_ref, bufs, tb, unroll, compute, side_work=None):
    _gather_rows(idx_ref, 0, tab_ref, bufs[0])

    def body(i, carry):
        if side_work is not None:
            side_work(i)
        for u in range(unroll):
            t = i * unroll + u
            _gather_rows(idx_ref, jnp.minimum(t + 1, tb - 1), tab_ref, bufs[(u + 1) % 2])
            compute(t, bufs[u % 2])
        return carry

    lax.fori_loop(0, tb // unroll, body, 0)


def _peer_dot_body(idx_ref, h_ref, g_ref, tab_ref, c_ref, buf0, buf1, part_scr, tb, unroll, side_work):
    pattern = _chunk_pattern()

    def compute(t, buf):
        h16 = jnp.concatenate(_split_bf16(h_ref[_token_rows(t), :]), axis=0)
        r = lax.dot_general(h16, _gathered_bf16(buf), NT_DIMS, preferred_element_type=F32)
        r = jnp.where(pattern, r, 0.0)
        part_scr[pl.ds(t, 1), :] = jnp.sum(r[:SUBLANES] + r[SUBLANES:], axis=0, keepdims=True)

    _token_loop(idx_ref, tab_ref, (buf0, buf1), tb, unroll, compute, side_work)

    n_part = PEER_SLOTS * SUBLANES
    slot_of = lax.broadcasted_iota(jnp.int32, (n_part, PEER_SLOTS), 0) // SUBLANES
    fold = jnp.where(slot_of == lax.broadcasted_iota(jnp.int32, (n_part, PEER_SLOTS), 1), 1.0, 0.0).astype(BF16)
    p_hi, p_lo = _split_bf16(part_scr[...])
    a = (jnp.dot(p_hi, fold, preferred_element_type=F32)
         + jnp.dot(p_lo, fold, preferred_element_type=F32))
    c_ref[...] = g_ref[...] * (0.5 * a * (1.0 + lax.erf(a * (1.0 / math.sqrt(2.0)))))


def _peer_dot_kernel(idx_ref, h_ref, g_ref, tab_ref, c_ref, buf0, buf1, part_scr, *, tb, unroll):
    _peer_dot_body(idx_ref, h_ref, g_ref, tab_ref, c_ref, buf0, buf1, part_scr, tb, unroll, None)


def _peer_dot_route_kernel(idx_ref, h_ref, g_ref, tab_ref, rh_ref, wq_ref, keys_ref,
                           c_ref, ridx_ref, rg_ref, buf0, buf1, part_scr, *route_scr, tb, unroll):
    trips = tb // unroll
    phase = pl.program_id(0) % (PEER_HEADS // trips)
    _route_init(route_scr)

    def side_work(i):
        _route_head(rh_ref, wq_ref, keys_ref, phase * trips + i, route_scr)

    _peer_dot_body(idx_ref, h_ref, g_ref, tab_ref, c_ref, buf0, buf1, part_scr, tb, unroll, side_work)

    @pl.when(phase == PEER_HEADS // trips - 1)
    def _():
        _route_emit(ridx_ref, rg_ref, route_scr[2], route_scr[3])


def _peer_dot_specs(tab, tb, h_block_off):
    in_specs = [
        pl.BlockSpec((tb, PEER_SLOTS), lambda i: (i, 0), memory_space=pltpu.SMEM),
        pl.BlockSpec((tb * SUBLANES, LANES), lambda i: (i + h_block_off, 0)),
        pl.BlockSpec((tb, PEER_SLOTS), lambda i: (i, 0)),
        _table_spec(tab.shape[0]),
    ]
    scratch = [pltpu.VMEM((PEER_SLOTS * PACK_ROWS, LANES), jnp.uint32),
               pltpu.VMEM((PEER_SLOTS * PACK_ROWS, LANES), jnp.uint32),
               pltpu.VMEM((tb, PEER_SLOTS * SUBLANES), F32)]
    return in_specs, scratch


def _peer_dot(idx, h2r, gate, tab, tb, unroll, tok_off=0):
    n = idx.shape[0]
    in_specs, scratch = _peer_dot_specs(tab, tb, tok_off // tb)
    return pl.pallas_call(
        functools.partial(_peer_dot_kernel, tb=tb, unroll=unroll),
        grid=(n // tb,),
        in_specs=in_specs,
        out_specs=pl.BlockSpec((tb, PEER_SLOTS), lambda i: (i, 0)),
        out_shape=jax.ShapeDtypeStruct((n, PEER_SLOTS), F32),
        scratch_shapes=scratch,
        compiler_params=_params(("arbitrary",)),
        name="peer_dot",
    )(idx, h2r, gate, tab)


def _peer_dot_route(idx, h2r, gate, tab, h2, wq3, keys_bf, tb, unroll, tok_off, route_off):
    n = idx.shape[0]
    assert PEER_HEADS % (tb // unroll) == 0
    steps_per_block = PEER_HEADS // (tb // unroll)
    assert steps_per_block * tb == ROUTE_TOKENS and route_off % ROUTE_TOKENS == 0
    in_specs, scratch = _peer_dot_specs(tab, tb, tok_off // tb)
    route_in_block = lambda i: (i // steps_per_block + route_off // ROUTE_TOKENS, 0)
    route_block = lambda i: (i // steps_per_block, 0)
    route_out_specs, route_out_shape = _route_out(n, route_block)
    return pl.pallas_call(
        functools.partial(_peer_dot_route_kernel, tb=tb, unroll=unroll),
        grid=(n // tb,),
        in_specs=in_specs + _route_in_specs(route_in_block),
        out_specs=[pl.BlockSpec((tb, PEER_SLOTS), lambda i: (i, 0))] + route_out_specs,
        out_shape=[jax.ShapeDtypeStruct((n, PEER_SLOTS), F32)] + route_out_shape,
        scratch_shapes=scratch + _route_scratch(),
        compiler_params=_params(("arbitrary",)),
        name="peer_dot_route",
    )(idx, h2r, gate, tab, h2, wq3, keys_bf)


def _peer_axpy_kernel(idx_ref, c_ref, x_ref, gf_ref, tab_ref, y_ref, buf0, buf1, cexp_scr, *, tb, unroll):
    pattern = _chunk_pattern()
    n_part = PEER_SLOTS * SUBLANES
    slot_of = lax.broadcasted_iota(jnp.int32, (PEER_SLOTS, n_part), 1) // SUBLANES
    spread = jnp.where(slot_of == lax.broadcasted_iota(jnp.int32, (PEER_SLOTS, n_part), 0), 1.0, 0.0).astype(BF16)
    c_hi, c_lo = _split_bf16(c_ref[...])
    cexp_scr[0:tb, :] = jnp.dot(c_hi, spread, preferred_element_type=F32)
    cexp_scr[tb:2 * tb, :] = jnp.dot(c_lo, spread, preferred_element_type=F32)

    def compute(t, buf):
        coef = jnp.concatenate([jnp.broadcast_to(cexp_scr[pl.ds(t, 1), :], (SUBLANES, n_part)),
                                jnp.broadcast_to(cexp_scr[pl.ds(tb + t, 1), :], (SUBLANES, n_part))], axis=0)
        coef = jnp.where(pattern, coef, 0.0).astype(BF16)
        o16 = jnp.dot(coef, _gathered_bf16(buf), preferred_element_type=F32)
        x2 = _load_token_chunks(x_ref, t) + (o16[:SUBLANES] + o16[SUBLANES:])
        ss = jnp.sum(jnp.sum(x2 * x2, axis=1, keepdims=True), axis=0, keepdims=True)
        _store_token_chunks(y_ref, t, x2 * lax.rsqrt(ss * (1.0 / D_MODEL) + RMS_EPS) * gf_ref[...])

    _token_loop(idx_ref, tab_ref, (buf0, buf1), tb, unroll, compute)


def _peer_axpy(idx, c, x1r, gfr, tab, tb, unroll):
    n = idx.shape[0]
    kern = functools.partial(_peer_axpy_kernel, tb=tb, unroll=unroll)
    return pl.pallas_call(
        kern,
        grid=(n // tb,),
        in_specs=[
            pl.BlockSpec((tb, PEER_SLOTS), lambda i: (i, 0), memory_space=pltpu.SMEM),
            pl.BlockSpec((tb, PEER_SLOTS), lambda i: (i, 0)),
            pl.BlockSpec((tb, D_MODEL), lambda i: (i, 0)),
            pl.BlockSpec((SUBLANES, LANES), lambda i: (0, 0)),
            _table_spec(tab.shape[0]),
        ],
        out_specs=pl.BlockSpec((tb, D_MODEL), lambda i: (i, 0)),
        out_shape=jax.ShapeDtypeStruct((n, D_MODEL), F32),
        scratch_shapes=[pltpu.VMEM((PEER_SLOTS * PACK_ROWS, LANES), jnp.uint32),
                        pltpu.VMEM((PEER_SLOTS * PACK_ROWS, LANES), jnp.uint32),
                        pltpu.VMEM((2 * tb, PEER_SLOTS * SUBLANES), F32)],
        compiler_params=_params(("arbitrary",)),
        name="peer_axpy",
    )(idx, c, x1r, gfr, tab)


def _row_tile(n, want):
    t = min(want, n)
    assert n % t == 0
    return t


def _layer(x2, b, s, norm_mix, w_in, w_sb_proj, w_ca_proj, ca_rel_bias, w_out, norm_ffn,
           peer_w_query, peer_sub_keys, peer_u, peer_v, final_gain):
    n = b * s
    col = np.arange(IN_COLS)
    is_q = (col < ATT_WIDTH) | ((col >= 3 * ATT_WIDTH) & (col < 4 * ATT_WIDTH))
    col_scale = np.where(is_q, math.log2(math.e) / math.sqrt(HEAD_DIM), 1.0)
    w_in_bf = (w_in * jnp.asarray(col_scale, F32)[None, :]).astype(BF16)

    proj = _inproj(x2, norm_mix.reshape(1, D_MODEL), w_in_bf, _row_tile(n, 512))
    proj3 = proj.reshape(b, s, IN_COLS)

    tq = 256
    assert s % tq == 0 and s >= tq + CA_PAD
    sb_tq = min(2048, s)
    assert s % sb_tq == 0
    o_sb = _sb_attention(proj3, sb_tq, 256)
    win = tq + CA_PAD
    o_ca = _ca_attention(proj3, _ca_rings(ca_rel_bias, tq, win), tq, win, 64)

    x1, h2 = _merge(o_sb.reshape(n, ATT_WIDTH), o_ca.reshape(n, ATT_WIDTH), proj, x2,
                    w_sb_proj.astype(BF16), w_ca_proj.astype(BF16), w_out.astype(BF16),
                    norm_ffn.reshape(1, D_MODEL), _row_tile(n, 512))

    keys_bf = peer_sub_keys.reshape(2 * PEER_HEADS, PEER_N_KEYS, PEER_HALF).astype(BF16)
    wq3 = jnp.transpose(peer_w_query.astype(BF16).reshape(D_MODEL, 2 * PEER_HEADS, PEER_HALF), (1, 0, 2))
    tab_u = _pack_table(peer_u)
    h2r = h2.reshape(n * SUBLANES, LANES)
    tbg, unroll = 64, 16
    n_seg = b if (b > 1 and s % ROUTE_TOKENS == 0) else 1
    seg = n // n_seg
    assert seg % ROUTE_TOKENS == 0
    idx_j, gate_j = _route(h2, wq3, keys_bf, seg)
    idx_parts, c_parts = [], []
    for j in range(n_seg):
        idx_parts.append(idx_j)
        if j + 1 < n_seg:
            c_j, idx_j, gate_j = _peer_dot_route(idx_j, h2r, gate_j, tab_u, h2, wq3, keys_bf, tbg, unroll,
                                                 j * seg, (j + 1) * seg)
        else:
            c_j = _peer_dot(idx_j, h2r, gate_j, tab_u, tbg, unroll, j * seg)
        c_parts.append(c_j)
    idx = jnp.concatenate(idx_parts, axis=0)
    c = jnp.concatenate(c_parts, axis=0)
    return _peer_axpy(idx, c, x1, final_gain.reshape(SUBLANES, LANES), _pack_table(peer_v), tbg, unroll)


def kernel(x, norm_mix, w_in, w_sb_proj, w_ca_proj, ca_rel_bias, w_out, norm_ffn,
           peer_w_query, peer_sub_keys, peer_u, peer_v, norm_final):
    b, s, d = x.shape
    depth = norm_mix.shape[0]
    assert d == D_MODEL and depth == 1
    y = _layer(x.reshape(b * s, d), b, s, norm_mix[0], w_in[0], w_sb_proj[0], w_ca_proj[0],
               ca_rel_bias[0], w_out[0], norm_ffn[0], peer_w_query[0], peer_sub_keys[0],
               peer_u[0], peer_v[0], norm_final)
    return y.reshape(b, s, d)
```

```python
import functools
import math

import numpy as np
import jax
import jax.numpy as jnp
from jax import lax
from jax.experimental import pallas as pl
from jax.experimental.pallas import tpu as pltpu

D_MODEL = 1024
HEAD_DIM = 64
N_HEADS = 8
ATT_WIDTH = N_HEADS * HEAD_DIM
IN_COLS = 6 * ATT_WIDTH + 2 * D_MODEL
CHUNK = 64
CA_LEFT_CHUNKS = 8
CA_PAD = CA_LEFT_CHUNKS * CHUNK
REL_CLIP = 128
PEER_HEADS = 8
PEER_N_KEYS = 128
PEER_HALF = 128
PEER_TOPK = 16
PEER_SLOTS = PEER_HEADS * PEER_TOPK
RMS_EPS = 1e-6
NEG_INF = -1e30

LANES = 128
SUBLANES = 8
PACK_ROWS = D_MODEL // 2 // LANES

VMEM_LIMIT = 56 * 1024 * 1024

F32 = jnp.float32
BF16 = jnp.bfloat16
NT_DIMS = (((1,), (1,)), ((), ()))


def _params(sem, vmem=VMEM_LIMIT):
    return pltpu.CompilerParams(dimension_semantics=sem, vmem_limit_bytes=vmem)


def _inproj_kernel(x_ref, g_ref, w_ref, o_ref):
    x = x_ref[...]
    ms = jnp.mean(x * x, axis=-1, keepdims=True)
    h = (x * lax.rsqrt(ms + RMS_EPS) * g_ref[...]).astype(BF16)
    for c in range(IN_COLS // D_MODEL):
        cols = slice(c * D_MODEL, (c + 1) * D_MODEL)
        o_ref[:, cols] = jnp.dot(h, w_ref[:, cols], preferred_element_type=F32).astype(o_ref.dtype)


def _inproj(x2, g, w_bf, tm):
    n = x2.shape[0]
    return pl.pallas_call(
        _inproj_kernel,
        grid=(n // tm,),
        in_specs=[
            pl.BlockSpec((tm, D_MODEL), lambda i: (i, 0)),
            pl.BlockSpec((1, D_MODEL), lambda i: (0, 0)),
            pl.BlockSpec((D_MODEL, IN_COLS), lambda i: (0, 0)),
        ],
        out_specs=pl.BlockSpec((tm, IN_COLS), lambda i: (i, 0)),
        out_shape=jax.ShapeDtypeStruct((n, IN_COLS), BF16),
        compiler_params=_params(("arbitrary",)),
        name="inproj",
    )(x2, g, w_bf)


def _sb_kernel(q_ref, k_ref, v_ref, o_ref, acc_ref, r_ref, z_scr, lat_scr, *, tq, tk):
    i = pl.program_id(2)
    n_chunks = tq // tk
    lane = lax.broadcasted_iota(jnp.int32, (1, LANES), 1)
    head_mask = (lane < HEAD_DIM, lane >= HEAD_DIM)
    kk = lax.broadcasted_iota(jnp.int32, (tk, tk), 0)
    ss = lax.broadcasted_iota(jnp.int32, (tk, tk), 1)
    stri = jnp.where(kk > ss, 1.0, 0.0).astype(BF16)
    causal = ss < kk
    sign = jnp.uint32(0x80000000)

    acc_ref[...] = jnp.zeros_like(acc_ref)
    r_ref[...] = jnp.zeros_like(r_ref)

    def rows(c):
        return slice(c * tk, (c + 1) * tk)

    def both_heads(x):
        zero = jnp.zeros_like(x)
        return jnp.concatenate([jnp.where(head_mask[0], x, zero), jnp.where(head_mask[1], x, zero)], axis=0)

    def logits(j, chunks):
        k2 = both_heads(k_ref[0, pl.ds(pl.multiple_of(j * tk, tk), tk), :])
        for c in chunks:
            z_scr[c] = lax.dot_general(q_ref[0, rows(c), :], k2, NT_DIMS, preferred_element_type=F32)

    def later_sums(chunks, diag_chunk):
        for c in chunks:
            for h in range(2):
                cols = slice(h * tk, (h + 1) * tk)
                z = z_scr[c, :, cols]
                neg_abs = lax.bitcast_convert_type(lax.bitcast_convert_type(z, jnp.uint32) | sign, F32)
                sp = jnp.maximum(z, 0.0) + jnp.log2(1.0 + jnp.exp2(neg_abs))
                if c == diag_chunk:
                    sp = jnp.where(causal, sp, 0.0)
                z_scr[c, :, cols] = z - sp
                r = r_ref[h, rows(c), :]
                later = jnp.concatenate([r] * (tk // LANES), axis=1) + jnp.dot(
                    sp.astype(BF16), stri, preferred_element_type=F32)
                lat_scr[c, :, cols] = later
                r_ref[h, rows(c), :] = jnp.broadcast_to(later[:, 0:1] + sp[:, 0:1], (tk, LANES))

    def weigh(j, chunks, diag_chunk):
        v2 = both_heads(v_ref[0, pl.ds(pl.multiple_of(j * tk, tk), tk), :])
        for c in chunks:
            ws = []
            for h in range(2):
                cols = slice(h * tk, (h + 1) * tk)
                logw = z_scr[c, :, cols] - lat_scr[c, :, cols]
                if c == diag_chunk:
                    logw = jnp.where(causal, logw, NEG_INF)
                ws.append(jnp.exp2(logw).astype(BF16))
            acc_ref[rows(c), :] += jnp.dot(jnp.concatenate(ws, axis=1), v2, preferred_element_type=F32)

    all_chunks = list(range(n_chunks))
    base = i * n_chunks
    for d in range(n_chunks - 1, -1, -1):
        chunks = list(range(d, n_chunks))
        logits(base + d, chunks)
        later_sums(chunks, d)
        weigh(base + d, chunks, d)

    def body(jj, carry):
        j = base - 1 - jj
        logits(j, all_chunks)
        later_sums(all_chunks, None)
        weigh(j, all_chunks, None)
        return carry

    lax.fori_loop(0, base, body, 0)
    o_ref[0] = acc_ref[...].astype(o_ref.dtype)


def _sb_attention(proj3, tq, tk):
    b, s, _ = proj3.shape
    n_pairs = ATT_WIDTH // LANES
    kern = functools.partial(_sb_kernel, tq=tq, tk=tk)
    return pl.pallas_call(
        kern,
        grid=(b, n_pairs, s // tq),
        in_specs=[
            pl.BlockSpec((1, tq, LANES), lambda bi, p, i: (bi, i, p)),
            pl.BlockSpec((1, s, LANES), lambda bi, p, i: (bi, 0, n_pairs + p)),
            pl.BlockSpec((1, s, LANES), lambda bi, p, i: (bi, 0, 2 * n_pairs + p)),
        ],
        out_specs=pl.BlockSpec((1, tq, LANES), lambda bi, p, i: (bi, i, p)),
        out_shape=jax.ShapeDtypeStruct((b, s, ATT_WIDTH), BF16),
        scratch_shapes=[
            pltpu.VMEM((tq, LANES), F32),
            pltpu.VMEM((2, tq, LANES), F32),
            pltpu.VMEM((tq // tk, tk, 2 * tk), F32),
            pltpu.VMEM((tq // tk, tk, 2 * tk), F32),
        ],
        compiler_params=_params(("arbitrary", "arbitrary", "arbitrary")),
        name="sb_attention",
    )(proj3, proj3, proj3)


def _ca_kernel(q_ref, k_ref, v_ref, ring_ref, o_ref, z_scr, p_scr, b_scr, *, tq, win, rc):
    i = pl.program_id(2)
    ws = pl.multiple_of(jnp.maximum(i * tq - CA_PAD, 0), tq)

    @pl.when(i <= 2)
    def _():
        off = i * tq - ws
        r = lax.broadcasted_iota(jnp.int32, (tq, win), 0)
        c = lax.broadcasted_iota(jnp.int32, (tq, win), 1)
        relc = ((off + r) >> 6) - (c >> 6)
        valid = (relc >= 0) & (relc <= CA_LEFT_CHUNKS)
        for h in range(2):
            ring = jnp.broadcast_to(ring_ref[0, 0, h:h + 1, :], (tq, ring_ref.shape[-1]))
            toep = pltpu.roll(ring, 0, 1, stride=1, stride_axis=0)
            b_scr[h] = jnp.where(valid, toep[:, :win], NEG_INF)

    lane = lax.broadcasted_iota(jnp.int32, (1, LANES), 1)
    head_mask = (lane < HEAD_DIM, lane >= HEAD_DIM)
    k = k_ref[0, pl.ds(ws, win), :]
    v = v_ref[0, pl.ds(ws, win), :]
    zero = jnp.zeros_like(k)
    k2 = jnp.concatenate([jnp.where(head_mask[0], k, zero), jnp.where(head_mask[1], k, zero)], axis=0)
    z_scr[...] = lax.dot_general(q_ref[0], k2, NT_DIMS, preferred_element_type=F32)
    inv = [[None] * (tq // rc) for _ in range(2)]
    for h in range(2):
        for c in range(tq // rc):
            rows = slice(c * rc, (c + 1) * rc)
            cols = slice(h * win, (h + 1) * win)
            z = z_scr[rows, cols] + b_scr[h, rows, :]
            p = jnp.exp2(z - jnp.max(z, axis=1, keepdims=True))
            inv[h][c] = 1.0 / jnp.sum(p, axis=1, keepdims=True)
            p_scr[rows, cols] = p.astype(BF16)
    outs = [jnp.dot(p_scr[:, h * win:(h + 1) * win], jnp.where(head_mask[h], v, zero),
                    preferred_element_type=F32) for h in range(2)]
    o_ref[0] = (outs[0] * jnp.concatenate(inv[0], axis=0)
                + outs[1] * jnp.concatenate(inv[1], axis=0)).astype(o_ref.dtype)


def _ca_rings(rel_table, tq, win):
    ring_len = tq + win
    out = []
    for var in range(3):
        off = var * tq - max(var * tq - CA_PAD, 0)
        dist = np.concatenate([off - np.arange(win), [0], off + np.arange(tq - 1, 0, -1)])
        out.append(rel_table.astype(F32)[:, np.clip(dist, -REL_CLIP, REL_CLIP) + REL_CLIP])
    rings = jnp.stack(out) * math.log2(math.e)
    return rings.reshape(3, rel_table.shape[0] // 2, 2, ring_len)


def _ca_attention(proj3, rings, tq, win, rc):
    b, s, _ = proj3.shape
    n_pairs = ATT_WIDTH // LANES
    assert CHUNK == 64 and (tq + win) % LANES == 0
    kern = functools.partial(_ca_kernel, tq=tq, win=win, rc=rc)
    return pl.pallas_call(
        kern,
        grid=(b, n_pairs, s // tq),
        in_specs=[
            pl.BlockSpec((1, tq, LANES), lambda bi, p, i: (bi, i, 3 * n_pairs + p)),
            pl.BlockSpec((1, s, LANES), lambda bi, p, i: (bi, 0, 4 * n_pairs + p)),
            pl.BlockSpec((1, s, LANES), lambda bi, p, i: (bi, 0, 5 * n_pairs + p)),
            pl.BlockSpec((1, 1, 2, tq + win), lambda bi, p, i: (jnp.minimum(i, 2), p, 0, 0)),
        ],
        out_specs=pl.BlockSpec((1, tq, LANES), lambda bi, p, i: (bi, i, p)),
        out_shape=jax.ShapeDtypeStruct((b, s, ATT_WIDTH), BF16),
        scratch_shapes=[pltpu.VMEM((tq, 2 * win), F32), pltpu.VMEM((tq, 2 * win), BF16),
                        pltpu.VMEM((2, tq, win), F32)],
        compiler_params=_params(("arbitrary", "arbitrary", "arbitrary")),
        name="ca_attention",
    )(proj3, proj3, proj3, rings)


def _merge_kernel(osb_ref, oca_ref, gsb_ref, gca_ref, x_ref, psb_ref, pca_ref, wout_ref, nf_ref,
                  x1_ref, h2_ref, h2c_ref):
    br_sb = jnp.dot(osb_ref[...], psb_ref[...], preferred_element_type=F32)
    br_ca = jnp.dot(oca_ref[...], pca_ref[...], preferred_element_type=F32)
    merged = (jax.nn.sigmoid(gsb_ref[...].astype(F32)) * br_sb
              + jax.nn.sigmoid(gca_ref[...].astype(F32)) * br_ca)
    x1 = x_ref[...] + jnp.dot(merged.astype(BF16), wout_ref[...], preferred_element_type=F32)
    x1_ref[...] = x1
    ms = jnp.mean(x1 * x1, axis=-1, keepdims=True)
    h2 = x1 * lax.rsqrt(ms + RMS_EPS) * nf_ref[...]
    h2_ref[...] = h2
    tm = h2.shape[0]
    for j in range(SUBLANES):
        h2c_ref[pl.ds(j, tm, stride=SUBLANES), :] = h2[:, j * LANES:(j + 1) * LANES]


def _merge(o_sb, o_ca, proj, x2, psb, pca, wout, nf, tm):
    n = x2.shape[0]
    gate_blk = 3 * ATT_WIDTH * 2 // D_MODEL
    return pl.pallas_call(
        _merge_kernel,
        grid=(n // tm,),
        in_specs=[
            pl.BlockSpec((tm, ATT_WIDTH), lambda i: (i, 0)),
            pl.BlockSpec((tm, ATT_WIDTH), lambda i: (i, 0)),
            pl.BlockSpec((tm, D_MODEL), lambda i: (i, gate_blk)),
            pl.BlockSpec((tm, D_MODEL), lambda i: (i, gate_blk + 1)),
            pl.BlockSpec((tm, D_MODEL), lambda i: (i, 0)),
            pl.BlockSpec((ATT_WIDTH, D_MODEL), lambda i: (0, 0)),
            pl.BlockSpec((ATT_WIDTH, D_MODEL), lambda i: (0, 0)),
            pl.BlockSpec((D_MODEL, D_MODEL), lambda i: (0, 0)),
            pl.BlockSpec((1, D_MODEL), lambda i: (0, 0)),
        ],
        out_specs=[
            pl.BlockSpec((tm, D_MODEL), lambda i: (i, 0)),
            pl.BlockSpec((tm, D_MODEL), lambda i: (i, 0)),
            pl.BlockSpec((tm * SUBLANES, LANES), lambda i: (i, 0)),
        ],
        out_shape=[jax.ShapeDtypeStruct((n, D_MODEL), F32), jax.ShapeDtypeStruct((n, D_MODEL), F32),
                   jax.ShapeDtypeStruct((n * SUBLANES, LANES), F32)],
        compiler_params=_params(("arbitrary",)),
        name="merge",
    )(o_sb, o_ca, proj, proj, x2, psb, pca, wout, nf)


def _peer_candidates():
    return [(a, b) for a in range(PEER_TOPK) for b in range(PEER_TOPK) if (a + 1) * (b + 1) <= PEER_TOPK]


def _extract_top(s_scr, iota_scr, n_rounds, sentinel):
    vals, ids = [], []
    for _ in range(n_rounds):
        s = s_scr[...]
        m = jnp.max(s, axis=0, keepdims=True)
        cand = jnp.where(s == m, iota_scr[...], sentinel)
        am = jnp.min(cand, axis=0, keepdims=True)
        vals.append(m)
        ids.append(am)
        s_scr[...] = jnp.where(cand == am, -jnp.inf, s)
    return vals, ids


ROUTE_TOKENS = 128


def _route_scratch():
    n_rows = -(-len(_peer_candidates()) // SUBLANES) * SUBLANES
    return [
        pltpu.VMEM((n_rows, ROUTE_TOKENS), F32),
        pltpu.VMEM((n_rows, ROUTE_TOKENS), F32),
        pltpu.VMEM((PEER_SLOTS, ROUTE_TOKENS), F32),
        pltpu.VMEM((PEER_SLOTS, ROUTE_TOKENS), F32),
        pltpu.VMEM((2, PEER_N_KEYS, ROUTE_TOKENS), F32),
        pltpu.VMEM((PEER_N_KEYS, ROUTE_TOKENS), F32),
    ]


def _route_init(scr):
    cv_scr, ci_scr, _, _, _, iota_scr = scr
    n_cand = len(_peer_candidates())
    n_rows = cv_scr.shape[0]
    cv_scr[n_cand:, :] = jnp.full((n_rows - n_cand, ROUTE_TOKENS), -jnp.inf, F32)
    ci_scr[n_cand:, :] = jnp.zeros((n_rows - n_cand, ROUTE_TOKENS), F32)
    iota_scr[...] = lax.broadcasted_iota(jnp.int32, iota_scr.shape, 0).astype(F32)


def _slot_row(hd, r):
    if isinstance(hd, int):
        return slice(hd * PEER_TOPK + r, hd * PEER_TOPK + r + 1)
    return pl.ds(hd * PEER_TOPK + r, 1)


def _route_head(h_ref, wq_ref, keys_ref, hd, scr):
    cv_scr, ci_scr, ri_scr, rg_scr, s_scr, iota_scr = scr
    cands = _peer_candidates()
    n_rows = cv_scr.shape[0]
    h = h_ref[...].astype(BF16)
    iota_c = lax.broadcasted_iota(jnp.int32, (n_rows, ROUTE_TOKENS), 0).astype(F32)
    for p in range(2):
        hp = hd * 2 + p
        q = jnp.dot(h, wq_ref[hp], preferred_element_type=F32).astype(BF16)
        s_scr[p] = lax.dot_general(keys_ref[hp], q, NT_DIMS, preferred_element_type=F32)
    tops = [_extract_top(s_scr.at[p], iota_scr, PEER_TOPK, float(PEER_N_KEYS)) for p in range(2)]
    (v1, i1), (v2, i2) = tops
    i1s = [x * float(PEER_N_KEYS * PACK_ROWS) for x in i1]
    i2s = [x * float(PACK_ROWS) for x in i2]
    for r, (a, b) in enumerate(cands):
        cv_scr[r:r + 1, :] = v1[a] + v2[b]
        ci_scr[r:r + 1, :] = i1s[a] + i2s[b]
    cv = cv_scr[...]
    ci = ci_scr[...]
    best = []
    for r in range(PEER_TOPK):
        m = jnp.max(cv, axis=0, keepdims=True)
        cand = jnp.where(cv == m, iota_c, float(n_rows))
        hit = cand == jnp.min(cand, axis=0, keepdims=True)
        e = jnp.sum(jnp.where(hit, ci, 0.0), axis=0, keepdims=True)
        cv = jnp.where(hit, -jnp.inf, cv)
        best.append(m)
        ri_scr[_slot_row(hd, r), :] = e
    ex = [jnp.exp(bv - best[0]) for bv in best]
    tot = ex[0]
    for t in ex[1:]:
        tot = tot + t
    inv = 1.0 / tot
    for r in range(PEER_TOPK):
        rg_scr[_slot_row(hd, r), :] = ex[r] * inv


def _route_emit(idx_ref, g_ref, ri_scr, rg_scr):
    idx_ref[...] = ri_scr[...].T.astype(jnp.int32)
    g_ref[...] = rg_scr[...].T


def _route_kernel(h_ref, wq_ref, keys_ref, idx_ref, g_ref, *scr):
    _route_init(scr)
    for hd in range(PEER_HEADS):
        _route_head(h_ref, wq_ref, keys_ref, hd, scr)
    _route_emit(idx_ref, g_ref, scr[2], scr[3])


def _route_in_specs(block_index):
    return [
        pl.BlockSpec((ROUTE_TOKENS, D_MODEL), block_index),
        pl.BlockSpec((2 * PEER_HEADS, D_MODEL, PEER_HALF), lambda i: (0, 0, 0)),
        pl.BlockSpec((2 * PEER_HEADS, PEER_N_KEYS, PEER_HALF), lambda i: (0, 0, 0)),
    ]


def _route_out(n, block_index):
    specs = [pl.BlockSpec((ROUTE_TOKENS, PEER_SLOTS), block_index),
             pl.BlockSpec((ROUTE_TOKENS, PEER_SLOTS), block_index)]
    shapes = [jax.ShapeDtypeStruct((n, PEER_SLOTS), jnp.int32), jax.ShapeDtypeStruct((n, PEER_SLOTS), F32)]
    return specs, shapes


def _route(h2, wq3, keys_bf, n, tok_off=0):
    out_specs, out_shape = _route_out(n, lambda i: (i, 0))
    return pl.pallas_call(
        _route_kernel,
        grid=(n // ROUTE_TOKENS,),
        in_specs=_route_in_specs(lambda i: (i + tok_off // ROUTE_TOKENS, 0)),
        out_specs=out_specs,
        out_shape=out_shape,
        scratch_shapes=_route_scratch(),
        compiler_params=_params(("arbitrary",)),
        name="peer_route",
    )(h2, wq3, keys_bf)


def _pack_kernel(t_ref, o_ref):
    te = t_ref.shape[0]
    for r in range(PACK_ROWS):
        lo = t_ref[:, (2 * r) * LANES:(2 * r + 1) * LANES].astype(BF16).astype(F32)
        hi = t_ref[:, (2 * r + 1) * LANES:(2 * r + 2) * LANES].astype(BF16).astype(F32)
        word = lax.bitcast_convert_type(hi, jnp.uint32) | (lax.bitcast_convert_type(lo, jnp.uint32) >> 16)
        o_ref[pl.ds(r, te, stride=PACK_ROWS), :] = word


def _pack_table(t):
    e = t.shape[0]
    te = _row_tile(e, 512)
    return pl.pallas_call(
        _pack_kernel,
        grid=(e // te,),
        in_specs=[pl.BlockSpec((te, D_MODEL), lambda i: (i, 0))],
        out_specs=pl.BlockSpec((te * PACK_ROWS, LANES), lambda i: (i, 0)),
        out_shape=jax.ShapeDtypeStruct((e * PACK_ROWS, LANES), jnp.uint32),
        compiler_params=_params(("arbitrary",)),
        name="pack_table",
    )(t)


def _table_spec(n_rows):
    return pl.BlockSpec((n_rows, LANES), lambda i: (0, 0), pipeline_mode=pl.Buffered(1))


def _gather_rows(idx_ref, t, tab_ref, buf):
    for k in range(PEER_SLOTS):
        e = pl.multiple_of(idx_ref[t, k], PACK_ROWS)
        buf[k * PACK_ROWS:(k + 1) * PACK_ROWS, :] = tab_ref[pl.ds(e, PACK_ROWS), :]


def _gathered_bf16(buf):
    return pltpu.bitcast(buf[...], BF16)


def _chunk_pattern():
    shape = (2 * SUBLANES, PEER_SLOTS * SUBLANES)
    row = lax.broadcasted_iota(jnp.int32, shape, 0)
    col = lax.broadcasted_iota(jnp.int32, shape, 1)
    return (row & (SUBLANES - 1)) == (col & (SUBLANES - 1))


def _token_rows(t):
    if isinstance(t, int):
        return slice(t * SUBLANES, (t + 1) * SUBLANES)
    return pl.ds(pl.multiple_of(t * SUBLANES, SUBLANES), SUBLANES)


def _load_token_chunks(ref, t):
    row = ref[pl.ds(t, 1), :]
    return jnp.concatenate([row[:, j * LANES:(j + 1) * LANES] for j in range(SUBLANES)], axis=0)


def _store_token_chunks(ref, t, val):
    ref[pl.ds(t, 1), :] = jnp.concatenate([val[j:j + 1, :] for j in range(SUBLANES)], axis=1)


def _split_bf16(x):
    hi = x.astype(BF16)
    return hi, (x - hi.astype(F32)).astype(BF16)


def _token_loop(idx_ref, tab_ref, bufs, tb, unroll, compute, side_work=None):
    _gather_rows(idx_ref, 0, tab_ref, bufs[0])

    def body(i, carry):
        if side_work is not None:
            side_work(i)
        for u in range(unroll):
            t = i * unroll + u
            _gather_rows(idx_ref, jnp.minimum(t + 1, tb - 1), tab_ref, bufs[(u + 1) % 2])
            compute(t, bufs[u % 2])
        return carry

    lax.fori_loop(0, tb // unroll, body, 0)


def _peer_dot_body(idx_ref, h_ref, g_ref, tab_ref, c_ref, buf0, buf1, part_scr, tb, unroll, side_work):
    pattern = _chunk_pattern()

    def compute(t, buf):
        h16 = jnp.concatenate(_split_bf16(h_ref[_token_rows(t), :]), axis=0)
        r = lax.dot_general(h16, _gathered_bf16(buf), NT_DIMS, preferred_element_type=F32)
        r = jnp.where(pattern, r, 0.0)
        part_scr[pl.ds(t, 1), :] = jnp.sum(r[:SUBLANES] + r[SUBLANES:], axis=0, keepdims=True)

    _token_loop(idx_ref, tab_ref, (buf0, buf1), tb, unroll, compute, side_work)

    n_part = PEER_SLOTS * SUBLANES
    slot_of = lax.broadcasted_iota(jnp.int32, (n_part, PEER_SLOTS), 0) // SUBLANES
    fold = jnp.where(slot_of == lax.broadcasted_iota(jnp.int32, (n_part, PEER_SLOTS), 1), 1.0, 0.0).astype(BF16)
    p_hi, p_lo = _split_bf16(part_scr[...])
    a = (jnp.dot(p_hi, fold, preferred_element_type=F32)
         + jnp.dot(p_lo, fold, preferred_element_type=F32))
    c_ref[...] = g_ref[...] * (0.5 * a * (1.0 + lax.erf(a * (1.0 / math.sqrt(2.0)))))


def _peer_dot_kernel(idx_ref, h_ref, g_ref, tab_ref, c_ref, buf0, buf1, part_scr, *, tb, unroll):
    _peer_dot_body(idx_ref, h_ref, g_ref, tab_ref, c_ref, buf0, buf1, part_scr, tb, unroll, None)


def _peer_dot_route_kernel(idx_ref, h_ref, g_ref, tab_ref, rh_ref, wq_ref, keys_ref,
                           c_ref, ridx_ref, rg_ref, buf0, buf1, part_scr, *route_scr, tb, unroll):
    trips = tb // unroll
    phase = pl.program_id(0) % (PEER_HEADS // trips)
    _route_init(route_scr)

    def side_work(i):
        _route_head(rh_ref, wq_ref, keys_ref, phase * trips + i, route_scr)

    _peer_dot_body(idx_ref, h_ref, g_ref, tab_ref, c_ref, buf0, buf1, part_scr, tb, unroll, side_work)

    @pl.when(phase == PEER_HEADS // trips - 1)
    def _():
        _route_emit(ridx_ref, rg_ref, route_scr[2], route_scr[3])


def _peer_dot_specs(tab, tb, h_block_off):
    in_specs = [
        pl.BlockSpec((tb, PEER_SLOTS), lambda i: (i, 0), memory_space=pltpu.SMEM),
        pl.BlockSpec((tb * SUBLANES, LANES), lambda i: (i + h_block_off, 0)),
        pl.BlockSpec((tb, PEER_SLOTS), lambda i: (i, 0)),
        _table_spec(tab.shape[0]),
    ]
    scratch = [pltpu.VMEM((PEER_SLOTS * PACK_ROWS, LANES), jnp.uint32),
               pltpu.VMEM((PEER_SLOTS * PACK_ROWS, LANES), jnp.uint32),
               pltpu.VMEM((tb, PEER_SLOTS * SUBLANES), F32)]
    return in_specs, scratch


def _peer_dot(idx, h2r, gate, tab, tb, unroll, tok_off=0):
    n = idx.shape[0]
    in_specs, scratch = _peer_dot_specs(tab, tb, tok_off // tb)
    return pl.pallas_call(
        functools.partial(_peer_dot_kernel, tb=tb, unroll=unroll),
        grid=(n // tb,),
        in_specs=in_specs,
        out_specs=pl.BlockSpec((tb, PEER_SLOTS), lambda i: (i, 0)),
        out_shape=jax.ShapeDtypeStruct((n, PEER_SLOTS), F32),
        scratch_shapes=scratch,
        compiler_params=_params(("arbitrary",)),
        name="peer_dot",
    )(idx, h2r, gate, tab)


def _peer_dot_route(idx, h2r, gate, tab, h2, wq3, keys_bf, tb, unroll, tok_off, route_off):
    n = idx.shape[0]
    assert PEER_HEADS % (tb // unroll) == 0
    steps_per_block = PEER_HEADS // (tb // unroll)
    assert steps_per_block * tb == ROUTE_TOKENS and route_off % ROUTE_TOKENS == 0
    in_specs, scratch = _peer_dot_specs(tab, tb, tok_off // tb)
    route_in_block = lambda i: (i // steps_per_block + route_off // ROUTE_TOKENS, 0)
    route_block = lambda i: (i // steps_per_block, 0)
    route_out_specs, route_out_shape = _route_out(n, route_block)
    return pl.pallas_call(
        functools.partial(_peer_dot_route_kernel, tb=tb, unroll=unroll),
        grid=(n // tb,),
        in_specs=in_specs + _route_in_specs(route_in_block),
        out_specs=[pl.BlockSpec((tb, PEER_SLOTS), lambda i: (i, 0))] + route_out_specs,
        out_shape=[jax.ShapeDtypeStruct((n, PEER_SLOTS), F32)] + route_out_shape,
        scratch_shapes=scratch + _route_scratch(),
        compiler_params=_params(("arbitrary",)),
        name="peer_dot_route",
    )(idx, h2r, gate, tab, h2, wq3, keys_bf)


def _peer_axpy_kernel(idx_ref, c_ref, x_ref, gf_ref, tab_ref, y_ref, buf0, buf1, cexp_scr, *, tb, unroll):
    pattern = _chunk_pattern()
    n_part = PEER_SLOTS * SUBLANES
    slot_of = lax.broadcasted_iota(jnp.int32, (PEER_SLOTS, n_part), 1) // SUBLANES
    spread = jnp.where(slot_of == lax.broadcasted_iota(jnp.int32, (PEER_SLOTS, n_part), 0), 1.0, 0.0).astype(BF16)
    c_hi, c_lo = _split_bf16(c_ref[...])
    cexp_scr[0:tb, :] = jnp.dot(c_hi, spread, preferred_element_type=F32)
    cexp_scr[tb:2 * tb, :] = jnp.dot(c_lo, spread, preferred_element_type=F32)

    def compute(t, buf):
        coef = jnp.concatenate([jnp.broadcast_to(cexp_scr[pl.ds(t, 1), :], (SUBLANES, n_part)),
                                jnp.broadcast_to(cexp_scr[pl.ds(tb + t, 1), :], (SUBLANES, n_part))], axis=0)
        coef = jnp.where(pattern, coef, 0.0).astype(BF16)
        o16 = jnp.dot(coef, _gathered_bf16(buf), preferred_element_type=F32)
        x2 = _load_token_chunks(x_ref, t) + (o16[:SUBLANES] + o16[SUBLANES:])
        ss = jnp.sum(jnp.sum(x2 * x2, axis=1, keepdims=True), axis=0, keepdims=True)
        _store_token_chunks(y_ref, t, x2 * lax.rsqrt(ss * (1.0 / D_MODEL) + RMS_EPS) * gf_ref[...])

    _token_loop(idx_ref, tab_ref, (buf0, buf1), tb, unroll, compute)


def _peer_axpy(idx, c, x1r, gfr, tab, tb, unroll):
    n = idx.shape[0]
    kern = functools.partial(_peer_axpy_kernel, tb=tb, unroll=unroll)
    return pl.pallas_call(
        kern,
        grid=(n // tb,),
        in_specs=[
            pl.BlockSpec((tb, PEER_SLOTS), lambda i: (i, 0), memory_space=pltpu.SMEM),
            pl.BlockSpec((tb, PEER_SLOTS), lambda i: (i, 0)),
            pl.BlockSpec((tb, D_MODEL), lambda i: (i, 0)),
            pl.BlockSpec((SUBLANES, LANES), lambda i: (0, 0)),
            _table_spec(tab.shape[0]),
        ],
        out_specs=pl.BlockSpec((tb, D_MODEL), lambda i: (i, 0)),
        out_shape=jax.ShapeDtypeStruct((n, D_MODEL), F32),
        scratch_shapes=[pltpu.VMEM((PEER_SLOTS * PACK_ROWS, LANES), jnp.uint32),
                        pltpu.VMEM((PEER_SLOTS * PACK_ROWS, LANES), jnp.uint32),
                        pltpu.VMEM((2 * tb, PEER_SLOTS * SUBLANES), F32)],
        compiler_params=_params(("arbitrary",)),
        name="peer_axpy",
    )(idx, c, x1r, gfr, tab)


def _row_tile(n, want):
    t = min(want, n)
    assert n % t == 0
    return t


def _layer(x2, b, s, norm_mix, w_in, w_sb_proj, w_ca_proj, ca_rel_bias, w_out, norm_ffn,
           peer_w_query, peer_sub_keys, peer_u, peer_v, final_gain):
    n = b * s
    col = np.arange(IN_COLS)
    is_q = (col < ATT_WIDTH) | ((col >= 3 * ATT_WIDTH) & (col < 4 * ATT_WIDTH))
    col_scale = np.where(is_q, math.log2(math.e) / math.sqrt(HEAD_DIM), 1.0)
    w_in_bf = (w_in * jnp.asarray(col_scale, F32)[None, :]).astype(BF16)

    proj = _inproj(x2, norm_mix.reshape(1, D_MODEL), w_in_bf, _row_tile(n, 512))
    proj3 = proj.reshape(b, s, IN_COLS)

    tq = 256
    assert s % tq == 0 and s >= tq + CA_PAD
    sb_tq = min(2048, s)
    assert s % sb_tq == 0
    o_sb = _sb_attention(proj3, sb_tq, 256)
    win = tq + CA_PAD
    o_ca = _ca_attention(proj3, _ca_rings(ca_rel_bias, tq, win), tq, win, 64)

    x1, h2, h2r = _merge(o_sb.reshape(n, ATT_WIDTH), o_ca.reshape(n, ATT_WIDTH), proj, x2,
                    w_sb_proj.astype(BF16), w_ca_proj.astype(BF16), w_out.astype(BF16),
                    norm_ffn.reshape(1, D_MODEL), _row_tile(n, 512))

    keys_bf = peer_sub_keys.reshape(2 * PEER_HEADS, PEER_N_KEYS, PEER_HALF).astype(BF16)
    wq3 = jnp.transpose(peer_w_query.astype(BF16).reshape(D_MODEL, 2 * PEER_HEADS, PEER_HALF), (1, 0, 2))
    tab_u = _pack_table(peer_u)
    tbg, unroll = 64, 16
    n_seg = b if (b > 1 and s % ROUTE_TOKENS == 0) else 1
    seg = n // n_seg
    assert seg % ROUTE_TOKENS == 0
    idx_j, gate_j = _route(h2, wq3, keys_bf, seg)
    idx_parts, c_parts = [], []
    for j in range(n_seg):
        idx_parts.append(idx_j)
        if j + 1 < n_seg:
            c_j, idx_j, gate_j = _peer_dot_route(idx_j, h2r, gate_j, tab_u, h2, wq3, keys_bf, tbg, unroll,
                                                 j * seg, (j + 1) * seg)
        else:
            c_j = _peer_dot(idx_j, h2r, gate_j, tab_u, tbg, unroll, j * seg)
        c_parts.append(c_j)
    idx = jnp.concatenate(idx_parts, axis=0)
    c = jnp.concatenate(c_parts, axis=0)
    return _peer_axpy(idx, c, x1, final_gain.reshape(SUBLANES, LANES), _pack_table(peer_v), tbg, unroll)


def kernel(x, norm_mix, w_in, w_sb_proj, w_ca_proj, ca_rel_bias, w_out, norm_ffn,
           peer_w_query, peer_sub_keys, peer_u, peer_v, norm_final):
    b, s, d = x.shape
    depth = norm_mix.shape[0]
    assert d == D_MODEL and depth == 1
    y = _layer(x.reshape(b * s, d), b, s, norm_mix[0], w_in[0], w_sb_proj[0], w_ca_proj[0],
               ca_rel_bias[0], w_out[0], norm_ffn[0], peer_w_query[0], peer_sub_keys[0],
               peer_u[0], peer_v[0], norm_final)
    return y.reshape(b, s, d)
```

```python
import dataclasses
import functools
import math

import numpy as np
import jax
import jax.numpy as jnp
from jax import lax
from jax.experimental import pallas as pl
from jax.experimental.pallas import tpu as pltpu
from jax.experimental.pallas import tpu_sc as plsc

D_MODEL = 1024
HEAD_DIM = 64
N_HEADS = 8
ATT_WIDTH = N_HEADS * HEAD_DIM
IN_COLS = 6 * ATT_WIDTH + 2 * D_MODEL
CHUNK = 64
CA_LEFT_CHUNKS = 8
CA_PAD = CA_LEFT_CHUNKS * CHUNK
REL_CLIP = 128
PEER_HEADS = 8
PEER_N_KEYS = 128
PEER_HALF = 128
PEER_TOPK = 16
PEER_SLOTS = PEER_HEADS * PEER_TOPK
RMS_EPS = 1e-6
NEG_INF = -1e30

LANES = 128
SUBLANES = 8
PACK_ROWS = D_MODEL // 2 // LANES

VMEM_LIMIT = 56 * 1024 * 1024

F32 = jnp.float32
BF16 = jnp.bfloat16
NT_DIMS = (((1,), (1,)), ((), ()))


def _params(sem, vmem=VMEM_LIMIT):
    return pltpu.CompilerParams(dimension_semantics=sem, vmem_limit_bytes=vmem)


def _inproj_kernel(x_ref, g_ref, w_ref, o_ref):
    x = x_ref[...]
    ms = jnp.mean(x * x, axis=-1, keepdims=True)
    h = (x * lax.rsqrt(ms + RMS_EPS) * g_ref[...]).astype(BF16)
    for c in range(IN_COLS // D_MODEL):
        cols = slice(c * D_MODEL, (c + 1) * D_MODEL)
        o_ref[:, cols] = jnp.dot(h, w_ref[:, cols], preferred_element_type=F32).astype(o_ref.dtype)


def _inproj(x2, g, w_bf, tm):
    n = x2.shape[0]
    return pl.pallas_call(
        _inproj_kernel,
        grid=(n // tm,),
        in_specs=[
            pl.BlockSpec((tm, D_MODEL), lambda i: (i, 0)),
            pl.BlockSpec((1, D_MODEL), lambda i: (0, 0)),
            pl.BlockSpec((D_MODEL, IN_COLS), lambda i: (0, 0)),
        ],
        out_specs=pl.BlockSpec((tm, IN_COLS), lambda i: (i, 0)),
        out_shape=jax.ShapeDtypeStruct((n, IN_COLS), BF16),
        compiler_params=_params(("arbitrary",)),
        name="inproj",
    )(x2, g, w_bf)


def _sb_kernel(q_ref, k_ref, v_ref, o_ref, acc_ref, r_ref, z_scr, lat_scr, *, tq, tk):
    i = pl.program_id(2)
    n_chunks = tq // tk
    lane = lax.broadcasted_iota(jnp.int32, (1, LANES), 1)
    head_mask = (lane < HEAD_DIM, lane >= HEAD_DIM)
    kk = lax.broadcasted_iota(jnp.int32, (tk, tk), 0)
    ss = lax.broadcasted_iota(jnp.int32, (tk, tk), 1)
    stri = jnp.where(kk > ss, 1.0, 0.0).astype(BF16)
    causal = ss < kk
    sign = jnp.uint32(0x80000000)

    acc_ref[...] = jnp.zeros_like(acc_ref)
    r_ref[...] = jnp.zeros_like(r_ref)

    def rows(c):
        return slice(c * tk, (c + 1) * tk)

    def both_heads(x):
        zero = jnp.zeros_like(x)
        return jnp.concatenate([jnp.where(head_mask[0], x, zero), jnp.where(head_mask[1], x, zero)], axis=0)

    def logits(j, chunks):
        k2 = both_heads(k_ref[0, pl.ds(pl.multiple_of(j * tk, tk), tk), :])
        for c in chunks:
            z_scr[c] = lax.dot_general(q_ref[0, rows(c), :], k2, NT_DIMS, preferred_element_type=F32)

    def later_sums(chunks, diag_chunk):
        for c in chunks:
            for h in range(2):
                cols = slice(h * tk, (h + 1) * tk)
                z = z_scr[c, :, cols]
                neg_abs = lax.bitcast_convert_type(lax.bitcast_convert_type(z, jnp.uint32) | sign, F32)
                sp = jnp.maximum(z, 0.0) + jnp.log2(1.0 + jnp.exp2(neg_abs))
                if c == diag_chunk:
                    sp = jnp.where(causal, sp, 0.0)
                z_scr[c, :, cols] = z - sp
                r = r_ref[h, rows(c), :]
                later = jnp.concatenate([r] * (tk // LANES), axis=1) + jnp.dot(
                    sp.astype(BF16), stri, preferred_element_type=F32)
                lat_scr[c, :, cols] = later
                r_ref[h, rows(c), :] = jnp.broadcast_to(later[:, 0:1] + sp[:, 0:1], (tk, LANES))

    def weigh(j, chunks, diag_chunk):
        v2 = both_heads(v_ref[0, pl.ds(pl.multiple_of(j * tk, tk), tk), :])
        for c in chunks:
            ws = []
            for h in range(2):
                cols = slice(h * tk, (h + 1) * tk)
                logw = z_scr[c, :, cols] - lat_scr[c, :, cols]
                if c == diag_chunk:
                    logw = jnp.where(causal, logw, NEG_INF)
                ws.append(jnp.exp2(logw).astype(BF16))
            acc_ref[rows(c), :] += jnp.dot(jnp.concatenate(ws, axis=1), v2, preferred_element_type=F32)

    all_chunks = list(range(n_chunks))
    base = i * n_chunks
    for d in range(n_chunks - 1, -1, -1):
        chunks = list(range(d, n_chunks))
        logits(base + d, chunks)
        later_sums(chunks, d)
        weigh(base + d, chunks, d)

    def body(jj, carry):
        j = base - 1 - jj
        logits(j, all_chunks)
        later_sums(all_chunks, None)
        weigh(j, all_chunks, None)
        return carry

    lax.fori_loop(0, base, body, 0)
    o_ref[0] = acc_ref[...].astype(o_ref.dtype)


def _sb_attention(proj3, tq, tk):
    b, s, _ = proj3.shape
    n_pairs = ATT_WIDTH // LANES
    kern = functools.partial(_sb_kernel, tq=tq, tk=tk)
    return pl.pallas_call(
        kern,
        grid=(b, n_pairs, s // tq),
        in_specs=[
            pl.BlockSpec((1, tq, LANES), lambda bi, p, i: (bi, i, p)),
            pl.BlockSpec((1, s, LANES), lambda bi, p, i: (bi, 0, n_pairs + p)),
            pl.BlockSpec((1, s, LANES), lambda bi, p, i: (bi, 0, 2 * n_pairs + p)),
        ],
        out_specs=pl.BlockSpec((1, tq, LANES), lambda bi, p, i: (bi, i, p)),
        out_shape=jax.ShapeDtypeStruct((b, s, ATT_WIDTH), BF16),
        scratch_shapes=[
            pltpu.VMEM((tq, LANES), F32),
            pltpu.VMEM((2, tq, LANES), F32),
            pltpu.VMEM((tq // tk, tk, 2 * tk), F32),
            pltpu.VMEM((tq // tk, tk, 2 * tk), F32),
        ],
        compiler_params=_params(("arbitrary", "arbitrary", "arbitrary")),
        name="sb_attention",
    )(proj3, proj3, proj3)


def _ca_kernel(q_ref, k_ref, v_ref, ring_ref, o_ref, z_scr, p_scr, b_scr, *, tq, win, rc):
    i = pl.program_id(2)
    ws = pl.multiple_of(jnp.maximum(i * tq - CA_PAD, 0), tq)

    @pl.when(i <= 2)
    def _():
        off = i * tq - ws
        r = lax.broadcasted_iota(jnp.int32, (tq, win), 0)
        c = lax.broadcasted_iota(jnp.int32, (tq, win), 1)
        relc = ((off + r) >> 6) - (c >> 6)
        valid = (relc >= 0) & (relc <= CA_LEFT_CHUNKS)
        for h in range(2):
            ring = jnp.broadcast_to(ring_ref[0, 0, h:h + 1, :], (tq, ring_ref.shape[-1]))
            toep = pltpu.roll(ring, 0, 1, stride=1, stride_axis=0)
            b_scr[h] = jnp.where(valid, toep[:, :win], NEG_INF)

    lane = lax.broadcasted_iota(jnp.int32, (1, LANES), 1)
    head_mask = (lane < HEAD_DIM, lane >= HEAD_DIM)
    k = k_ref[0, pl.ds(ws, win), :]
    v = v_ref[0, pl.ds(ws, win), :]
    zero = jnp.zeros_like(k)
    k2 = jnp.concatenate([jnp.where(head_mask[0], k, zero), jnp.where(head_mask[1], k, zero)], axis=0)
    z_scr[...] = lax.dot_general(q_ref[0], k2, NT_DIMS, preferred_element_type=F32)
    inv = [[None] * (tq // rc) for _ in range(2)]
    for h in range(2):
        for c in range(tq // rc):
            rows = slice(c * rc, (c + 1) * rc)
            cols = slice(h * win, (h + 1) * win)
            z = z_scr[rows, cols] + b_scr[h, rows, :]
            p = jnp.exp2(z - jnp.max(z, axis=1, keepdims=True))
            inv[h][c] = 1.0 / jnp.sum(p, axis=1, keepdims=True)
            p_scr[rows, cols] = p.astype(BF16)
    outs = [jnp.dot(p_scr[:, h * win:(h + 1) * win], jnp.where(head_mask[h], v, zero),
                    preferred_element_type=F32) for h in range(2)]
    o_ref[0] = (outs[0] * jnp.concatenate(inv[0], axis=0)
                + outs[1] * jnp.concatenate(inv[1], axis=0)).astype(o_ref.dtype)


def _ca_rings(rel_table, tq, win):
    ring_len = tq + win
    out = []
    for var in range(3):
        off = var * tq - max(var * tq - CA_PAD, 0)
        dist = np.concatenate([off - np.arange(win), [0], off + np.arange(tq - 1, 0, -1)])
        out.append(rel_table.astype(F32)[:, np.clip(dist, -REL_CLIP, REL_CLIP) + REL_CLIP])
    rings = jnp.stack(out) * math.log2(math.e)
    return rings.reshape(3, rel_table.shape[0] // 2, 2, ring_len)


def _ca_attention(proj3, rings, tq, win, rc):
    b, s, _ = proj3.shape
    n_pairs = ATT_WIDTH // LANES
    assert CHUNK == 64 and (tq + win) % LANES == 0
    kern = functools.partial(_ca_kernel, tq=tq, win=win, rc=rc)
    return pl.pallas_call(
        kern,
        grid=(b, n_pairs, s // tq),
        in_specs=[
            pl.BlockSpec((1, tq, LANES), lambda bi, p, i: (bi, i, 3 * n_pairs + p)),
            pl.BlockSpec((1, s, LANES), lambda bi, p, i: (bi, 0, 4 * n_pairs + p)),
            pl.BlockSpec((1, s, LANES), lambda bi, p, i: (bi, 0, 5 * n_pairs + p)),
            pl.BlockSpec((1, 1, 2, tq + win), lambda bi, p, i: (jnp.minimum(i, 2), p, 0, 0)),
        ],
        out_specs=pl.BlockSpec((1, tq, LANES), lambda bi, p, i: (bi, i, p)),
        out_shape=jax.ShapeDtypeStruct((b, s, ATT_WIDTH), BF16),
        scratch_shapes=[pltpu.VMEM((tq, 2 * win), F32), pltpu.VMEM((tq, 2 * win), BF16),
                        pltpu.VMEM((2, tq, win), F32)],
        compiler_params=_params(("arbitrary", "arbitrary", "arbitrary")),
        name="ca_attention",
    )(proj3, proj3, proj3, rings)


def _merge_kernel(osb_ref, oca_ref, gsb_ref, gca_ref, x_ref, psb_ref, pca_ref, wout_ref, nf_ref,
                  x1_ref, h2_ref, h2c_ref):
    br_sb = jnp.dot(osb_ref[...], psb_ref[...], preferred_element_type=F32)
    br_ca = jnp.dot(oca_ref[...], pca_ref[...], preferred_element_type=F32)
    merged = (jax.nn.sigmoid(gsb_ref[...].astype(F32)) * br_sb
              + jax.nn.sigmoid(gca_ref[...].astype(F32)) * br_ca)
    x1 = x_ref[...] + jnp.dot(merged.astype(BF16), wout_ref[...], preferred_element_type=F32)
    x1_ref[...] = x1
    ms = jnp.mean(x1 * x1, axis=-1, keepdims=True)
    h2 = x1 * lax.rsqrt(ms + RMS_EPS) * nf_ref[...]
    h2_ref[...] = h2
    tm = h2.shape[0]
    for j in range(SUBLANES):
        h2c_ref[pl.ds(j, tm, stride=SUBLANES), :] = h2[:, j * LANES:(j + 1) * LANES]


def _merge(o_sb, o_ca, proj, x2, psb, pca, wout, nf, tm):
    n = x2.shape[0]
    gate_blk = 3 * ATT_WIDTH * 2 // D_MODEL
    return pl.pallas_call(
        _merge_kernel,
        grid=(n // tm,),
        in_specs=[
            pl.BlockSpec((tm, ATT_WIDTH), lambda i: (i, 0)),
            pl.BlockSpec((tm, ATT_WIDTH), lambda i: (i, 0)),
            pl.BlockSpec((tm, D_MODEL), lambda i: (i, gate_blk)),
            pl.BlockSpec((tm, D_MODEL), lambda i: (i, gate_blk + 1)),
            pl.BlockSpec((tm, D_MODEL), lambda i: (i, 0)),
            pl.BlockSpec((ATT_WIDTH, D_MODEL), lambda i: (0, 0)),
            pl.BlockSpec((ATT_WIDTH, D_MODEL), lambda i: (0, 0)),
            pl.BlockSpec((D_MODEL, D_MODEL), lambda i: (0, 0)),
            pl.BlockSpec((1, D_MODEL), lambda i: (0, 0)),
        ],
        out_specs=[
            pl.BlockSpec((tm, D_MODEL), lambda i: (i, 0)),
            pl.BlockSpec((tm, D_MODEL), lambda i: (i, 0)),
            pl.BlockSpec((tm * SUBLANES, LANES), lambda i: (i, 0)),
        ],
        out_shape=[jax.ShapeDtypeStruct((n, D_MODEL), F32), jax.ShapeDtypeStruct((n, D_MODEL), F32),
                   jax.ShapeDtypeStruct((n * SUBLANES, LANES), F32)],
        compiler_params=_params(("arbitrary",)),
        name="merge",
    )(o_sb, o_ca, proj, proj, x2, psb, pca, wout, nf)


def _peer_candidates():
    return [(a, b) for a in range(PEER_TOPK) for b in range(PEER_TOPK) if (a + 1) * (b + 1) <= PEER_TOPK]


def _extract_top(s_scr, iota_scr, n_rounds, sentinel):
    vals, ids = [], []
    for _ in range(n_rounds):
        s = s_scr[...]
        m = jnp.max(s, axis=0, keepdims=True)
        cand = jnp.where(s == m, iota_scr[...], sentinel)
        am = jnp.min(cand, axis=0, keepdims=True)
        vals.append(m)
        ids.append(am)
        s_scr[...] = jnp.where(cand == am, -jnp.inf, s)
    return vals, ids


ROUTE_TOKENS = 128


def _route_scratch():
    n_rows = -(-len(_peer_candidates()) // SUBLANES) * SUBLANES
    return [
        pltpu.VMEM((n_rows, ROUTE_TOKENS), F32),
        pltpu.VMEM((n_rows, ROUTE_TOKENS), F32),
        pltpu.VMEM((PEER_SLOTS, ROUTE_TOKENS), F32),
        pltpu.VMEM((PEER_SLOTS, ROUTE_TOKENS), F32),
        pltpu.VMEM((2, PEER_N_KEYS, ROUTE_TOKENS), F32),
        pltpu.VMEM((PEER_N_KEYS, ROUTE_TOKENS), F32),
    ]


def _route_init(scr):
    cv_scr, ci_scr, _, _, _, iota_scr = scr
    n_cand = len(_peer_candidates())
    n_rows = cv_scr.shape[0]

    @pl.when(pl.program_id(0) == 0)
    def _():
        cv_scr[n_cand:, :] = jnp.full((n_rows - n_cand, ROUTE_TOKENS), -jnp.inf, F32)
        ci_scr[n_cand:, :] = jnp.zeros((n_rows - n_cand, ROUTE_TOKENS), F32)
        iota_scr[...] = lax.broadcasted_iota(jnp.int32, iota_scr.shape, 0).astype(F32)


def _slot_row(hd, r):
    if isinstance(hd, int):
        return slice(hd * PEER_TOPK + r, hd * PEER_TOPK + r + 1)
    return pl.ds(hd * PEER_TOPK + r, 1)


def _route_head(h_ref, wq_ref, keys_ref, hd, scr):
    cv_scr, ci_scr, ri_scr, rg_scr, s_scr, iota_scr = scr
    cands = _peer_candidates()
    n_rows = cv_scr.shape[0]
    h = h_ref[...].astype(BF16)
    iota_c = lax.broadcasted_iota(jnp.int32, (n_rows, ROUTE_TOKENS), 0).astype(F32)
    for p in range(2):
        hp = hd * 2 + p
        q = jnp.dot(h, wq_ref[hp], preferred_element_type=F32).astype(BF16)
        s_scr[p] = lax.dot_general(keys_ref[hp], q, NT_DIMS, preferred_element_type=F32)
    tops = [_extract_top(s_scr.at[p], iota_scr, PEER_TOPK, float(PEER_N_KEYS)) for p in range(2)]
    (v1, i1), (v2, i2) = tops
    i1s = [x * float(PEER_N_KEYS * PACK_ROWS) for x in i1]
    i2s = [x * float(PACK_ROWS) for x in i2]
    for r, (a, b) in enumerate(cands):
        cv_scr[r:r + 1, :] = v1[a] + v2[b]
        ci_scr[r:r + 1, :] = i1s[a] + i2s[b]
    cv = cv_scr[...]
    ci = ci_scr[...]
    best = []
    for r in range(PEER_TOPK):
        m = jnp.max(cv, axis=0, keepdims=True)
        cand = jnp.where(cv == m, iota_c, float(n_rows))
        hit = cand == jnp.min(cand, axis=0, keepdims=True)
        e = jnp.sum(jnp.where(hit, ci, 0.0), axis=0, keepdims=True)
        cv = jnp.where(hit, -jnp.inf, cv)
        best.append(m)
        ri_scr[_slot_row(hd, r), :] = e
    ex = [jnp.exp(bv - best[0]) for bv in best]
    tot = ex[0]
    for t in ex[1:]:
        tot = tot + t
    inv = 1.0 / tot
    for r in range(PEER_TOPK):
        rg_scr[_slot_row(hd, r), :] = ex[r] * inv


def _route_emit(idx_ref, g_ref, ri_scr, rg_scr):
    idx_ref[...] = ri_scr[...].T.astype(jnp.int32)
    g_ref[...] = rg_scr[...].T


def _route_kernel(h_ref, wq_ref, keys_ref, idx_ref, g_ref, *scr):
    _route_init(scr)
    for hd in range(PEER_HEADS):
        _route_head(h_ref, wq_ref, keys_ref, hd, scr)
    _route_emit(idx_ref, g_ref, scr[2], scr[3])


def _route_in_specs(block_index):
    return [
        pl.BlockSpec((ROUTE_TOKENS, D_MODEL), block_index),
        pl.BlockSpec((2 * PEER_HEADS, D_MODEL, PEER_HALF), lambda i: (0, 0, 0)),
        pl.BlockSpec((2 * PEER_HEADS, PEER_N_KEYS, PEER_HALF), lambda i: (0, 0, 0)),
    ]


def _route_out(n, block_index):
    specs = [pl.BlockSpec((ROUTE_TOKENS, PEER_SLOTS), block_index),
             pl.BlockSpec((ROUTE_TOKENS, PEER_SLOTS), block_index)]
    shapes = [jax.ShapeDtypeStruct((n, PEER_SLOTS), jnp.int32), jax.ShapeDtypeStruct((n, PEER_SLOTS), F32)]
    return specs, shapes


def _route(h2, wq3, keys_bf, n, tok_off=0):
    out_specs, out_shape = _route_out(n, lambda i: (i, 0))
    return pl.pallas_call(
        _route_kernel,
        grid=(n // ROUTE_TOKENS,),
        in_specs=_route_in_specs(lambda i: (i + tok_off // ROUTE_TOKENS, 0)),
        out_specs=out_specs,
        out_shape=out_shape,
        scratch_shapes=_route_scratch(),
        compiler_params=_params(("arbitrary",)),
        name="peer_route",
    )(h2, wq3, keys_bf)


def _pack_kernel(t_ref, o_ref, *row_major_ref):
    te = t_ref.shape[0]
    for r in range(PACK_ROWS):
        lo = t_ref[:, (2 * r) * LANES:(2 * r + 1) * LANES].astype(BF16).astype(F32)
        hi = t_ref[:, (2 * r + 1) * LANES:(2 * r + 2) * LANES].astype(BF16).astype(F32)
        word = lax.bitcast_convert_type(hi, jnp.uint32) | (lax.bitcast_convert_type(lo, jnp.uint32) >> 16)
        o_ref[pl.ds(r, te, stride=PACK_ROWS), :] = word
        for ref in row_major_ref:
            ref[:, r * LANES:(r + 1) * LANES] = word


def _pack_table(t, with_row_major=False):
    e = t.shape[0]
    te = _row_tile(e, 512)
    out_specs = [pl.BlockSpec((te * PACK_ROWS, LANES), lambda i: (i, 0))]
    out_shape = [jax.ShapeDtypeStruct((e * PACK_ROWS, LANES), jnp.uint32)]
    if with_row_major:
        out_specs.append(pl.BlockSpec((te, PACK_ROWS * LANES), lambda i: (i, 0)))
        out_shape.append(jax.ShapeDtypeStruct((e, PACK_ROWS * LANES), jnp.uint32))
    out = pl.pallas_call(
        _pack_kernel,
        grid=(e // te,),
        in_specs=[pl.BlockSpec((te, D_MODEL), lambda i: (i, 0))],
        out_specs=out_specs,
        out_shape=out_shape,
        compiler_params=_params(("arbitrary",)),
        name="pack_table",
    )(t)
    return out if with_row_major else out[0]


def _table_spec(n_rows):
    return pl.BlockSpec((n_rows, LANES), lambda i: (0, 0), pipeline_mode=pl.Buffered(1))


def _gather_rows(idx_ref, t, tab_ref, buf):
    for k in range(PEER_SLOTS):
        e = pl.multiple_of(idx_ref[t, k], PACK_ROWS)
        buf[k * PACK_ROWS:(k + 1) * PACK_ROWS, :] = tab_ref[pl.ds(e, PACK_ROWS), :]


def _gathered_bf16(buf):
    return pltpu.bitcast(buf[...], BF16)


def _chunk_pattern():
    shape = (2 * SUBLANES, PEER_SLOTS * SUBLANES)
    row = lax.broadcasted_iota(jnp.int32, shape, 0)
    col = lax.broadcasted_iota(jnp.int32, shape, 1)
    return (row & (SUBLANES - 1)) == (col & (SUBLANES - 1))


def _token_rows(t):
    if isinstance(t, int):
        return slice(t * SUBLANES, (t + 1) * SUBLANES)
    return pl.ds(pl.multiple_of(t * SUBLANES, SUBLANES), SUBLANES)


def _load_token_chunks(ref, t):
    row = ref[pl.ds(t, 1), :]
    return jnp.concatenate([row[:, j * LANES:(j + 1) * LANES] for j in range(SUBLANES)], axis=0)


def _store_token_chunks(ref, t, val):
    ref[pl.ds(t, 1), :] = jnp.concatenate([val[j:j + 1, :] for j in range(SUBLANES)], axis=1)


def _split_bf16(x):
    hi = x.astype(BF16)
    return hi, (x - hi.astype(F32)).astype(BF16)


def _token_loop(idx_ref, tab_ref, bufs, tb, unroll, compute, side_work=None):
    _gather_rows(idx_ref, 0, tab_ref, bufs[0])

    def body(i, carry):
        if side_work is not None:
            side_work(i)
        for u in range(unroll):
            t = i * unroll + u
            _gather_rows(idx_ref, jnp.minimum(t + 1, tb - 1), tab_ref, bufs[(u + 1) % 2])
            compute(t, bufs[u % 2])
        return carry

    lax.fori_loop(0, tb // unroll, body, 0)


def _peer_dot_body(idx_ref, h_ref, g_ref, tab_ref, c_ref, buf0, buf1, part_scr, tb, unroll, side_work):
    pattern = _chunk_pattern()

    def compute(t, buf):
        h16 = jnp.concatenate(_split_bf16(h_ref[_token_rows(t), :]), axis=0)
        r = lax.dot_general(h16, _gathered_bf16(buf), NT_DIMS, preferred_element_type=F32)
        r = jnp.where(pattern, r, 0.0)
        part_scr[pl.ds(t, 1), :] = jnp.sum(r[:SUBLANES] + r[SUBLANES:], axis=0, keepdims=True)

    _token_loop(idx_ref, tab_ref, (buf0, buf1), tb, unroll, compute, side_work)

    n_part = PEER_SLOTS * SUBLANES
    slot_of = lax.broadcasted_iota(jnp.int32, (n_part, PEER_SLOTS), 0) // SUBLANES
    fold = jnp.where(slot_of == lax.broadcasted_iota(jnp.int32, (n_part, PEER_SLOTS), 1), 1.0, 0.0).astype(BF16)
    p_hi, p_lo = _split_bf16(part_scr[...])
    a = (jnp.dot(p_hi, fold, preferred_element_type=F32)
         + jnp.dot(p_lo, fold, preferred_element_type=F32))
    c_ref[...] = g_ref[...] * (0.5 * a * (1.0 + lax.erf(a * (1.0 / math.sqrt(2.0)))))


def _peer_dot_kernel(idx_ref, h_ref, g_ref, tab_ref, c_ref, buf0, buf1, part_scr, *, tb, unroll):
    _peer_dot_body(idx_ref, h_ref, g_ref, tab_ref, c_ref, buf0, buf1, part_scr, tb, unroll, None)


def _peer_dot_route_kernel(idx_ref, h_ref, g_ref, tab_ref, rh_ref, wq_ref, keys_ref,
                           c_ref, ridx_ref, rg_ref, buf0, buf1, part_scr, *route_scr, tb, unroll):
    trips = tb // unroll
    phase = pl.program_id(0) % (PEER_HEADS // trips)
    _route_init(route_scr)

    def side_work(i):
        _route_head(rh_ref, wq_ref, keys_ref, phase * trips + i, route_scr)

    _peer_dot_body(idx_ref, h_ref, g_ref, tab_ref, c_ref, buf0, buf1, part_scr, tb, unroll, side_work)

    @pl.when(phase == PEER_HEADS // trips - 1)
    def _():
        _route_emit(ridx_ref, rg_ref, route_scr[2], route_scr[3])


def _peer_dot_specs(tab, tb, h_block_off):
    in_specs = [
        pl.BlockSpec((tb, PEER_SLOTS), lambda i: (i, 0), memory_space=pltpu.SMEM),
        pl.BlockSpec((tb * SUBLANES, LANES), lambda i: (i + h_block_off, 0)),
        pl.BlockSpec((tb, PEER_SLOTS), lambda i: (i, 0)),
        _table_spec(tab.shape[0]),
    ]
    scratch = [pltpu.VMEM((PEER_SLOTS * PACK_ROWS, LANES), jnp.uint32),
               pltpu.VMEM((PEER_SLOTS * PACK_ROWS, LANES), jnp.uint32),
               pltpu.VMEM((tb, PEER_SLOTS * SUBLANES), F32)]
    return in_specs, scratch


def _peer_dot(idx, h2r, gate, tab, tb, unroll, tok_off=0):
    n = idx.shape[0]
    in_specs, scratch = _peer_dot_specs(tab, tb, tok_off // tb)
    return pl.pallas_call(
        functools.partial(_peer_dot_kernel, tb=tb, unroll=unroll),
        grid=(n // tb,),
        in_specs=in_specs,
        out_specs=pl.BlockSpec((tb, PEER_SLOTS), lambda i: (i, 0)),
        out_shape=jax.ShapeDtypeStruct((n, PEER_SLOTS), F32),
        scratch_shapes=scratch,
        compiler_params=_params(("arbitrary",)),
        name="peer_dot",
    )(idx, h2r, gate, tab)


def _peer_dot_route(idx, h2r, gate, tab, h2, wq3, keys_bf, tb, unroll, tok_off, route_off):
    n = idx.shape[0]
    assert PEER_HEADS % (tb // unroll) == 0
    steps_per_block = PEER_HEADS // (tb // unroll)
    assert steps_per_block * tb == ROUTE_TOKENS and route_off % ROUTE_TOKENS == 0
    in_specs, scratch = _peer_dot_specs(tab, tb, tok_off // tb)
    route_in_block = lambda i: (i // steps_per_block + route_off // ROUTE_TOKENS, 0)
    route_block = lambda i: (i // steps_per_block, 0)
    route_out_specs, route_out_shape = _route_out(n, route_block)
    return pl.pallas_call(
        functools.partial(_peer_dot_route_kernel, tb=tb, unroll=unroll),
        grid=(n // tb,),
        in_specs=in_specs + _route_in_specs(route_in_block),
        out_specs=[pl.BlockSpec((tb, PEER_SLOTS), lambda i: (i, 0))] + route_out_specs,
        out_shape=[jax.ShapeDtypeStruct((n, PEER_SLOTS), F32)] + route_out_shape,
        scratch_shapes=scratch + _route_scratch(),
        compiler_params=_params(("arbitrary",)),
        name="peer_dot_route",
    )(idx, h2r, gate, tab, h2, wq3, keys_bf)


def _peer_axpy_kernel(idx_ref, c_ref, x_ref, gf_ref, tab_ref, y_ref, buf0, buf1, cexp_scr, *, tb, unroll):
    pattern = _chunk_pattern()
    n_part = PEER_SLOTS * SUBLANES
    slot_of = lax.broadcasted_iota(jnp.int32, (PEER_SLOTS, n_part), 1) // SUBLANES
    spread = jnp.where(slot_of == lax.broadcasted_iota(jnp.int32, (PEER_SLOTS, n_part), 0), 1.0, 0.0).astype(BF16)
    c_hi, c_lo = _split_bf16(c_ref[...])
    cexp_scr[0:tb, :] = jnp.dot(c_hi, spread, preferred_element_type=F32)
    cexp_scr[tb:2 * tb, :] = jnp.dot(c_lo, spread, preferred_element_type=F32)

    def compute(t, buf):
        coef = jnp.concatenate([jnp.broadcast_to(cexp_scr[pl.ds(t, 1), :], (SUBLANES, n_part)),
                                jnp.broadcast_to(cexp_scr[pl.ds(tb + t, 1), :], (SUBLANES, n_part))], axis=0)
        coef = jnp.where(pattern, coef, 0.0).astype(BF16)
        o16 = jnp.dot(coef, _gathered_bf16(buf), preferred_element_type=F32)
        x2 = _load_token_chunks(x_ref, t) + (o16[:SUBLANES] + o16[SUBLANES:])
        ss = jnp.sum(jnp.sum(x2 * x2, axis=1, keepdims=True), axis=0, keepdims=True)
        _store_token_chunks(y_ref, t, x2 * lax.rsqrt(ss * (1.0 / D_MODEL) + RMS_EPS) * gf_ref[...])

    _token_loop(idx_ref, tab_ref, (buf0, buf1), tb, unroll, compute)


def _peer_axpy(idx, c, x1r, gfr, tab, tb, unroll, n_tokens):
    n = idx.shape[0]
    kern = functools.partial(_peer_axpy_kernel, tb=tb, unroll=unroll)
    return pl.pallas_call(
        kern,
        grid=(n_tokens // tb,),
        in_specs=[
            pl.BlockSpec((tb, PEER_SLOTS), lambda i: (i, 0), memory_space=pltpu.SMEM),
            pl.BlockSpec((tb, PEER_SLOTS), lambda i: (i, 0)),
            pl.BlockSpec((tb, D_MODEL), lambda i: (i, 0)),
            pl.BlockSpec((SUBLANES, LANES), lambda i: (0, 0)),
            _table_spec(tab.shape[0]),
        ],
        out_specs=pl.BlockSpec((tb, D_MODEL), lambda i: (i, 0)),
        out_shape=jax.ShapeDtypeStruct((n, D_MODEL), F32),
        scratch_shapes=[pltpu.VMEM((PEER_SLOTS * PACK_ROWS, LANES), jnp.uint32),
                        pltpu.VMEM((PEER_SLOTS * PACK_ROWS, LANES), jnp.uint32),
                        pltpu.VMEM((2 * tb, PEER_SLOTS * SUBLANES), F32)],
        compiler_params=_params(("arbitrary",)),
        name="peer_axpy",
    )(idx, c, x1r, gfr, tab)


SC_WORKERS = 32
SC_LANES = 16
SC_CHUNK_TOKENS = 16
SC_HALF_ROWS = PEER_SLOTS // 2
SC_ROW_BLOCK = 8
SC_COL_GROUPS = LANES // SC_LANES
SC_SHARE_NUM, SC_SHARE_DEN = 15, 32


def _peer_axpy_sc(expert, c, tab_rows):
    n = expert.shape[0]
    per = n // SC_WORKERS
    assert n % (SC_WORKERS * SC_CHUNK_TOKENS) == 0
    mesh = plsc.VectorSubcoreMesh(core_axis_name="c", subcore_axis_name="s")
    params = pltpu.CompilerParams()
    if "needs_layout_passes" in pltpu.CompilerParams.__dataclass_fields__:
        params = dataclasses.replace(params, needs_layout_passes=False)
    expert2 = expert.reshape(n * 2, SC_HALF_ROWS)
    words = PACK_ROWS * LANES

    @functools.partial(
        pl.kernel, mesh=mesh, compiler_params=params,
        out_type=jax.ShapeDtypeStruct((n, D_MODEL), F32),
        scratch_types=[pltpu.VMEM((2 * SC_CHUNK_TOKENS, SC_HALF_ROWS), jnp.int32),
                       pltpu.VMEM((SC_CHUNK_TOKENS, PEER_SLOTS), F32),
                       pltpu.VMEM((SC_HALF_ROWS, words), jnp.uint32),
                       pltpu.VMEM((SC_HALF_ROWS, words), jnp.uint32),
                       pltpu.VMEM((D_MODEL,), F32),
                       pltpu.SemaphoreType.DMA, pltpu.SemaphoreType.DMA],
    )
    def sc_kernel(idx_hbm, c_hbm, tab_hbm, out_hbm, idx_v, c_v, rows0, rows1, out_v, sem0, sem1):
        wid = lax.axis_index("s") * 2 + lax.axis_index("c")
        bufs = ((rows0, sem0), (rows1, sem1))

        def gather(u, b):
            return pltpu.make_async_copy(tab_hbm.at[idx_v.at[u]], bufs[b][0], bufs[b][1])

        def accumulate(tl, half, b):
            rows = bufs[b][0]

            @pl.loop(0, PACK_ROWS)
            def _(cb):
                @pl.loop(0, SC_HALF_ROWS // SC_ROW_BLOCK)
                def _(rb):
                    acc = [None] * (2 * SC_COL_GROUPS)
                    for r in range(SC_ROW_BLOCK):
                        kk = rb * SC_ROW_BLOCK + r
                        zero = jnp.zeros((SC_LANES,), jnp.int32)
                        ck = plsc.load_gather(c_v, [zero + tl, zero + (half * SC_HALF_ROWS + kk)])
                        for g in range(SC_COL_GROUPS):
                            w = rows[kk, pl.ds(cb * LANES + g * SC_LANES, SC_LANES)]
                            hi = ck * lax.bitcast_convert_type(w & jnp.uint32(0xFFFF0000), F32)
                            lo = ck * lax.bitcast_convert_type(w << 16, F32)
                            acc[2 * g] = hi if acc[2 * g] is None else acc[2 * g] + hi
                            acc[2 * g + 1] = lo if acc[2 * g + 1] is None else acc[2 * g + 1] + lo
                    for g in range(SC_COL_GROUPS):
                        plsc.addupdate(out_v.at[pl.ds((2 * cb + 1) * LANES + g * SC_LANES, SC_LANES)], acc[2 * g])
                        plsc.addupdate(out_v.at[pl.ds(2 * cb * LANES + g * SC_LANES, SC_LANES)], acc[2 * g + 1])

        @pl.loop(0, per // SC_CHUNK_TOKENS)
        def _(ch):
            t0 = wid * per + ch * SC_CHUNK_TOKENS
            pltpu.sync_copy(idx_hbm.at[pl.ds(2 * t0, 2 * SC_CHUNK_TOKENS)], idx_v)
            pltpu.sync_copy(c_hbm.at[pl.ds(t0, SC_CHUNK_TOKENS)], c_v)
            gather(0, 0).start()

            @pl.loop(0, SC_CHUNK_TOKENS)
            def _(tl):
                for g in range(D_MODEL // SC_LANES):
                    out_v[pl.ds(g * SC_LANES, SC_LANES)] = jnp.zeros((SC_LANES,), F32)
                gather(2 * tl + 1, 1).start()
                gather(2 * tl, 0).wait()
                accumulate(tl, 0, 0)

                @pl.when(tl + 1 < SC_CHUNK_TOKENS)
                def _():
                    gather(2 * tl + 2, 0).start()

                gather(2 * tl + 1, 1).wait()
                accumulate(tl, 1, 1)
                pltpu.sync_copy(out_v, out_hbm.at[t0 + tl])

    return sc_kernel(expert2, c, tab_rows)


def _finish_kernel(y_hbm, x_ref, p_ref, gf_ref, o_ref):
    del y_hbm
    x2 = x_ref[...] + p_ref[...]
    ms = jnp.mean(x2 * x2, axis=-1, keepdims=True)
    o_ref[...] = x2 * lax.rsqrt(ms + RMS_EPS) * gf_ref[...]


def _peer_finish(y, x1, p_tail, gf, tm):
    n = y.shape[0]
    n_tail = p_tail.shape[0]
    off = (n - n_tail) // tm
    assert (n - n_tail) % tm == 0 and n_tail % tm == 0
    return pl.pallas_call(
        _finish_kernel,
        grid=(n_tail // tm,),
        in_specs=[
            pl.BlockSpec(memory_space=pl.ANY),
            pl.BlockSpec((tm, D_MODEL), lambda i: (i + off, 0)),
            pl.BlockSpec((tm, D_MODEL), lambda i: (i, 0)),
            pl.BlockSpec((1, D_MODEL), lambda i: (0, 0)),
        ],
        out_specs=pl.BlockSpec((tm, D_MODEL), lambda i: (i + off, 0)),
        out_shape=jax.ShapeDtypeStruct((n, D_MODEL), F32),
        input_output_aliases={0: 0},
        compiler_params=_params(("arbitrary",)),
        name="peer_finish",
    )(y, x1, p_tail, gf)


def _row_tile(n, want):
    t = min(want, n)
    assert n % t == 0
    return t


def _layer(x2, b, s, norm_mix, w_in, w_sb_proj, w_ca_proj, ca_rel_bias, w_out, norm_ffn,
           peer_w_query, peer_sub_keys, peer_u, peer_v, final_gain):
    n = b * s
    col = np.arange(IN_COLS)
    is_q = (col < ATT_WIDTH) | ((col >= 3 * ATT_WIDTH) & (col < 4 * ATT_WIDTH))
    col_scale = np.where(is_q, math.log2(math.e) / math.sqrt(HEAD_DIM), 1.0)
    w_in_bf = (w_in * jnp.asarray(col_scale, F32)[None, :]).astype(BF16)

    proj = _inproj(x2, norm_mix.reshape(1, D_MODEL), w_in_bf, _row_tile(n, 512))
    proj3 = proj.reshape(b, s, IN_COLS)

    tq = 256
    assert s % tq == 0 and s >= tq + CA_PAD
    sb_tq = min(2048, s)
    assert s % sb_tq == 0
    o_sb = _sb_attention(proj3, sb_tq, 256)
    win = tq + CA_PAD
    o_ca = _ca_attention(proj3, _ca_rings(ca_rel_bias, tq, win), tq, win, 64)

    x1, h2, h2r = _merge(o_sb.reshape(n, ATT_WIDTH), o_ca.reshape(n, ATT_WIDTH), proj, x2,
                    w_sb_proj.astype(BF16), w_ca_proj.astype(BF16), w_out.astype(BF16),
                    norm_ffn.reshape(1, D_MODEL), _row_tile(n, 512))

    keys_bf = peer_sub_keys.reshape(2 * PEER_HEADS, PEER_N_KEYS, PEER_HALF).astype(BF16)
    wq3 = jnp.transpose(peer_w_query.astype(BF16).reshape(D_MODEL, 2 * PEER_HEADS, PEER_HALF), (1, 0, 2))
    tab_u = _pack_table(peer_u)
    tbg, unroll = 64, 16
    n_seg = b if (b > 1 and s % ROUTE_TOKENS == 0) else 1
    seg = n // n_seg
    assert seg % ROUTE_TOKENS == 0
    idx_j, gate_j = _route(h2, wq3, keys_bf, seg)
    idx_parts, c_parts = [], []
    for j in range(n_seg):
        idx_parts.append(idx_j)
        if j + 1 < n_seg:
            c_j, idx_j, gate_j = _peer_dot_route(idx_j, h2r, gate_j, tab_u, h2, wq3, keys_bf, tbg, unroll,
                                                 j * seg, (j + 1) * seg)
        else:
            c_j = _peer_dot(idx_j, h2r, gate_j, tab_u, tbg, unroll, j * seg)
        c_parts.append(c_j)
    idx = jnp.concatenate(idx_parts, axis=0)
    c = jnp.concatenate(c_parts, axis=0)
    tab_v, tab_v_rows = _pack_table(peer_v, with_row_major=True)
    sc_unit = SC_WORKERS * SC_CHUNK_TOKENS
    n_sc = (n * SC_SHARE_NUM // SC_SHARE_DEN) // sc_unit * sc_unit
    n_tc = n - n_sc
    y = _peer_axpy(idx, c, x1, final_gain.reshape(SUBLANES, LANES), tab_v, tbg, unroll, n_tc)
    if n_sc:
        p_sc = _peer_axpy_sc(idx[n_tc:] // PACK_ROWS, c[n_tc:], tab_v_rows)
        y = _peer_finish(y, x1, p_sc, final_gain.reshape(1, D_MODEL), sc_unit)
    return y


def kernel(x, norm_mix, w_in, w_sb_proj, w_ca_proj, ca_rel_bias, w_out, norm_ffn,
           peer_w_query, peer_sub_keys, peer_u, peer_v, norm_final):
    b, s, d = x.shape
    depth = norm_mix.shape[0]
    assert d == D_MODEL and depth == 1
    y = _layer(x.reshape(b * s, d), b, s, norm_mix[0], w_in[0], w_sb_proj[0], w_ca_proj[0],
               ca_rel_bias[0], w_out[0], norm_ffn[0], peer_w_query[0], peer_sub_keys[0],
               peer_u[0], peer_v[0], norm_final)
    return y.reshape(b, s, d)
```

```python
import dataclasses
import functools
import math

import numpy as np
import jax
import jax.numpy as jnp
from jax import lax
from jax.experimental import pallas as pl
from jax.experimental.pallas import tpu as pltpu
from jax.experimental.pallas import tpu_sc as plsc

D_MODEL = 1024
HEAD_DIM = 64
N_HEADS = 8
ATT_WIDTH = N_HEADS * HEAD_DIM
IN_COLS = 6 * ATT_WIDTH + 2 * D_MODEL
CHUNK = 64
CA_LEFT_CHUNKS = 8
CA_PAD = CA_LEFT_CHUNKS * CHUNK
REL_CLIP = 128
PEER_HEADS = 8
PEER_N_KEYS = 128
PEER_HALF = 128
PEER_TOPK = 16
PEER_SLOTS = PEER_HEADS * PEER_TOPK
RMS_EPS = 1e-6
NEG_INF = -1e30

LANES = 128
SUBLANES = 8
PACK_ROWS = D_MODEL // 2 // LANES

VMEM_LIMIT = 56 * 1024 * 1024

F32 = jnp.float32
BF16 = jnp.bfloat16
NT_DIMS = (((1,), (1,)), ((), ()))


def _params(sem, vmem=VMEM_LIMIT):
    return pltpu.CompilerParams(dimension_semantics=sem, vmem_limit_bytes=vmem)


def _inproj_kernel(x_ref, g_ref, w_ref, o_ref):
    x = x_ref[...]
    ms = jnp.mean(x * x, axis=-1, keepdims=True)
    h = (x * lax.rsqrt(ms + RMS_EPS) * g_ref[...]).astype(BF16)
    for c in range(IN_COLS // D_MODEL):
        cols = slice(c * D_MODEL, (c + 1) * D_MODEL)
        o_ref[:, cols] = jnp.dot(h, w_ref[:, cols], preferred_element_type=F32).astype(o_ref.dtype)


def _inproj(x2, g, w_bf, tm):
    n = x2.shape[0]
    return pl.pallas_call(
        _inproj_kernel,
        grid=(n // tm,),
        in_specs=[
            pl.BlockSpec((tm, D_MODEL), lambda i: (i, 0)),
            pl.BlockSpec((1, D_MODEL), lambda i: (0, 0)),
            pl.BlockSpec((D_MODEL, IN_COLS), lambda i: (0, 0)),
        ],
        out_specs=pl.BlockSpec((tm, IN_COLS), lambda i: (i, 0)),
        out_shape=jax.ShapeDtypeStruct((n, IN_COLS), BF16),
        compiler_params=_params(("arbitrary",)),
        name="inproj",
    )(x2, g, w_bf)


def _sb_kernel(q_ref, k_ref, v_ref, o_ref, acc_ref, r_ref, z_scr, lat_scr, *, tq, tk):
    i = pl.program_id(2)
    n_chunks = tq // tk
    lane = lax.broadcasted_iota(jnp.int32, (1, LANES), 1)
    head_mask = (lane < HEAD_DIM, lane >= HEAD_DIM)
    kk = lax.broadcasted_iota(jnp.int32, (tk, tk), 0)
    ss = lax.broadcasted_iota(jnp.int32, (tk, tk), 1)
    stri = jnp.where(kk > ss, 1.0, 0.0).astype(BF16)
    causal = ss < kk
    sign = jnp.uint32(0x80000000)

    acc_ref[...] = jnp.zeros_like(acc_ref)
    r_ref[...] = jnp.zeros_like(r_ref)

    def rows(c):
        return slice(c * tk, (c + 1) * tk)

    def both_heads(x):
        zero = jnp.zeros_like(x)
        return jnp.concatenate([jnp.where(head_mask[0], x, zero), jnp.where(head_mask[1], x, zero)], axis=0)

    def logits(j, chunks):
        k2 = both_heads(k_ref[0, pl.ds(pl.multiple_of(j * tk, tk), tk), :])
        for c in chunks:
            z_scr[c] = lax.dot_general(q_ref[0, rows(c), :], k2, NT_DIMS, preferred_element_type=F32)

    def later_sums(chunks, diag_chunk):
        for c in chunks:
            for h in range(2):
                cols = slice(h * tk, (h + 1) * tk)
                z = z_scr[c, :, cols]
                neg_abs = lax.bitcast_convert_type(lax.bitcast_convert_type(z, jnp.uint32) | sign, F32)
                sp = jnp.maximum(z, 0.0) + jnp.log2(1.0 + jnp.exp2(neg_abs))
                if c == diag_chunk:
                    sp = jnp.where(causal, sp, 0.0)
                z_scr[c, :, cols] = z - sp
                r = r_ref[h, rows(c), :]
                later = jnp.concatenate([r] * (tk // LANES), axis=1) + jnp.dot(
                    sp.astype(BF16), stri, preferred_element_type=F32)
                lat_scr[c, :, cols] = later
                r_ref[h, rows(c), :] = jnp.broadcast_to(later[:, 0:1] + sp[:, 0:1], (tk, LANES))

    def weigh(j, chunks, diag_chunk):
        v2 = both_heads(v_ref[0, pl.ds(pl.multiple_of(j * tk, tk), tk), :])
        for c in chunks:
            ws = []
            for h in range(2):
                cols = slice(h * tk, (h + 1) * tk)
                logw = z_scr[c, :, cols] - lat_scr[c, :, cols]
                if c == diag_chunk:
                    logw = jnp.where(causal, logw, NEG_INF)
                ws.append(jnp.exp2(logw).astype(BF16))
            acc_ref[rows(c), :] += jnp.dot(jnp.concatenate(ws, axis=1), v2, preferred_element_type=F32)

    all_chunks = list(range(n_chunks))
    base = i * n_chunks
    for d in range(n_chunks - 1, -1, -1):
        chunks = list(range(d, n_chunks))
        logits(base + d, chunks)
        later_sums(chunks, d)
        weigh(base + d, chunks, d)

    def body(jj, carry):
        j = base - 1 - jj
        logits(j, all_chunks)
        later_sums(all_chunks, None)
        weigh(j, all_chunks, None)
        return carry

    lax.fori_loop(0, base, body, 0)
    o_ref[0] = acc_ref[...].astype(o_ref.dtype)


def _sb_attention(proj3, tq, tk):
    b, s, _ = proj3.shape
    n_pairs = ATT_WIDTH // LANES
    kern = functools.partial(_sb_kernel, tq=tq, tk=tk)
    return pl.pallas_call(
        kern,
        grid=(b, n_pairs, s // tq),
        in_specs=[
            pl.BlockSpec((1, tq, LANES), lambda bi, p, i: (bi, i, p)),
            pl.BlockSpec((1, s, LANES), lambda bi, p, i: (bi, 0, n_pairs + p)),
            pl.BlockSpec((1, s, LANES), lambda bi, p, i: (bi, 0, 2 * n_pairs + p)),
        ],
        out_specs=pl.BlockSpec((1, tq, LANES), lambda bi, p, i: (bi, i, p)),
        out_shape=jax.ShapeDtypeStruct((b, s, ATT_WIDTH), BF16),
        scratch_shapes=[
            pltpu.VMEM((tq, LANES), F32),
            pltpu.VMEM((2, tq, LANES), F32),
            pltpu.VMEM((tq // tk, tk, 2 * tk), F32),
            pltpu.VMEM((tq // tk, tk, 2 * tk), F32),
        ],
        compiler_params=_params(("arbitrary", "arbitrary", "arbitrary")),
        name="sb_attention",
    )(proj3, proj3, proj3)


def _ca_kernel(q_ref, k_ref, v_ref, ring_ref, o_ref, z_scr, p_scr, b_scr, *, tq, win, rc):
    i = pl.program_id(2)
    ws = pl.multiple_of(jnp.maximum(i * tq - CA_PAD, 0), tq)

    @pl.when(i <= 2)
    def _():
        off = i * tq - ws
        r = lax.broadcasted_iota(jnp.int32, (tq, win), 0)
        c = lax.broadcasted_iota(jnp.int32, (tq, win), 1)
        relc = ((off + r) >> 6) - (c >> 6)
        valid = (relc >= 0) & (relc <= CA_LEFT_CHUNKS)
        for h in range(2):
            ring = jnp.broadcast_to(ring_ref[0, 0, h:h + 1, :], (tq, ring_ref.shape[-1]))
            toep = pltpu.roll(ring, 0, 1, stride=1, stride_axis=0)
            b_scr[h] = jnp.where(valid, toep[:, :win], NEG_INF)

    lane = lax.broadcasted_iota(jnp.int32, (1, LANES), 1)
    head_mask = (lane < HEAD_DIM, lane >= HEAD_DIM)
    k = k_ref[0, pl.ds(ws, win), :]
    v = v_ref[0, pl.ds(ws, win), :]
    zero = jnp.zeros_like(k)
    k2 = jnp.concatenate([jnp.where(head_mask[0], k, zero), jnp.where(head_mask[1], k, zero)], axis=0)
    z_scr[...] = lax.dot_general(q_ref[0], k2, NT_DIMS, preferred_element_type=F32)
    inv = [[None] * (tq // rc) for _ in range(2)]
    for h in range(2):
        for c in range(tq // rc):
            rows = slice(c * rc, (c + 1) * rc)
            cols = slice(h * win, (h + 1) * win)
            z = z_scr[rows, cols] + b_scr[h, rows, :]
            p = jnp.exp2(z - jnp.max(z, axis=1, keepdims=True))
            inv[h][c] = 1.0 / jnp.sum(p, axis=1, keepdims=True)
            p_scr[rows, cols] = p.astype(BF16)
    outs = [jnp.dot(p_scr[:, h * win:(h + 1) * win], jnp.where(head_mask[h], v, zero),
                    preferred_element_type=F32) for h in range(2)]
    o_ref[0] = (outs[0] * jnp.concatenate(inv[0], axis=0)
                + outs[1] * jnp.concatenate(inv[1], axis=0)).astype(o_ref.dtype)


def _ca_rings(rel_table, tq, win):
    ring_len = tq + win
    out = []
    for var in range(3):
        off = var * tq - max(var * tq - CA_PAD, 0)
        dist = np.concatenate([off - np.arange(win), [0], off + np.arange(tq - 1, 0, -1)])
        out.append(rel_table.astype(F32)[:, np.clip(dist, -REL_CLIP, REL_CLIP) + REL_CLIP])
    rings = jnp.stack(out) * math.log2(math.e)
    return rings.reshape(3, rel_table.shape[0] // 2, 2, ring_len)


def _ca_attention(proj3, rings, tq, win, rc):
    b, s, _ = proj3.shape
    n_pairs = ATT_WIDTH // LANES
    assert CHUNK == 64 and (tq + win) % LANES == 0
    kern = functools.partial(_ca_kernel, tq=tq, win=win, rc=rc)
    return pl.pallas_call(
        kern,
        grid=(b, n_pairs, s // tq),
        in_specs=[
            pl.BlockSpec((1, tq, LANES), lambda bi, p, i: (bi, i, 3 * n_pairs + p)),
            pl.BlockSpec((1, s, LANES), lambda bi, p, i: (bi, 0, 4 * n_pairs + p)),
            pl.BlockSpec((1, s, LANES), lambda bi, p, i: (bi, 0, 5 * n_pairs + p)),
            pl.BlockSpec((1, 1, 2, tq + win), lambda bi, p, i: (jnp.minimum(i, 2), p, 0, 0)),
        ],
        out_specs=pl.BlockSpec((1, tq, LANES), lambda bi, p, i: (bi, i, p)),
        out_shape=jax.ShapeDtypeStruct((b, s, ATT_WIDTH), BF16),
        scratch_shapes=[pltpu.VMEM((tq, 2 * win), F32), pltpu.VMEM((tq, 2 * win), BF16),
                        pltpu.VMEM((2, tq, win), F32)],
        compiler_params=_params(("arbitrary", "arbitrary", "arbitrary")),
        name="ca_attention",
    )(proj3, proj3, proj3, rings)


def _merge_kernel(osb_ref, oca_ref, gsb_ref, gca_ref, x_ref, psb_ref, pca_ref, wout_ref, nf_ref,
                  x1_ref, h2_ref, h2c_ref):
    br_sb = jnp.dot(osb_ref[...], psb_ref[...], preferred_element_type=F32)
    br_ca = jnp.dot(oca_ref[...], pca_ref[...], preferred_element_type=F32)
    merged = (jax.nn.sigmoid(gsb_ref[...].astype(F32)) * br_sb
              + jax.nn.sigmoid(gca_ref[...].astype(F32)) * br_ca)
    x1 = x_ref[...] + jnp.dot(merged.astype(BF16), wout_ref[...], preferred_element_type=F32)
    x1_ref[...] = x1
    ms = jnp.mean(x1 * x1, axis=-1, keepdims=True)
    h2 = x1 * lax.rsqrt(ms + RMS_EPS) * nf_ref[...]
    h2_ref[...] = h2
    tm = h2.shape[0]
    for j in range(SUBLANES):
        h2c_ref[pl.ds(j, tm, stride=SUBLANES), :] = h2[:, j * LANES:(j + 1) * LANES]


def _merge(o_sb, o_ca, proj, x2, psb, pca, wout, nf, tm):
    n = x2.shape[0]
    gate_blk = 3 * ATT_WIDTH * 2 // D_MODEL
    return pl.pallas_call(
        _merge_kernel,
        grid=(n // tm,),
        in_specs=[
            pl.BlockSpec((tm, ATT_WIDTH), lambda i: (i, 0)),
            pl.BlockSpec((tm, ATT_WIDTH), lambda i: (i, 0)),
            pl.BlockSpec((tm, D_MODEL), lambda i: (i, gate_blk)),
            pl.BlockSpec((tm, D_MODEL), lambda i: (i, gate_blk + 1)),
            pl.BlockSpec((tm, D_MODEL), lambda i: (i, 0)),
            pl.BlockSpec((ATT_WIDTH, D_MODEL), lambda i: (0, 0)),
            pl.BlockSpec((ATT_WIDTH, D_MODEL), lambda i: (0, 0)),
            pl.BlockSpec((D_MODEL, D_MODEL), lambda i: (0, 0)),
            pl.BlockSpec((1, D_MODEL), lambda i: (0, 0)),
        ],
        out_specs=[
            pl.BlockSpec((tm, D_MODEL), lambda i: (i, 0)),
            pl.BlockSpec((tm, D_MODEL), lambda i: (i, 0)),
            pl.BlockSpec((tm * SUBLANES, LANES), lambda i: (i, 0)),
        ],
        out_shape=[jax.ShapeDtypeStruct((n, D_MODEL), F32), jax.ShapeDtypeStruct((n, D_MODEL), F32),
                   jax.ShapeDtypeStruct((n * SUBLANES, LANES), F32)],
        compiler_params=_params(("arbitrary",)),
        name="merge",
    )(o_sb, o_ca, proj, proj, x2, psb, pca, wout, nf)


def _peer_candidates():
    return [(a, b) for a in range(PEER_TOPK) for b in range(PEER_TOPK) if (a + 1) * (b + 1) <= PEER_TOPK]


def _extract_top(s_scr, iota_scr, n_rounds, sentinel):
    vals, ids = [], []
    for _ in range(n_rounds):
        s = s_scr[...]
        m = jnp.max(s, axis=0, keepdims=True)
        cand = jnp.where(s == m, iota_scr[...], sentinel)
        am = jnp.min(cand, axis=0, keepdims=True)
        vals.append(m)
        ids.append(am)
        s_scr[...] = jnp.where(cand == am, -jnp.inf, s)
    return vals, ids


ROUTE_TOKENS = 128


def _route_scratch():
    n_rows = -(-len(_peer_candidates()) // SUBLANES) * SUBLANES
    return [
        pltpu.VMEM((n_rows, ROUTE_TOKENS), F32),
        pltpu.VMEM((n_rows, ROUTE_TOKENS), F32),
        pltpu.VMEM((PEER_SLOTS, ROUTE_TOKENS), F32),
        pltpu.VMEM((PEER_SLOTS, ROUTE_TOKENS), F32),
        pltpu.VMEM((2, PEER_N_KEYS, ROUTE_TOKENS), F32),
        pltpu.VMEM((PEER_N_KEYS, ROUTE_TOKENS), F32),
    ]


def _route_init(scr):
    cv_scr, ci_scr, _, _, _, iota_scr = scr
    n_cand = len(_peer_candidates())
    n_rows = cv_scr.shape[0]

    @pl.when(pl.program_id(0) == 0)
    def _():
        cv_scr[n_cand:, :] = jnp.full((n_rows - n_cand, ROUTE_TOKENS), -jnp.inf, F32)
        ci_scr[n_cand:, :] = jnp.zeros((n_rows - n_cand, ROUTE_TOKENS), F32)
        iota_scr[...] = lax.broadcasted_iota(jnp.int32, iota_scr.shape, 0).astype(F32)


def _slot_row(hd, r):
    if isinstance(hd, int):
        return slice(hd * PEER_TOPK + r, hd * PEER_TOPK + r + 1)
    return pl.ds(hd * PEER_TOPK + r, 1)


def _route_head(h_ref, wq_ref, keys_ref, hd, scr):
    cv_scr, ci_scr, ri_scr, rg_scr, s_scr, iota_scr = scr
    cands = _peer_candidates()
    n_rows = cv_scr.shape[0]
    h = h_ref[...].astype(BF16)
    iota_c = lax.broadcasted_iota(jnp.int32, (n_rows, ROUTE_TOKENS), 0).astype(F32)
    for p in range(2):
        hp = hd * 2 + p
        q = jnp.dot(h, wq_ref[hp], preferred_element_type=F32).astype(BF16)
        s_scr[p] = lax.dot_general(keys_ref[hp], q, NT_DIMS, preferred_element_type=F32)
    tops = [_extract_top(s_scr.at[p], iota_scr, PEER_TOPK, float(PEER_N_KEYS)) for p in range(2)]
    (v1, i1), (v2, i2) = tops
    i1s = [x * float(PEER_N_KEYS * PACK_ROWS) for x in i1]
    i2s = [x * float(PACK_ROWS) for x in i2]
    for r, (a, b) in enumerate(cands):
        cv_scr[r:r + 1, :] = v1[a] + v2[b]
        ci_scr[r:r + 1, :] = i1s[a] + i2s[b]
    cv = cv_scr[...]
    ci = ci_scr[...]
    best = []
    for r in range(PEER_TOPK):
        m = jnp.max(cv, axis=0, keepdims=True)
        cand = jnp.where(cv == m, iota_c, float(n_rows))
        hit = cand == jnp.min(cand, axis=0, keepdims=True)
        e = jnp.sum(jnp.where(hit, ci, 0.0), axis=0, keepdims=True)
        cv = jnp.where(hit, -jnp.inf, cv)
        best.append(m)
        ri_scr[_slot_row(hd, r), :] = e
    ex = [jnp.exp(bv - best[0]) for bv in best]
    tot = ex[0]
    for t in ex[1:]:
        tot = tot + t
    inv = 1.0 / tot
    for r in range(PEER_TOPK):
        rg_scr[_slot_row(hd, r), :] = ex[r] * inv


def _route_emit(idx_ref, g_ref, ri_scr, rg_scr):
    idx_ref[...] = ri_scr[...].T.astype(jnp.int32)
    g_ref[...] = rg_scr[...].T


def _route_kernel(h_ref, wq_ref, keys_ref, idx_ref, g_ref, *scr):
    _route_init(scr)
    for hd in range(PEER_HEADS):
        _route_head(h_ref, wq_ref, keys_ref, hd, scr)
    _route_emit(idx_ref, g_ref, scr[2], scr[3])


def _route_in_specs(block_index):
    return [
        pl.BlockSpec((ROUTE_TOKENS, D_MODEL), block_index),
        pl.BlockSpec((2 * PEER_HEADS, D_MODEL, PEER_HALF), lambda i: (0, 0, 0)),
        pl.BlockSpec((2 * PEER_HEADS, PEER_N_KEYS, PEER_HALF), lambda i: (0, 0, 0)),
    ]


def _route_out(n, block_index):
    specs = [pl.BlockSpec((ROUTE_TOKENS, PEER_SLOTS), block_index),
             pl.BlockSpec((ROUTE_TOKENS, PEER_SLOTS), block_index)]
    shapes = [jax.ShapeDtypeStruct((n, PEER_SLOTS), jnp.int32), jax.ShapeDtypeStruct((n, PEER_SLOTS), F32)]
    return specs, shapes


def _route(h2, wq3, keys_bf, n, tok_off=0):
    out_specs, out_shape = _route_out(n, lambda i: (i, 0))
    return pl.pallas_call(
        _route_kernel,
        grid=(n // ROUTE_TOKENS,),
        in_specs=_route_in_specs(lambda i: (i + tok_off // ROUTE_TOKENS, 0)),
        out_specs=out_specs,
        out_shape=out_shape,
        scratch_shapes=_route_scratch(),
        compiler_params=_params(("arbitrary",)),
        name="peer_route",
    )(h2, wq3, keys_bf)


def _pack_kernel(t_ref, o_ref, *row_major_ref):
    te = t_ref.shape[0]
    for r in range(PACK_ROWS):
        lo = t_ref[:, (2 * r) * LANES:(2 * r + 1) * LANES].astype(BF16).astype(F32)
        hi = t_ref[:, (2 * r + 1) * LANES:(2 * r + 2) * LANES].astype(BF16).astype(F32)
        word = lax.bitcast_convert_type(hi, jnp.uint32) | (lax.bitcast_convert_type(lo, jnp.uint32) >> 16)
        o_ref[pl.ds(r, te, stride=PACK_ROWS), :] = word
        for ref in row_major_ref:
            ref[:, r * LANES:(r + 1) * LANES] = word


def _pack_table(t, with_row_major=False):
    e = t.shape[0]
    te = _row_tile(e, 512)
    out_specs = [pl.BlockSpec((te * PACK_ROWS, LANES), lambda i: (i, 0))]
    out_shape = [jax.ShapeDtypeStruct((e * PACK_ROWS, LANES), jnp.uint32)]
    if with_row_major:
        out_specs.append(pl.BlockSpec((te, PACK_ROWS * LANES), lambda i: (i, 0)))
        out_shape.append(jax.ShapeDtypeStruct((e, PACK_ROWS * LANES), jnp.uint32))
    out = pl.pallas_call(
        _pack_kernel,
        grid=(e // te,),
        in_specs=[pl.BlockSpec((te, D_MODEL), lambda i: (i, 0))],
        out_specs=out_specs,
        out_shape=out_shape,
        compiler_params=_params(("arbitrary",)),
        name="pack_table",
    )(t)
    return out if with_row_major else out[0]


def _table_spec(n_rows):
    return pl.BlockSpec((n_rows, LANES), lambda i: (0, 0), pipeline_mode=pl.Buffered(1))


def _gather_rows(idx_ref, t, tab_ref, buf):
    for k in range(PEER_SLOTS):
        e = pl.multiple_of(idx_ref[t, k], PACK_ROWS)
        buf[k * PACK_ROWS:(k + 1) * PACK_ROWS, :] = tab_ref[pl.ds(e, PACK_ROWS), :]


def _gathered_bf16(buf):
    return pltpu.bitcast(buf[...], BF16)


def _chunk_pattern():
    shape = (2 * SUBLANES, PEER_SLOTS * SUBLANES)
    row = lax.broadcasted_iota(jnp.int32, shape, 0)
    col = lax.broadcasted_iota(jnp.int32, shape, 1)
    return (row & (SUBLANES - 1)) == (col & (SUBLANES - 1))


def _token_rows(t):
    if isinstance(t, int):
        return slice(t * SUBLANES, (t + 1) * SUBLANES)
    return pl.ds(pl.multiple_of(t * SUBLANES, SUBLANES), SUBLANES)


def _load_token_chunks(ref, t):
    row = ref[pl.ds(t, 1), :]
    return jnp.concatenate([row[:, j * LANES:(j + 1) * LANES] for j in range(SUBLANES)], axis=0)


def _store_token_chunks(ref, t, val):
    ref[pl.ds(t, 1), :] = jnp.concatenate([val[j:j + 1, :] for j in range(SUBLANES)], axis=1)


def _split_bf16(x):
    hi = x.astype(BF16)
    return hi, (x - hi.astype(F32)).astype(BF16)


def _token_loop(idx_ref, tab_ref, bufs, tb, unroll, compute, side_work=None):
    _gather_rows(idx_ref, 0, tab_ref, bufs[0])

    def body(i, carry):
        if side_work is not None:
            side_work(i)
        for u in range(unroll):
            t = i * unroll + u
            _gather_rows(idx_ref, jnp.minimum(t + 1, tb - 1), tab_ref, bufs[(u + 1) % 2])
            compute(t, bufs[u % 2])
        return carry

    lax.fori_loop(0, tb // unroll, body, 0)


def _peer_dot_body(idx_ref, h_ref, g_ref, tab_ref, c_ref, buf0, buf1, part_scr, tb, unroll, side_work):
    pattern = _chunk_pattern()

    def compute(t, buf):
        h16 = jnp.concatenate(_split_bf16(h_ref[_token_rows(t), :]), axis=0)
        r = lax.dot_general(h16, _gathered_bf16(buf), NT_DIMS, preferred_element_type=F32)
        r = jnp.where(pattern, r, 0.0)
        part_scr[pl.ds(t, 1), :] = jnp.sum(r[:SUBLANES] + r[SUBLANES:], axis=0, keepdims=True)

    _token_loop(idx_ref, tab_ref, (buf0, buf1), tb, unroll, compute, side_work)

    n_part = PEER_SLOTS * SUBLANES
    slot_of = lax.broadcasted_iota(jnp.int32, (n_part, PEER_SLOTS), 0) // SUBLANES
    fold = jnp.where(slot_of == lax.broadcasted_iota(jnp.int32, (n_part, PEER_SLOTS), 1), 1.0, 0.0).astype(BF16)
    p_hi, p_lo = _split_bf16(part_scr[...])
    a = (jnp.dot(p_hi, fold, preferred_element_type=F32)
         + jnp.dot(p_lo, fold, preferred_element_type=F32))
    c_ref[...] = g_ref[...] * (0.5 * a * (1.0 + lax.erf(a * (1.0 / math.sqrt(2.0)))))


def _peer_dot_kernel(idx_ref, h_ref, g_ref, tab_ref, c_ref, buf0, buf1, part_scr, *, tb, unroll):
    _peer_dot_body(idx_ref, h_ref, g_ref, tab_ref, c_ref, buf0, buf1, part_scr, tb, unroll, None)


def _peer_dot_route_kernel(idx_ref, h_ref, g_ref, tab_ref, rh_ref, wq_ref, keys_ref,
                           c_ref, ridx_ref, rg_ref, buf0, buf1, part_scr, *route_scr, tb, unroll):
    trips = tb // unroll
    phase = pl.program_id(0) % (PEER_HEADS // trips)
    _route_init(route_scr)

    def side_work(i):
        _route_head(rh_ref, wq_ref, keys_ref, phase * trips + i, route_scr)

    _peer_dot_body(idx_ref, h_ref, g_ref, tab_ref, c_ref, buf0, buf1, part_scr, tb, unroll, side_work)

    @pl.when(phase == PEER_HEADS // trips - 1)
    def _():
        _route_emit(ridx_ref, rg_ref, route_scr[2], route_scr[3])


def _peer_dot_specs(tab, tb, h_block_off):
    in_specs = [
        pl.BlockSpec((tb, PEER_SLOTS), lambda i: (i, 0), memory_space=pltpu.SMEM),
        pl.BlockSpec((tb * SUBLANES, LANES), lambda i: (i + h_block_off, 0)),
        pl.BlockSpec((tb, PEER_SLOTS), lambda i: (i, 0)),
        _table_spec(tab.shape[0]),
    ]
    scratch = [pltpu.VMEM((PEER_SLOTS * PACK_ROWS, LANES), jnp.uint32),
               pltpu.VMEM((PEER_SLOTS * PACK_ROWS, LANES), jnp.uint32),
               pltpu.VMEM((tb, PEER_SLOTS * SUBLANES), F32)]
    return in_specs, scratch


def _peer_dot(idx, h2r, gate, tab, tb, unroll, tok_off=0):
    n = idx.shape[0]
    in_specs, scratch = _peer_dot_specs(tab, tb, tok_off // tb)
    return pl.pallas_call(
        functools.partial(_peer_dot_kernel, tb=tb, unroll=unroll),
        grid=(n // tb,),
        in_specs=in_specs,
        out_specs=pl.BlockSpec((tb, PEER_SLOTS), lambda i: (i, 0)),
        out_shape=jax.ShapeDtypeStruct((n, PEER_SLOTS), F32),
        scratch_shapes=scratch,
        compiler_params=_params(("arbitrary",)),
        name="peer_dot",
    )(idx, h2r, gate, tab)


def _peer_dot_route(idx, h2r, gate, tab, h2, wq3, keys_bf, tb, unroll, tok_off, route_off):
    n = idx.shape[0]
    assert PEER_HEADS % (tb // unroll) == 0
    steps_per_block = PEER_HEADS // (tb // unroll)
    assert steps_per_block * tb == ROUTE_TOKENS and route_off % ROUTE_TOKENS == 0
    in_specs, scratch = _peer_dot_specs(tab, tb, tok_off // tb)
    route_in_block = lambda i: (i // steps_per_block + route_off // ROUTE_TOKENS, 0)
    route_block = lambda i: (i // steps_per_block, 0)
    route_out_specs, route_out_shape = _route_out(n, route_block)
    return pl.pallas_call(
        functools.partial(_peer_dot_route_kernel, tb=tb, unroll=unroll),
        grid=(n // tb,),
        in_specs=in_specs + _route_in_specs(route_in_block),
        out_specs=[pl.BlockSpec((tb, PEER_SLOTS), lambda i: (i, 0))] + route_out_specs,
        out_shape=[jax.ShapeDtypeStruct((n, PEER_SLOTS), F32)] + route_out_shape,
        scratch_shapes=scratch + _route_scratch(),
        compiler_params=_params(("arbitrary",)),
        name="peer_dot_route",
    )(idx, h2r, gate, tab, h2, wq3, keys_bf)


def _peer_axpy_kernel(idx_ref, c_ref, x_ref, gf_ref, tab_ref, y_ref, buf0, buf1, cexp_scr, *, tb, unroll):
    pattern = _chunk_pattern()
    n_part = PEER_SLOTS * SUBLANES
    slot_of = lax.broadcasted_iota(jnp.int32, (PEER_SLOTS, n_part), 1) // SUBLANES
    spread = jnp.where(slot_of == lax.broadcasted_iota(jnp.int32, (PEER_SLOTS, n_part), 0), 1.0, 0.0).astype(BF16)
    c_hi, c_lo = _split_bf16(c_ref[...])
    cexp_scr[0:tb, :] = jnp.dot(c_hi, spread, preferred_element_type=F32)
    cexp_scr[tb:2 * tb, :] = jnp.dot(c_lo, spread, preferred_element_type=F32)

    def compute(t, buf):
        coef = jnp.concatenate([jnp.broadcast_to(cexp_scr[pl.ds(t, 1), :], (SUBLANES, n_part)),
                                jnp.broadcast_to(cexp_scr[pl.ds(tb + t, 1), :], (SUBLANES, n_part))], axis=0)
        coef = jnp.where(pattern, coef, 0.0).astype(BF16)
        o16 = jnp.dot(coef, _gathered_bf16(buf), preferred_element_type=F32)
        x2 = _load_token_chunks(x_ref, t) + (o16[:SUBLANES] + o16[SUBLANES:])
        ss = jnp.sum(jnp.sum(x2 * x2, axis=1, keepdims=True), axis=0, keepdims=True)
        _store_token_chunks(y_ref, t, x2 * lax.rsqrt(ss * (1.0 / D_MODEL) + RMS_EPS) * gf_ref[...])

    _token_loop(idx_ref, tab_ref, (buf0, buf1), tb, unroll, compute)


def _peer_axpy(idx, c, x1r, gfr, tab, tb, unroll, n_tokens):
    n = idx.shape[0]
    kern = functools.partial(_peer_axpy_kernel, tb=tb, unroll=unroll)
    return pl.pallas_call(
        kern,
        grid=(n_tokens // tb,),
        in_specs=[
            pl.BlockSpec((tb, PEER_SLOTS), lambda i: (i, 0), memory_space=pltpu.SMEM),
            pl.BlockSpec((tb, PEER_SLOTS), lambda i: (i, 0)),
            pl.BlockSpec((tb, D_MODEL), lambda i: (i, 0)),
            pl.BlockSpec((SUBLANES, LANES), lambda i: (0, 0)),
            _table_spec(tab.shape[0]),
        ],
        out_specs=pl.BlockSpec((tb, D_MODEL), lambda i: (i, 0)),
        out_shape=jax.ShapeDtypeStruct((n, D_MODEL), F32),
        scratch_shapes=[pltpu.VMEM((PEER_SLOTS * PACK_ROWS, LANES), jnp.uint32),
                        pltpu.VMEM((PEER_SLOTS * PACK_ROWS, LANES), jnp.uint32),
                        pltpu.VMEM((2 * tb, PEER_SLOTS * SUBLANES), F32)],
        compiler_params=_params(("arbitrary",)),
        name="peer_axpy",
    )(idx, c, x1r, gfr, tab)


SC_WORKERS = 32
SC_LANES = 16
SC_CHUNK_TOKENS = 16
SC_HALF_ROWS = PEER_SLOTS // 2
SC_ROW_BLOCK = 8
SC_COL_GROUPS = LANES // SC_LANES
SC_SHARE_NUM, SC_SHARE_DEN = 9, 32


def _peer_axpy_sc(expert, c, tab_rows):
    n = expert.shape[0]
    per = n // SC_WORKERS
    assert n % (SC_WORKERS * SC_CHUNK_TOKENS) == 0
    mesh = plsc.VectorSubcoreMesh(core_axis_name="c", subcore_axis_name="s")
    params = pltpu.CompilerParams()
    if "needs_layout_passes" in pltpu.CompilerParams.__dataclass_fields__:
        params = dataclasses.replace(params, needs_layout_passes=False)
    expert2 = expert.reshape(n * 2, SC_HALF_ROWS)
    words = PACK_ROWS * LANES

    @functools.partial(
        pl.kernel, mesh=mesh, compiler_params=params,
        out_type=jax.ShapeDtypeStruct((n, D_MODEL), F32),
        scratch_types=[pltpu.VMEM((2 * SC_CHUNK_TOKENS, SC_HALF_ROWS), jnp.int32),
                       pltpu.VMEM((SC_CHUNK_TOKENS, PEER_SLOTS), F32),
                       pltpu.VMEM((SC_HALF_ROWS, words), jnp.uint32),
                       pltpu.VMEM((SC_HALF_ROWS, words), jnp.uint32),
                       pltpu.VMEM((D_MODEL,), F32),
                       pltpu.SemaphoreType.DMA, pltpu.SemaphoreType.DMA],
    )
    def sc_kernel(idx_hbm, c_hbm, tab_hbm, out_hbm, idx_v, c_v, rows0, rows1, out_v, sem0, sem1):
        wid = lax.axis_index("s") * 2 + lax.axis_index("c")
        bufs = ((rows0, sem0), (rows1, sem1))

        def gather(u, b):
            return pltpu.make_async_copy(tab_hbm.at[idx_v.at[u]], bufs[b][0], bufs[b][1])

        def accumulate(tl, half, b):
            rows = bufs[b][0]

            @pl.loop(0, PACK_ROWS)
            def _(cb):
                @pl.loop(0, SC_HALF_ROWS // SC_ROW_BLOCK)
                def _(rb):
                    acc = [None] * (2 * SC_COL_GROUPS)
                    for r in range(SC_ROW_BLOCK):
                        kk = rb * SC_ROW_BLOCK + r
                        zero = jnp.zeros((SC_LANES,), jnp.int32)
                        ck = plsc.load_gather(c_v, [zero + tl, zero + (half * SC_HALF_ROWS + kk)])
                        for g in range(SC_COL_GROUPS):
                            w = rows[kk, pl.ds(cb * LANES + g * SC_LANES, SC_LANES)]
                            hi = ck * lax.bitcast_convert_type(w & jnp.uint32(0xFFFF0000), F32)
                            lo = ck * lax.bitcast_convert_type(w << 16, F32)
                            acc[2 * g] = hi if acc[2 * g] is None else acc[2 * g] + hi
                            acc[2 * g + 1] = lo if acc[2 * g + 1] is None else acc[2 * g + 1] + lo
                    for g in range(SC_COL_GROUPS):
                        plsc.addupdate(out_v.at[pl.ds((2 * cb + 1) * LANES + g * SC_LANES, SC_LANES)], acc[2 * g])
                        plsc.addupdate(out_v.at[pl.ds(2 * cb * LANES + g * SC_LANES, SC_LANES)], acc[2 * g + 1])

        @pl.loop(0, per // SC_CHUNK_TOKENS)
        def _(ch):
            t0 = wid * per + ch * SC_CHUNK_TOKENS
            pltpu.sync_copy(idx_hbm.at[pl.ds(2 * t0, 2 * SC_CHUNK_TOKENS)], idx_v)
            pltpu.sync_copy(c_hbm.at[pl.ds(t0, SC_CHUNK_TOKENS)], c_v)
            gather(0, 0).start()

            @pl.loop(0, SC_CHUNK_TOKENS)
            def _(tl):
                for g in range(D_MODEL // SC_LANES):
                    out_v[pl.ds(g * SC_LANES, SC_LANES)] = jnp.zeros((SC_LANES,), F32)
                gather(2 * tl + 1, 1).start()
                gather(2 * tl, 0).wait()
                accumulate(tl, 0, 0)

                @pl.when(tl + 1 < SC_CHUNK_TOKENS)
                def _():
                    gather(2 * tl + 2, 0).start()

                gather(2 * tl + 1, 1).wait()
                accumulate(tl, 1, 1)
                pltpu.sync_copy(out_v, out_hbm.at[t0 + tl])

    return sc_kernel(expert2, c, tab_rows)


def _peer_dot_sc(expert, h2, tab_rows, tok_off):
    n = expert.shape[0]
    per = n // SC_WORKERS
    assert n % (SC_WORKERS * SC_CHUNK_TOKENS) == 0 and PEER_SLOTS % SC_LANES == 0
    mesh = plsc.VectorSubcoreMesh(core_axis_name="c", subcore_axis_name="s")
    params = pltpu.CompilerParams()
    if "needs_layout_passes" in pltpu.CompilerParams.__dataclass_fields__:
        params = dataclasses.replace(params, needs_layout_passes=False)
    expert2 = expert.reshape(n * 2, SC_HALF_ROWS)
    words = PACK_ROWS * LANES
    row_block = SC_LANES

    @functools.partial(
        pl.kernel, mesh=mesh, compiler_params=params,
        out_type=jax.ShapeDtypeStruct((n, PEER_SLOTS), F32),
        scratch_types=[pltpu.VMEM((2 * SC_CHUNK_TOKENS, SC_HALF_ROWS), jnp.int32),
                       pltpu.VMEM((SC_CHUNK_TOKENS, D_MODEL), F32),
                       pltpu.VMEM((SC_HALF_ROWS, words), jnp.uint32),
                       pltpu.VMEM((SC_HALF_ROWS, words), jnp.uint32),
                       pltpu.VMEM((PEER_SLOTS,), F32),
                       pltpu.SemaphoreType.DMA, pltpu.SemaphoreType.DMA],
    )
    def sc_kernel(idx_hbm, h_hbm, tab_hbm, out_hbm, idx_v, h_v, rows0, rows1, a_v, sem0, sem1):
        wid = lax.axis_index("s") * 2 + lax.axis_index("c")
        bufs = ((rows0, sem0), (rows1, sem1))
        lane = lax.iota(jnp.int32, SC_LANES)

        def gather(u, b):
            return pltpu.make_async_copy(tab_hbm.at[idx_v.at[u]], bufs[b][0], bufs[b][1])

        def accumulate(tl, half, b):
            rows = bufs[b][0]

            @pl.loop(0, PACK_ROWS)
            def _(cb):
                h_hi = [h_v[tl, pl.ds((2 * cb + 1) * LANES + g * SC_LANES, SC_LANES)] for g in range(SC_COL_GROUPS)]
                h_lo = [h_v[tl, pl.ds(2 * cb * LANES + g * SC_LANES, SC_LANES)] for g in range(SC_COL_GROUPS)]

                @pl.loop(0, SC_HALF_ROWS // row_block)
                def _(rb):
                    res = jnp.zeros((SC_LANES,), F32)
                    for r in range(row_block):
                        kk = rb * row_block + r
                        acc = None
                        for g in range(SC_COL_GROUPS):
                            w = rows[kk, pl.ds(cb * LANES + g * SC_LANES, SC_LANES)]
                            hi = h_hi[g] * lax.bitcast_convert_type(w & jnp.uint32(0xFFFF0000), F32)
                            lo = h_lo[g] * lax.bitcast_convert_type(w << 16, F32)
                            acc = hi + lo if acc is None else acc + hi + lo
                        res = jnp.where(lane == r, jnp.sum(acc), res)
                    plsc.addupdate(a_v.at[pl.ds(half * SC_HALF_ROWS + rb * row_block, SC_LANES)], res)

        @pl.loop(0, per // SC_CHUNK_TOKENS)
        def _(ch):
            t0 = wid * per + ch * SC_CHUNK_TOKENS
            pltpu.sync_copy(idx_hbm.at[pl.ds(2 * t0, 2 * SC_CHUNK_TOKENS)], idx_v)
            pltpu.sync_copy(h_hbm.at[pl.ds(tok_off + t0, SC_CHUNK_TOKENS)], h_v)
            gather(0, 0).start()

            @pl.loop(0, SC_CHUNK_TOKENS)
            def _(tl):
                for g in range(PEER_SLOTS // SC_LANES):
                    a_v[pl.ds(g * SC_LANES, SC_LANES)] = jnp.zeros((SC_LANES,), F32)
                gather(2 * tl + 1, 1).start()
                gather(2 * tl, 0).wait()
                accumulate(tl, 0, 0)

                @pl.when(tl + 1 < SC_CHUNK_TOKENS)
                def _():
                    gather(2 * tl + 2, 0).start()

                gather(2 * tl + 1, 1).wait()
                accumulate(tl, 1, 1)
                pltpu.sync_copy(a_v, out_hbm.at[t0 + tl])

    return sc_kernel(expert2, h2, tab_rows)


def _gelu_gate_kernel(a_ref, g_ref, c_ref):
    a = a_ref[...]
    c_ref[...] = g_ref[...] * (0.5 * a * (1.0 + lax.erf(a * (1.0 / math.sqrt(2.0)))))


def _gelu_gate(a, gate):
    n = a.shape[0]
    tm = _row_tile(n, 1024)
    spec = pl.BlockSpec((tm, PEER_SLOTS), lambda i: (i, 0))
    return pl.pallas_call(
        _gelu_gate_kernel, grid=(n // tm,), in_specs=[spec, spec], out_specs=spec,
        out_shape=jax.ShapeDtypeStruct((n, PEER_SLOTS), F32),
        compiler_params=_params(("arbitrary",)), name="gelu_gate",
    )(a, gate)


def _finish_kernel(y_hbm, x_ref, p_ref, gf_ref, o_ref):
    del y_hbm
    x2 = x_ref[...] + p_ref[...]
    ms = jnp.mean(x2 * x2, axis=-1, keepdims=True)
    o_ref[...] = x2 * lax.rsqrt(ms + RMS_EPS) * gf_ref[...]


def _peer_finish(y, x1, p_tail, gf, tm):
    n = y.shape[0]
    n_tail = p_tail.shape[0]
    off = (n - n_tail) // tm
    assert (n - n_tail) % tm == 0 and n_tail % tm == 0
    return pl.pallas_call(
        _finish_kernel,
        grid=(n_tail // tm,),
        in_specs=[
            pl.BlockSpec(memory_space=pl.ANY),
            pl.BlockSpec((tm, D_MODEL), lambda i: (i + off, 0)),
            pl.BlockSpec((tm, D_MODEL), lambda i: (i, 0)),
            pl.BlockSpec((1, D_MODEL), lambda i: (0, 0)),
        ],
        out_specs=pl.BlockSpec((tm, D_MODEL), lambda i: (i + off, 0)),
        out_shape=jax.ShapeDtypeStruct((n, D_MODEL), F32),
        input_output_aliases={0: 0},
        compiler_params=_params(("arbitrary",)),
        name="peer_finish",
    )(y, x1, p_tail, gf)


def _row_tile(n, want):
    t = min(want, n)
    assert n % t == 0
    return t


def _layer(x2, b, s, norm_mix, w_in, w_sb_proj, w_ca_proj, ca_rel_bias, w_out, norm_ffn,
           peer_w_query, peer_sub_keys, peer_u, peer_v, final_gain):
    n = b * s
    col = np.arange(IN_COLS)
    is_q = (col < ATT_WIDTH) | ((col >= 3 * ATT_WIDTH) & (col < 4 * ATT_WIDTH))
    col_scale = np.where(is_q, math.log2(math.e) / math.sqrt(HEAD_DIM), 1.0)
    w_in_bf = (w_in * jnp.asarray(col_scale, F32)[None, :]).astype(BF16)

    proj = _inproj(x2, norm_mix.reshape(1, D_MODEL), w_in_bf, _row_tile(n, 512))
    proj3 = proj.reshape(b, s, IN_COLS)

    tq = 256
    assert s % tq == 0 and s >= tq + CA_PAD
    sb_tq = min(2048, s)
    assert s % sb_tq == 0
    o_sb = _sb_attention(proj3, sb_tq, 256)
    win = tq + CA_PAD
    o_ca = _ca_attention(proj3, _ca_rings(ca_rel_bias, tq, win), tq, win, 64)

    x1, h2, h2r = _merge(o_sb.reshape(n, ATT_WIDTH), o_ca.reshape(n, ATT_WIDTH), proj, x2,
                    w_sb_proj.astype(BF16), w_ca_proj.astype(BF16), w_out.astype(BF16),
                    norm_ffn.reshape(1, D_MODEL), _row_tile(n, 512))

    keys_bf = peer_sub_keys.reshape(2 * PEER_HEADS, PEER_N_KEYS, PEER_HALF).astype(BF16)
    wq3 = jnp.transpose(peer_w_query.astype(BF16).reshape(D_MODEL, 2 * PEER_HEADS, PEER_HALF), (1, 0, 2))
    tab_u, tab_u_rows = _pack_table(peer_u, with_row_major=True)
    tbg, unroll = 64, 16
    sc_unit = SC_WORKERS * SC_CHUNK_TOKENS
    n_seg = b if (b > 1 and s % ROUTE_TOKENS == 0) else 1
    seg = n // n_seg
    assert seg % ROUTE_TOKENS == 0
    use_sc = seg % sc_unit == 0
    idx_j, gate_j = _route(h2, wq3, keys_bf, seg)
    idx_parts, c_parts = [], []
    for j in range(n_seg):
        idx_parts.append(idx_j)
        last = j + 1 == n_seg
        if j % 2 == 1 and use_sc:
            a_j = _peer_dot_sc(idx_j // PACK_ROWS, h2, tab_u_rows, j * seg)
            c_j = _gelu_gate(a_j, gate_j)
            if not last:
                idx_j, gate_j = _route(h2, wq3, keys_bf, seg, (j + 1) * seg)
        elif not last:
            c_j, idx_j, gate_j = _peer_dot_route(idx_j, h2r, gate_j, tab_u, h2, wq3, keys_bf, tbg, unroll,
                                                 j * seg, (j + 1) * seg)
        else:
            c_j = _peer_dot(idx_j, h2r, gate_j, tab_u, tbg, unroll, j * seg)
        c_parts.append(c_j)
    idx = jnp.concatenate(idx_parts, axis=0)
    c = jnp.concatenate(c_parts, axis=0)
    tab_v, tab_v_rows = _pack_table(peer_v, with_row_major=True)
    n_sc = (n * SC_SHARE_NUM // SC_SHARE_DEN) // sc_unit * sc_unit
    n_tc = n - n_sc
    y = _peer_axpy(idx, c, x1, final_gain.reshape(SUBLANES, LANES), tab_v, tbg, unroll, n_tc)
    if n_sc:
        p_sc = _peer_axpy_sc(idx[n_tc:] // PACK_ROWS, c[n_tc:], tab_v_rows)
        y = _peer_finish(y, x1, p_sc, final_gain.reshape(1, D_MODEL), sc_unit)
    return y


def kernel(x, norm_mix, w_in, w_sb_proj, w_ca_proj, ca_rel_bias, w_out, norm_ffn,
           peer_w_query, peer_sub_keys, peer_u, peer_v, norm_final):
    b, s, d = x.shape
    depth = norm_mix.shape[0]
    assert d == D_MODEL and depth == 1
    y = _layer(x.reshape(b * s, d), b, s, norm_mix[0], w_in[0], w_sb_proj[0], w_ca_proj[0],
               ca_rel_bias[0], w_out[0], norm_ffn[0], peer_w_query[0], peer_sub_keys[0],
               peer_u[0], peer_v[0], norm_final)
    return y.reshape(b, s, d)
```

```python
import dataclasses
import functools
import math

import numpy as np
import jax
import jax.numpy as jnp
from jax import lax
from jax.experimental import pallas as pl
from jax.experimental.pallas import tpu as pltpu
from jax.experimental.pallas import tpu_sc as plsc

D_MODEL = 1024
HEAD_DIM = 64
N_HEADS = 8
ATT_WIDTH = N_HEADS * HEAD_DIM
IN_COLS = 6 * ATT_WIDTH + 2 * D_MODEL
CHUNK = 64
CA_LEFT_CHUNKS = 8
CA_PAD = CA_LEFT_CHUNKS * CHUNK
REL_CLIP = 128
PEER_HEADS = 8
PEER_N_KEYS = 128
PEER_HALF = 128
PEER_TOPK = 16
PEER_SLOTS = PEER_HEADS * PEER_TOPK
RMS_EPS = 1e-6
NEG_INF = -1e30

LANES = 128
SUBLANES = 8
PACK_ROWS = D_MODEL // 2 // LANES

VMEM_LIMIT = 56 * 1024 * 1024

F32 = jnp.float32
BF16 = jnp.bfloat16
NT_DIMS = (((1,), (1,)), ((), ()))


def _params(sem, vmem=VMEM_LIMIT):
    return pltpu.CompilerParams(dimension_semantics=sem, vmem_limit_bytes=vmem)


def _inproj_kernel(x_ref, g_ref, w_ref, o_ref):
    x = x_ref[...]
    ms = jnp.mean(x * x, axis=-1, keepdims=True)
    h = (x * lax.rsqrt(ms + RMS_EPS) * g_ref[...]).astype(BF16)
    for c in range(IN_COLS // D_MODEL):
        cols = slice(c * D_MODEL, (c + 1) * D_MODEL)
        o_ref[:, cols] = jnp.dot(h, w_ref[:, cols], preferred_element_type=F32).astype(o_ref.dtype)


def _inproj(x2, g, w_bf, tm):
    n = x2.shape[0]
    return pl.pallas_call(
        _inproj_kernel,
        grid=(n // tm,),
        in_specs=[
            pl.BlockSpec((tm, D_MODEL), lambda i: (i, 0)),
            pl.BlockSpec((1, D_MODEL), lambda i: (0, 0)),
            pl.BlockSpec((D_MODEL, IN_COLS), lambda i: (0, 0)),
        ],
        out_specs=pl.BlockSpec((tm, IN_COLS), lambda i: (i, 0)),
        out_shape=jax.ShapeDtypeStruct((n, IN_COLS), BF16),
        compiler_params=_params(("arbitrary",)),
        name="inproj",
    )(x2, g, w_bf)


def _sb_kernel(q_ref, k_ref, v_ref, o_ref, acc_ref, r_ref, z_scr, lat_scr, *, tq, tk):
    i = pl.program_id(2)
    n_chunks = tq // tk
    lane = lax.broadcasted_iota(jnp.int32, (1, LANES), 1)
    head_mask = (lane < HEAD_DIM, lane >= HEAD_DIM)
    kk = lax.broadcasted_iota(jnp.int32, (tk, tk), 0)
    ss = lax.broadcasted_iota(jnp.int32, (tk, tk), 1)
    stri = jnp.where(kk > ss, 1.0, 0.0).astype(BF16)
    causal = ss < kk
    sign = jnp.uint32(0x80000000)

    acc_ref[...] = jnp.zeros_like(acc_ref)
    r_ref[...] = jnp.zeros_like(r_ref)

    def rows(c):
        return slice(c * tk, (c + 1) * tk)

    def both_heads(x):
        zero = jnp.zeros_like(x)
        return jnp.concatenate([jnp.where(head_mask[0], x, zero), jnp.where(head_mask[1], x, zero)], axis=0)

    def logits(j, chunks):
        k2 = both_heads(k_ref[0, pl.ds(pl.multiple_of(j * tk, tk), tk), :])
        for c in chunks:
            z_scr[c] = lax.dot_general(q_ref[0, rows(c), :], k2, NT_DIMS, preferred_element_type=F32)

    def later_sums(chunks, diag_chunk):
        for c in chunks:
            for h in range(2):
                cols = slice(h * tk, (h + 1) * tk)
                z = z_scr[c, :, cols]
                neg_abs = lax.bitcast_convert_type(lax.bitcast_convert_type(z, jnp.uint32) | sign, F32)
                sp = jnp.maximum(z, 0.0) + jnp.log2(1.0 + jnp.exp2(neg_abs))
                if c == diag_chunk:
                    sp = jnp.where(causal, sp, 0.0)
                z_scr[c, :, cols] = z - sp
                r = r_ref[h, rows(c), :]
                later = jnp.concatenate([r] * (tk // LANES), axis=1) + jnp.dot(
                    sp.astype(BF16), stri, preferred_element_type=F32)
                lat_scr[c, :, cols] = later
                r_ref[h, rows(c), :] = jnp.broadcast_to(later[:, 0:1] + sp[:, 0:1], (tk, LANES))

    def weigh(j, chunks, diag_chunk):
        v2 = both_heads(v_ref[0, pl.ds(pl.multiple_of(j * tk, tk), tk), :])
        for c in chunks:
            ws = []
            for h in range(2):
                cols = slice(h * tk, (h + 1) * tk)
                logw = z_scr[c, :, cols] - lat_scr[c, :, cols]
                if c == diag_chunk:
                    logw = jnp.where(causal, logw, NEG_INF)
                ws.append(jnp.exp2(logw).astype(BF16))
            acc_ref[rows(c), :] += jnp.dot(jnp.concatenate(ws, axis=1), v2, preferred_element_type=F32)

    all_chunks = list(range(n_chunks))
    base = i * n_chunks
    for d in range(n_chunks - 1, -1, -1):
        chunks = list(range(d, n_chunks))
        logits(base + d, chunks)
        later_sums(chunks, d)
        weigh(base + d, chunks, d)

    def body(jj, carry):
        j = base - 1 - jj
        logits(j, all_chunks)
        later_sums(all_chunks, None)
        weigh(j, all_chunks, None)
        return carry

    lax.fori_loop(0, base, body, 0)
    o_ref[0] = acc_ref[...].astype(o_ref.dtype)


def _sb_attention(proj3, tq, tk):
    b, s, _ = proj3.shape
    n_pairs = ATT_WIDTH // LANES
    kern = functools.partial(_sb_kernel, tq=tq, tk=tk)
    return pl.pallas_call(
        kern,
        grid=(b, n_pairs, s // tq),
        in_specs=[
            pl.BlockSpec((1, tq, LANES), lambda bi, p, i: (bi, i, p)),
            pl.BlockSpec((1, s, LANES), lambda bi, p, i: (bi, 0, n_pairs + p)),
            pl.BlockSpec((1, s, LANES), lambda bi, p, i: (bi, 0, 2 * n_pairs + p)),
        ],
        out_specs=pl.BlockSpec((1, tq, LANES), lambda bi, p, i: (bi, i, p)),
        out_shape=jax.ShapeDtypeStruct((b, s, ATT_WIDTH), BF16),
        scratch_shapes=[
            pltpu.VMEM((tq, LANES), F32),
            pltpu.VMEM((2, tq, LANES), F32),
            pltpu.VMEM((tq // tk, tk, 2 * tk), F32),
            pltpu.VMEM((tq // tk, tk, 2 * tk), F32),
        ],
        compiler_params=_params(("arbitrary", "arbitrary", "arbitrary")),
        name="sb_attention",
    )(proj3, proj3, proj3)


def _ca_kernel(q_ref, k_ref, v_ref, ring_ref, o_ref, z_scr, p_scr, b_scr, *, tq, win, rc):
    i = pl.program_id(2)
    ws = pl.multiple_of(jnp.maximum(i * tq - CA_PAD, 0), tq)

    @pl.when(i <= 2)
    def _():
        off = i * tq - ws
        r = lax.broadcasted_iota(jnp.int32, (tq, win), 0)
        c = lax.broadcasted_iota(jnp.int32, (tq, win), 1)
        relc = ((off + r) >> 6) - (c >> 6)
        valid = (relc >= 0) & (relc <= CA_LEFT_CHUNKS)
        for h in range(2):
            ring = jnp.broadcast_to(ring_ref[0, 0, h:h + 1, :], (tq, ring_ref.shape[-1]))
            toep = pltpu.roll(ring, 0, 1, stride=1, stride_axis=0)
            b_scr[h] = jnp.where(valid, toep[:, :win], NEG_INF)

    lane = lax.broadcasted_iota(jnp.int32, (1, LANES), 1)
    head_mask = (lane < HEAD_DIM, lane >= HEAD_DIM)
    k = k_ref[0, pl.ds(ws, win), :]
    v = v_ref[0, pl.ds(ws, win), :]
    zero = jnp.zeros_like(k)
    k2 = jnp.concatenate([jnp.where(head_mask[0], k, zero), jnp.where(head_mask[1], k, zero)], axis=0)
    z_scr[...] = lax.dot_general(q_ref[0], k2, NT_DIMS, preferred_element_type=F32)
    inv = [[None] * (tq // rc) for _ in range(2)]
    for h in range(2):
        for c in range(tq // rc):
            rows = slice(c * rc, (c + 1) * rc)
            cols = slice(h * win, (h + 1) * win)
            z = z_scr[rows, cols] + b_scr[h, rows, :]
            p = jnp.exp2(z - jnp.max(z, axis=1, keepdims=True))
            inv[h][c] = 1.0 / jnp.sum(p, axis=1, keepdims=True)
            p_scr[rows, cols] = p.astype(BF16)
    outs = [jnp.dot(p_scr[:, h * win:(h + 1) * win], jnp.where(head_mask[h], v, zero),
                    preferred_element_type=F32) for h in range(2)]
    o_ref[0] = (outs[0] * jnp.concatenate(inv[0], axis=0)
                + outs[1] * jnp.concatenate(inv[1], axis=0)).astype(o_ref.dtype)


def _ca_rings(rel_table, tq, win):
    ring_len = tq + win
    out = []
    for var in range(3):
        off = var * tq - max(var * tq - CA_PAD, 0)
        dist = np.concatenate([off - np.arange(win), [0], off + np.arange(tq - 1, 0, -1)])
        out.append(rel_table.astype(F32)[:, np.clip(dist, -REL_CLIP, REL_CLIP) + REL_CLIP])
    rings = jnp.stack(out) * math.log2(math.e)
    return rings.reshape(3, rel_table.shape[0] // 2, 2, ring_len)


def _ca_attention(proj3, rings, tq, win, rc):
    b, s, _ = proj3.shape
    n_pairs = ATT_WIDTH // LANES
    assert CHUNK == 64 and (tq + win) % LANES == 0
    kern = functools.partial(_ca_kernel, tq=tq, win=win, rc=rc)
    return pl.pallas_call(
        kern,
        grid=(b, n_pairs, s // tq),
        in_specs=[
            pl.BlockSpec((1, tq, LANES), lambda bi, p, i: (bi, i, 3 * n_pairs + p)),
            pl.BlockSpec((1, s, LANES), lambda bi, p, i: (bi, 0, 4 * n_pairs + p)),
            pl.BlockSpec((1, s, LANES), lambda bi, p, i: (bi, 0, 5 * n_pairs + p)),
            pl.BlockSpec((1, 1, 2, tq + win), lambda bi, p, i: (jnp.minimum(i, 2), p, 0, 0)),
        ],
        out_specs=pl.BlockSpec((1, tq, LANES), lambda bi, p, i: (bi, i, p)),
        out_shape=jax.ShapeDtypeStruct((b, s, ATT_WIDTH), BF16),
        scratch_shapes=[pltpu.VMEM((tq, 2 * win), F32), pltpu.VMEM((tq, 2 * win), BF16),
                        pltpu.VMEM((2, tq, win), F32)],
        compiler_params=_params(("arbitrary", "arbitrary", "arbitrary")),
        name="ca_attention",
    )(proj3, proj3, proj3, rings)


def _merge_kernel(osb_ref, oca_ref, gsb_ref, gca_ref, x_ref, psb_ref, pca_ref, wout_ref, nf_ref,
                  x1_ref, h2_ref, h2c_ref):
    br_sb = jnp.dot(osb_ref[...], psb_ref[...], preferred_element_type=F32)
    br_ca = jnp.dot(oca_ref[...], pca_ref[...], preferred_element_type=F32)
    merged = (jax.nn.sigmoid(gsb_ref[...].astype(F32)) * br_sb
              + jax.nn.sigmoid(gca_ref[...].astype(F32)) * br_ca)
    x1 = x_ref[...] + jnp.dot(merged.astype(BF16), wout_ref[...], preferred_element_type=F32)
    x1_ref[...] = x1
    ms = jnp.mean(x1 * x1, axis=-1, keepdims=True)
    h2 = x1 * lax.rsqrt(ms + RMS_EPS) * nf_ref[...]
    h2_ref[...] = h2
    tm = h2.shape[0]
    for j in range(SUBLANES):
        h2c_ref[pl.ds(j, tm, stride=SUBLANES), :] = h2[:, j * LANES:(j + 1) * LANES]


def _merge(o_sb, o_ca, proj, x2, psb, pca, wout, nf, tm):
    n = x2.shape[0]
    gate_blk = 3 * ATT_WIDTH * 2 // D_MODEL
    return pl.pallas_call(
        _merge_kernel,
        grid=(n // tm,),
        in_specs=[
            pl.BlockSpec((tm, ATT_WIDTH), lambda i: (i, 0)),
            pl.BlockSpec((tm, ATT_WIDTH), lambda i: (i, 0)),
            pl.BlockSpec((tm, D_MODEL), lambda i: (i, gate_blk)),
            pl.BlockSpec((tm, D_MODEL), lambda i: (i, gate_blk + 1)),
            pl.BlockSpec((tm, D_MODEL), lambda i: (i, 0)),
            pl.BlockSpec((ATT_WIDTH, D_MODEL), lambda i: (0, 0)),
            pl.BlockSpec((ATT_WIDTH, D_MODEL), lambda i: (0, 0)),
            pl.BlockSpec((D_MODEL, D_MODEL), lambda i: (0, 0)),
            pl.BlockSpec((1, D_MODEL), lambda i: (0, 0)),
        ],
        out_specs=[
            pl.BlockSpec((tm, D_MODEL), lambda i: (i, 0)),
            pl.BlockSpec((tm, D_MODEL), lambda i: (i, 0)),
            pl.BlockSpec((tm * SUBLANES, LANES), lambda i: (i, 0)),
        ],
        out_shape=[jax.ShapeDtypeStruct((n, D_MODEL), F32), jax.ShapeDtypeStruct((n, D_MODEL), F32),
                   jax.ShapeDtypeStruct((n * SUBLANES, LANES), F32)],
        compiler_params=_params(("arbitrary",)),
        name="merge",
    )(o_sb, o_ca, proj, proj, x2, psb, pca, wout, nf)


def _peer_candidates():
    return [(a, b) for a in range(PEER_TOPK) for b in range(PEER_TOPK) if (a + 1) * (b + 1) <= PEER_TOPK]


def _extract_top(s_scr, iota_scr, n_rounds, sentinel):
    vals, ids = [], []
    for _ in range(n_rounds):
        s = s_scr[...]
        m = jnp.max(s, axis=0, keepdims=True)
        cand = jnp.where(s == m, iota_scr[...], sentinel)
        am = jnp.min(cand, axis=0, keepdims=True)
        vals.append(m)
        ids.append(am)
        s_scr[...] = jnp.where(cand == am, -jnp.inf, s)
    return vals, ids


ROUTE_TOKENS = 128


def _route_scratch():
    n_rows = -(-len(_peer_candidates()) // SUBLANES) * SUBLANES
    return [
        pltpu.VMEM((n_rows, ROUTE_TOKENS), F32),
        pltpu.VMEM((n_rows, ROUTE_TOKENS), F32),
        pltpu.VMEM((PEER_SLOTS, ROUTE_TOKENS), F32),
        pltpu.VMEM((PEER_SLOTS, ROUTE_TOKENS), F32),
        pltpu.VMEM((2, PEER_N_KEYS, ROUTE_TOKENS), F32),
        pltpu.VMEM((PEER_N_KEYS, ROUTE_TOKENS), F32),
    ]


def _route_init(scr):
    cv_scr, ci_scr, _, _, _, iota_scr = scr
    n_cand = len(_peer_candidates())
    n_rows = cv_scr.shape[0]

    @pl.when(pl.program_id(0) == 0)
    def _():
        cv_scr[n_cand:, :] = jnp.full((n_rows - n_cand, ROUTE_TOKENS), -jnp.inf, F32)
        ci_scr[n_cand:, :] = jnp.zeros((n_rows - n_cand, ROUTE_TOKENS), F32)
        iota_scr[...] = lax.broadcasted_iota(jnp.int32, iota_scr.shape, 0).astype(F32)


def _slot_row(hd, r):
    if isinstance(hd, int):
        return slice(hd * PEER_TOPK + r, hd * PEER_TOPK + r + 1)
    return pl.ds(hd * PEER_TOPK + r, 1)


def _route_head(h_ref, wq_ref, keys_ref, hd, scr):
    cv_scr, ci_scr, ri_scr, rg_scr, s_scr, iota_scr = scr
    cands = _peer_candidates()
    n_rows = cv_scr.shape[0]
    h = h_ref[...].astype(BF16)
    iota_c = lax.broadcasted_iota(jnp.int32, (n_rows, ROUTE_TOKENS), 0).astype(F32)
    for p in range(2):
        hp = hd * 2 + p
        q = jnp.dot(h, wq_ref[hp], preferred_element_type=F32).astype(BF16)
        s_scr[p] = lax.dot_general(keys_ref[hp], q, NT_DIMS, preferred_element_type=F32)
    tops = [_extract_top(s_scr.at[p], iota_scr, PEER_TOPK, float(PEER_N_KEYS)) for p in range(2)]
    (v1, i1), (v2, i2) = tops
    i1s = [x * float(PEER_N_KEYS * PACK_ROWS) for x in i1]
    i2s = [x * float(PACK_ROWS) for x in i2]
    for r, (a, b) in enumerate(cands):
        cv_scr[r:r + 1, :] = v1[a] + v2[b]
        ci_scr[r:r + 1, :] = i1s[a] + i2s[b]
    cv = cv_scr[...]
    ci = ci_scr[...]
    best = []
    for r in range(PEER_TOPK):
        m = jnp.max(cv, axis=0, keepdims=True)
        cand = jnp.where(cv == m, iota_c, float(n_rows))
        hit = cand == jnp.min(cand, axis=0, keepdims=True)
        e = jnp.sum(jnp.where(hit, ci, 0.0), axis=0, keepdims=True)
        cv = jnp.where(hit, -jnp.inf, cv)
        best.append(m)
        ri_scr[_slot_row(hd, r), :] = e
    ex = [jnp.exp(bv - best[0]) for bv in best]
    tot = ex[0]
    for t in ex[1:]:
        tot = tot + t
    inv = 1.0 / tot
    for r in range(PEER_TOPK):
        rg_scr[_slot_row(hd, r), :] = ex[r] * inv


def _route_emit(idx_ref, g_ref, ri_scr, rg_scr):
    idx_ref[...] = ri_scr[...].T.astype(jnp.int32)
    g_ref[...] = rg_scr[...].T


def _route_kernel(h_ref, wq_ref, keys_ref, idx_ref, g_ref, *scr):
    _route_init(scr)
    for hd in range(PEER_HEADS):
        _route_head(h_ref, wq_ref, keys_ref, hd, scr)
    _route_emit(idx_ref, g_ref, scr[2], scr[3])


def _route_in_specs(block_index):
    return [
        pl.BlockSpec((ROUTE_TOKENS, D_MODEL), block_index),
        pl.BlockSpec((2 * PEER_HEADS, D_MODEL, PEER_HALF), lambda i: (0, 0, 0)),
        pl.BlockSpec((2 * PEER_HEADS, PEER_N_KEYS, PEER_HALF), lambda i: (0, 0, 0)),
    ]


def _route_out(n, block_index):
    specs = [pl.BlockSpec((ROUTE_TOKENS, PEER_SLOTS), block_index),
             pl.BlockSpec((ROUTE_TOKENS, PEER_SLOTS), block_index)]
    shapes = [jax.ShapeDtypeStruct((n, PEER_SLOTS), jnp.int32), jax.ShapeDtypeStruct((n, PEER_SLOTS), F32)]
    return specs, shapes


def _route(h2, wq3, keys_bf, n, tok_off=0):
    out_specs, out_shape = _route_out(n, lambda i: (i, 0))
    return pl.pallas_call(
        _route_kernel,
        grid=(n // ROUTE_TOKENS,),
        in_specs=_route_in_specs(lambda i: (i + tok_off // ROUTE_TOKENS, 0)),
        out_specs=out_specs,
        out_shape=out_shape,
        scratch_shapes=_route_scratch(),
        compiler_params=_params(("arbitrary",)),
        name="peer_route",
    )(h2, wq3, keys_bf)


def _pack_kernel(t_ref, o_ref, *row_major_ref):
    te = t_ref.shape[0]
    for r in range(PACK_ROWS):
        lo = t_ref[:, (2 * r) * LANES:(2 * r + 1) * LANES].astype(BF16).astype(F32)
        hi = t_ref[:, (2 * r + 1) * LANES:(2 * r + 2) * LANES].astype(BF16).astype(F32)
        word = lax.bitcast_convert_type(hi, jnp.uint32) | (lax.bitcast_convert_type(lo, jnp.uint32) >> 16)
        o_ref[pl.ds(r, te, stride=PACK_ROWS), :] = word
        for ref in row_major_ref:
            ref[:, r * LANES:(r + 1) * LANES] = word


def _pack_table(t, with_row_major=False):
    e = t.shape[0]
    te = _row_tile(e, 512)
    out_specs = [pl.BlockSpec((te * PACK_ROWS, LANES), lambda i: (i, 0))]
    out_shape = [jax.ShapeDtypeStruct((e * PACK_ROWS, LANES), jnp.uint32)]
    if with_row_major:
        out_specs.append(pl.BlockSpec((te, PACK_ROWS * LANES), lambda i: (i, 0)))
        out_shape.append(jax.ShapeDtypeStruct((e, PACK_ROWS * LANES), jnp.uint32))
    out = pl.pallas_call(
        _pack_kernel,
        grid=(e // te,),
        in_specs=[pl.BlockSpec((te, D_MODEL), lambda i: (i, 0))],
        out_specs=out_specs,
        out_shape=out_shape,
        compiler_params=_params(("arbitrary",)),
        name="pack_table",
    )(t)
    return out if with_row_major else out[0]


def _table_spec(n_rows):
    return pl.BlockSpec((n_rows, LANES), lambda i: (0, 0), pipeline_mode=pl.Buffered(1))


def _gather_rows(idx_ref, t, tab_ref, buf):
    for k in range(PEER_SLOTS):
        e = pl.multiple_of(idx_ref[t, k], PACK_ROWS)
        buf[k * PACK_ROWS:(k + 1) * PACK_ROWS, :] = tab_ref[pl.ds(e, PACK_ROWS), :]


def _gathered_bf16(buf):
    return pltpu.bitcast(buf[...], BF16)


def _chunk_pattern():
    shape = (2 * SUBLANES, PEER_SLOTS * SUBLANES)
    row = lax.broadcasted_iota(jnp.int32, shape, 0)
    col = lax.broadcasted_iota(jnp.int32, shape, 1)
    return (row & (SUBLANES - 1)) == (col & (SUBLANES - 1))


def _token_rows(t):
    if isinstance(t, int):
        return slice(t * SUBLANES, (t + 1) * SUBLANES)
    return pl.ds(pl.multiple_of(t * SUBLANES, SUBLANES), SUBLANES)


def _load_token_chunks(ref, t):
    row = ref[pl.ds(t, 1), :]
    return jnp.concatenate([row[:, j * LANES:(j + 1) * LANES] for j in range(SUBLANES)], axis=0)


def _store_token_chunks(ref, t, val):
    ref[pl.ds(t, 1), :] = jnp.concatenate([val[j:j + 1, :] for j in range(SUBLANES)], axis=1)


def _split_bf16(x):
    hi = x.astype(BF16)
    return hi, (x - hi.astype(F32)).astype(BF16)


def _token_loop(idx_ref, tab_ref, bufs, tb, unroll, compute, side_work=None):
    _gather_rows(idx_ref, 0, tab_ref, bufs[0])

    def body(i, carry):
        if side_work is not None:
            side_work(i)
        for u in range(unroll):
            t = i * unroll + u
            _gather_rows(idx_ref, jnp.minimum(t + 1, tb - 1), tab_ref, bufs[(u + 1) % 2])
            compute(t, bufs[u % 2])
        return carry

    lax.fori_loop(0, tb // unroll, body, 0)


def _peer_dot_body(idx_ref, h_ref, g_ref, tab_ref, c_ref, buf0, buf1, part_scr, tb, unroll, side_work):
    pattern = _chunk_pattern()

    def compute(t, buf):
        h16 = jnp.concatenate(_split_bf16(h_ref[_token_rows(t), :]), axis=0)
        r = lax.dot_general(h16, _gathered_bf16(buf), NT_DIMS, preferred_element_type=F32)
        r = jnp.where(pattern, r, 0.0)
        part_scr[pl.ds(t, 1), :] = jnp.sum(r[:SUBLANES] + r[SUBLANES:], axis=0, keepdims=True)

    _token_loop(idx_ref, tab_ref, (buf0, buf1), tb, unroll, compute, side_work)

    n_part = PEER_SLOTS * SUBLANES
    slot_of = lax.broadcasted_iota(jnp.int32, (n_part, PEER_SLOTS), 0) // SUBLANES
    fold = jnp.where(slot_of == lax.broadcasted_iota(jnp.int32, (n_part, PEER_SLOTS), 1), 1.0, 0.0).astype(BF16)
    p_hi, p_lo = _split_bf16(part_scr[...])
    a = (jnp.dot(p_hi, fold, preferred_element_type=F32)
         + jnp.dot(p_lo, fold, preferred_element_type=F32))
    c_ref[...] = g_ref[...] * (0.5 * a * (1.0 + lax.erf(a * (1.0 / math.sqrt(2.0)))))


def _peer_dot_kernel(idx_ref, h_ref, g_ref, tab_ref, c_ref, buf0, buf1, part_scr, *, tb, unroll):
    _peer_dot_body(idx_ref, h_ref, g_ref, tab_ref, c_ref, buf0, buf1, part_scr, tb, unroll, None)


def _peer_dot_route_kernel(idx_ref, h_ref, g_ref, tab_ref, rh_ref, wq_ref, keys_ref,
                           c_ref, ridx_ref, rg_ref, buf0, buf1, part_scr, *route_scr, tb, unroll):
    trips = tb // unroll
    phase = pl.program_id(0) % (PEER_HEADS // trips)
    _route_init(route_scr)

    def side_work(i):
        _route_head(rh_ref, wq_ref, keys_ref, phase * trips + i, route_scr)

    _peer_dot_body(idx_ref, h_ref, g_ref, tab_ref, c_ref, buf0, buf1, part_scr, tb, unroll, side_work)

    @pl.when(phase == PEER_HEADS // trips - 1)
    def _():
        _route_emit(ridx_ref, rg_ref, route_scr[2], route_scr[3])


def _peer_dot_specs(tab, tb, h_block_off):
    in_specs = [
        pl.BlockSpec((tb, PEER_SLOTS), lambda i: (i, 0), memory_space=pltpu.SMEM),
        pl.BlockSpec((tb * SUBLANES, LANES), lambda i: (i + h_block_off, 0)),
        pl.BlockSpec((tb, PEER_SLOTS), lambda i: (i, 0)),
        _table_spec(tab.shape[0]),
    ]
    scratch = [pltpu.VMEM((PEER_SLOTS * PACK_ROWS, LANES), jnp.uint32),
               pltpu.VMEM((PEER_SLOTS * PACK_ROWS, LANES), jnp.uint32),
               pltpu.VMEM((tb, PEER_SLOTS * SUBLANES), F32)]
    return in_specs, scratch


def _peer_dot(idx, h2r, gate, tab, tb, unroll, tok_off=0):
    n = idx.shape[0]
    in_specs, scratch = _peer_dot_specs(tab, tb, tok_off // tb)
    return pl.pallas_call(
        functools.partial(_peer_dot_kernel, tb=tb, unroll=unroll),
        grid=(n // tb,),
        in_specs=in_specs,
        out_specs=pl.BlockSpec((tb, PEER_SLOTS), lambda i: (i, 0)),
        out_shape=jax.ShapeDtypeStruct((n, PEER_SLOTS), F32),
        scratch_shapes=scratch,
        compiler_params=_params(("arbitrary",)),
        name="peer_dot",
    )(idx, h2r, gate, tab)


def _peer_dot_route(idx, h2r, gate, tab, h2, wq3, keys_bf, tb, unroll, tok_off, route_off):
    n = idx.shape[0]
    assert PEER_HEADS % (tb // unroll) == 0
    steps_per_block = PEER_HEADS // (tb // unroll)
    assert steps_per_block * tb == ROUTE_TOKENS and route_off % ROUTE_TOKENS == 0
    in_specs, scratch = _peer_dot_specs(tab, tb, tok_off // tb)
    route_in_block = lambda i: (i // steps_per_block + route_off // ROUTE_TOKENS, 0)
    route_block = lambda i: (i // steps_per_block, 0)
    route_out_specs, route_out_shape = _route_out(n, route_block)
    return pl.pallas_call(
        functools.partial(_peer_dot_route_kernel, tb=tb, unroll=unroll),
        grid=(n // tb,),
        in_specs=in_specs + _route_in_specs(route_in_block),
        out_specs=[pl.BlockSpec((tb, PEER_SLOTS), lambda i: (i, 0))] + route_out_specs,
        out_shape=[jax.ShapeDtypeStruct((n, PEER_SLOTS), F32)] + route_out_shape,
        scratch_shapes=scratch + _route_scratch(),
        compiler_params=_params(("arbitrary",)),
        name="peer_dot_route",
    )(idx, h2r, gate, tab, h2, wq3, keys_bf)


def _peer_axpy_kernel(idx_ref, c_ref, x_ref, gf_ref, tab_ref, y_ref, buf0, buf1, cexp_scr, *, tb, unroll):
    pattern = _chunk_pattern()
    n_part = PEER_SLOTS * SUBLANES
    slot_of = lax.broadcasted_iota(jnp.int32, (PEER_SLOTS, n_part), 1) // SUBLANES
    spread = jnp.where(slot_of == lax.broadcasted_iota(jnp.int32, (PEER_SLOTS, n_part), 0), 1.0, 0.0).astype(BF16)
    c_hi, c_lo = _split_bf16(c_ref[...])
    cexp_scr[0:tb, :] = jnp.dot(c_hi, spread, preferred_element_type=F32)
    cexp_scr[tb:2 * tb, :] = jnp.dot(c_lo, spread, preferred_element_type=F32)

    def compute(t, buf):
        coef = jnp.concatenate([jnp.broadcast_to(cexp_scr[pl.ds(t, 1), :], (SUBLANES, n_part)),
                                jnp.broadcast_to(cexp_scr[pl.ds(tb + t, 1), :], (SUBLANES, n_part))], axis=0)
        coef = jnp.where(pattern, coef, 0.0).astype(BF16)
        o16 = jnp.dot(coef, _gathered_bf16(buf), preferred_element_type=F32)
        x2 = _load_token_chunks(x_ref, t) + (o16[:SUBLANES] + o16[SUBLANES:])
        ss = jnp.sum(jnp.sum(x2 * x2, axis=1, keepdims=True), axis=0, keepdims=True)
        _store_token_chunks(y_ref, t, x2 * lax.rsqrt(ss * (1.0 / D_MODEL) + RMS_EPS) * gf_ref[...])

    _token_loop(idx_ref, tab_ref, (buf0, buf1), tb, unroll, compute)


def _peer_axpy(idx, c, x1r, gfr, tab, tb, unroll, n_tokens, n):
    kern = functools.partial(_peer_axpy_kernel, tb=tb, unroll=unroll)
    return pl.pallas_call(
        kern,
        grid=(n_tokens // tb,),
        in_specs=[
            pl.BlockSpec((tb, PEER_SLOTS), lambda i: (i, 0), memory_space=pltpu.SMEM),
            pl.BlockSpec((tb, PEER_SLOTS), lambda i: (i, 0)),
            pl.BlockSpec((tb, D_MODEL), lambda i: (i, 0)),
            pl.BlockSpec((SUBLANES, LANES), lambda i: (0, 0)),
            _table_spec(tab.shape[0]),
        ],
        out_specs=pl.BlockSpec((tb, D_MODEL), lambda i: (i, 0)),
        out_shape=jax.ShapeDtypeStruct((n, D_MODEL), F32),
        scratch_shapes=[pltpu.VMEM((PEER_SLOTS * PACK_ROWS, LANES), jnp.uint32),
                        pltpu.VMEM((PEER_SLOTS * PACK_ROWS, LANES), jnp.uint32),
                        pltpu.VMEM((2 * tb, PEER_SLOTS * SUBLANES), F32)],
        compiler_params=_params(("arbitrary",)),
        name="peer_axpy",
    )(idx, c, x1r, gfr, tab)


SC_WORKERS = 32
SC_LANES = 16
SC_CHUNK_TOKENS = 16
SC_HALF_ROWS = PEER_SLOTS // 2
SC_ROW_BLOCK = 8
SC_COL_GROUPS = LANES // SC_LANES
SC_SHARE_NUM, SC_SHARE_DEN = 17, 64


def _peer_axpy_sc(expert, c, tab_rows):
    n = expert.shape[0]
    per = n // SC_WORKERS
    assert n % (SC_WORKERS * SC_CHUNK_TOKENS) == 0
    mesh = plsc.VectorSubcoreMesh(core_axis_name="c", subcore_axis_name="s")
    params = pltpu.CompilerParams()
    if "needs_layout_passes" in pltpu.CompilerParams.__dataclass_fields__:
        params = dataclasses.replace(params, needs_layout_passes=False)
    expert2 = expert.reshape(n * 2, SC_HALF_ROWS)
    words = PACK_ROWS * LANES

    @functools.partial(
        pl.kernel, mesh=mesh, compiler_params=params,
        out_type=jax.ShapeDtypeStruct((n, D_MODEL), F32),
        scratch_types=[pltpu.VMEM((2 * SC_CHUNK_TOKENS, SC_HALF_ROWS), jnp.int32),
                       pltpu.VMEM((SC_CHUNK_TOKENS, PEER_SLOTS), F32),
                       pltpu.VMEM((SC_HALF_ROWS, words), jnp.uint32),
                       pltpu.VMEM((SC_HALF_ROWS, words), jnp.uint32),
                       pltpu.VMEM((D_MODEL,), F32),
                       pltpu.SemaphoreType.DMA, pltpu.SemaphoreType.DMA],
    )
    def sc_kernel(idx_hbm, c_hbm, tab_hbm, out_hbm, idx_v, c_v, rows0, rows1, out_v, sem0, sem1):
        wid = lax.axis_index("s") * 2 + lax.axis_index("c")
        bufs = ((rows0, sem0), (rows1, sem1))

        def gather(u, b):
            return pltpu.make_async_copy(tab_hbm.at[idx_v.at[u]], bufs[b][0], bufs[b][1])

        def accumulate(tl, half, b):
            rows = bufs[b][0]

            @pl.loop(0, PACK_ROWS)
            def _(cb):
                @pl.loop(0, SC_HALF_ROWS // SC_ROW_BLOCK)
                def _(rb):
                    acc = [None] * (2 * SC_COL_GROUPS)
                    for r in range(SC_ROW_BLOCK):
                        kk = rb * SC_ROW_BLOCK + r
                        zero = jnp.zeros((SC_LANES,), jnp.int32)
                        ck = plsc.load_gather(c_v, [zero + tl, zero + (half * SC_HALF_ROWS + kk)])
                        for g in range(SC_COL_GROUPS):
                            w = rows[kk, pl.ds(cb * LANES + g * SC_LANES, SC_LANES)]
                            hi = ck * lax.bitcast_convert_type(w & jnp.uint32(0xFFFF0000), F32)
                            lo = ck * lax.bitcast_convert_type(w << 16, F32)
                            acc[2 * g] = hi if acc[2 * g] is None else acc[2 * g] + hi
                            acc[2 * g + 1] = lo if acc[2 * g + 1] is None else acc[2 * g + 1] + lo
                    for g in range(SC_COL_GROUPS):
                        plsc.addupdate(out_v.at[pl.ds((2 * cb + 1) * LANES + g * SC_LANES, SC_LANES)], acc[2 * g])
                        plsc.addupdate(out_v.at[pl.ds(2 * cb * LANES + g * SC_LANES, SC_LANES)], acc[2 * g + 1])

        @pl.loop(0, per // SC_CHUNK_TOKENS)
        def _(ch):
            t0 = wid * per + ch * SC_CHUNK_TOKENS
            pltpu.sync_copy(idx_hbm.at[pl.ds(2 * t0, 2 * SC_CHUNK_TOKENS)], idx_v)
            pltpu.sync_copy(c_hbm.at[pl.ds(t0, SC_CHUNK_TOKENS)], c_v)
            gather(0, 0).start()

            @pl.loop(0, SC_CHUNK_TOKENS)
            def _(tl):
                for g in range(D_MODEL // SC_LANES):
                    out_v[pl.ds(g * SC_LANES, SC_LANES)] = jnp.zeros((SC_LANES,), F32)
                gather(2 * tl + 1, 1).start()
                gather(2 * tl, 0).wait()
                accumulate(tl, 0, 0)

                @pl.when(tl + 1 < SC_CHUNK_TOKENS)
                def _():
                    gather(2 * tl + 2, 0).start()

                gather(2 * tl + 1, 1).wait()
                accumulate(tl, 1, 1)
                pltpu.sync_copy(out_v, out_hbm.at[t0 + tl])

    return sc_kernel(expert2, c, tab_rows)


def _peer_dot_sc(expert, h2, tab_rows, tok_off):
    n = expert.shape[0]
    per = n // SC_WORKERS
    assert n % (SC_WORKERS * SC_CHUNK_TOKENS) == 0 and PEER_SLOTS % SC_LANES == 0
    mesh = plsc.VectorSubcoreMesh(core_axis_name="c", subcore_axis_name="s")
    params = pltpu.CompilerParams()
    if "needs_layout_passes" in pltpu.CompilerParams.__dataclass_fields__:
        params = dataclasses.replace(params, needs_layout_passes=False)
    expert2 = expert.reshape(n * 2, SC_HALF_ROWS)
    words = PACK_ROWS * LANES
    row_block = SC_LANES

    @functools.partial(
        pl.kernel, mesh=mesh, compiler_params=params,
        out_type=jax.ShapeDtypeStruct((n, PEER_SLOTS), F32),
        scratch_types=[pltpu.VMEM((2 * SC_CHUNK_TOKENS, SC_HALF_ROWS), jnp.int32),
                       pltpu.VMEM((SC_CHUNK_TOKENS, D_MODEL), F32),
                       pltpu.VMEM((SC_HALF_ROWS, words), jnp.uint32),
                       pltpu.VMEM((SC_HALF_ROWS, words), jnp.uint32),
                       pltpu.VMEM((PEER_SLOTS,), F32),
                       pltpu.SemaphoreType.DMA, pltpu.SemaphoreType.DMA],
    )
    def sc_kernel(idx_hbm, h_hbm, tab_hbm, out_hbm, idx_v, h_v, rows0, rows1, a_v, sem0, sem1):
        wid = lax.axis_index("s") * 2 + lax.axis_index("c")
        bufs = ((rows0, sem0), (rows1, sem1))
        lane = lax.iota(jnp.int32, SC_LANES)

        def gather(u, b):
            return pltpu.make_async_copy(tab_hbm.at[idx_v.at[u]], bufs[b][0], bufs[b][1])

        def accumulate(tl, half, b):
            rows = bufs[b][0]

            @pl.loop(0, PACK_ROWS)
            def _(cb):
                h_hi = [h_v[tl, pl.ds((2 * cb + 1) * LANES + g * SC_LANES, SC_LANES)] for g in range(SC_COL_GROUPS)]
                h_lo = [h_v[tl, pl.ds(2 * cb * LANES + g * SC_LANES, SC_LANES)] for g in range(SC_COL_GROUPS)]

                @pl.loop(0, SC_HALF_ROWS // row_block)
                def _(rb):
                    res = jnp.zeros((SC_LANES,), F32)
                    for r in range(row_block):
                        kk = rb * row_block + r
                        acc = None
                        for g in range(SC_COL_GROUPS):
                            w = rows[kk, pl.ds(cb * LANES + g * SC_LANES, SC_LANES)]
                            hi = h_hi[g] * lax.bitcast_convert_type(w & jnp.uint32(0xFFFF0000), F32)
                            lo = h_lo[g] * lax.bitcast_convert_type(w << 16, F32)
                            acc = hi + lo if acc is None else acc + hi + lo
                        res = jnp.where(lane == r, jnp.sum(acc), res)
                    plsc.addupdate(a_v.at[pl.ds(half * SC_HALF_ROWS + rb * row_block, SC_LANES)], res)

        @pl.loop(0, per // SC_CHUNK_TOKENS)
        def _(ch):
            t0 = wid * per + ch * SC_CHUNK_TOKENS
            pltpu.sync_copy(idx_hbm.at[pl.ds(2 * t0, 2 * SC_CHUNK_TOKENS)], idx_v)
            pltpu.sync_copy(h_hbm.at[pl.ds(tok_off + t0, SC_CHUNK_TOKENS)], h_v)
            gather(0, 0).start()

            @pl.loop(0, SC_CHUNK_TOKENS)
            def _(tl):
                for g in range(PEER_SLOTS // SC_LANES):
                    a_v[pl.ds(g * SC_LANES, SC_LANES)] = jnp.zeros((SC_LANES,), F32)
                gather(2 * tl + 1, 1).start()
                gather(2 * tl, 0).wait()
                accumulate(tl, 0, 0)

                @pl.when(tl + 1 < SC_CHUNK_TOKENS)
                def _():
                    gather(2 * tl + 2, 0).start()

                gather(2 * tl + 1, 1).wait()
                accumulate(tl, 1, 1)
                pltpu.sync_copy(a_v, out_hbm.at[t0 + tl])

    return sc_kernel(expert2, h2, tab_rows)


def _gelu_gate_kernel(a_ref, g_ref, c_ref):
    a = a_ref[...]
    c_ref[...] = g_ref[...] * (0.5 * a * (1.0 + lax.erf(a * (1.0 / math.sqrt(2.0)))))


def _gelu_gate(a, gate):
    n = a.shape[0]
    tm = _row_tile(n, 1024)
    spec = pl.BlockSpec((tm, PEER_SLOTS), lambda i: (i, 0))
    return pl.pallas_call(
        _gelu_gate_kernel, grid=(n // tm,), in_specs=[spec, spec], out_specs=spec,
        out_shape=jax.ShapeDtypeStruct((n, PEER_SLOTS), F32),
        compiler_params=_params(("arbitrary",)), name="gelu_gate",
    )(a, gate)


def _finish_kernel(y_hbm, x_ref, p_ref, gf_ref, o_ref):
    del y_hbm
    x2 = x_ref[...] + p_ref[...]
    ms = jnp.mean(x2 * x2, axis=-1, keepdims=True)
    o_ref[...] = x2 * lax.rsqrt(ms + RMS_EPS) * gf_ref[...]


def _peer_finish(y, x1, p_tail, gf, tm):
    n = y.shape[0]
    n_tail = p_tail.shape[0]
    off = (n - n_tail) // tm
    assert (n - n_tail) % tm == 0 and n_tail % tm == 0
    return pl.pallas_call(
        _finish_kernel,
        grid=(n_tail // tm,),
        in_specs=[
            pl.BlockSpec(memory_space=pl.ANY),
            pl.BlockSpec((tm, D_MODEL), lambda i: (i + off, 0)),
            pl.BlockSpec((tm, D_MODEL), lambda i: (i, 0)),
            pl.BlockSpec((1, D_MODEL), lambda i: (0, 0)),
        ],
        out_specs=pl.BlockSpec((tm, D_MODEL), lambda i: (i + off, 0)),
        out_shape=jax.ShapeDtypeStruct((n, D_MODEL), F32),
        input_output_aliases={0: 0},
        compiler_params=_params(("arbitrary",)),
        name="peer_finish",
    )(y, x1, p_tail, gf)


def _row_tile(n, want):
    t = min(want, n)
    assert n % t == 0
    return t


def _layer(x2, b, s, norm_mix, w_in, w_sb_proj, w_ca_proj, ca_rel_bias, w_out, norm_ffn,
           peer_w_query, peer_sub_keys, peer_u, peer_v, final_gain):
    n = b * s
    col = np.arange(IN_COLS)
    is_q = (col < ATT_WIDTH) | ((col >= 3 * ATT_WIDTH) & (col < 4 * ATT_WIDTH))
    col_scale = np.where(is_q, math.log2(math.e) / math.sqrt(HEAD_DIM), 1.0)
    w_in_bf = (w_in * jnp.asarray(col_scale, F32)[None, :]).astype(BF16)

    proj = _inproj(x2, norm_mix.reshape(1, D_MODEL), w_in_bf, _row_tile(n, 512))
    proj3 = proj.reshape(b, s, IN_COLS)

    tq = 256
    assert s % tq == 0 and s >= tq + CA_PAD
    sb_tq = min(2048, s)
    assert s % sb_tq == 0
    o_sb = _sb_attention(proj3, sb_tq, 256)
    win = tq + CA_PAD
    o_ca = _ca_attention(proj3, _ca_rings(ca_rel_bias, tq, win), tq, win, 64)

    x1, h2, h2r = _merge(o_sb.reshape(n, ATT_WIDTH), o_ca.reshape(n, ATT_WIDTH), proj, x2,
                    w_sb_proj.astype(BF16), w_ca_proj.astype(BF16), w_out.astype(BF16),
                    norm_ffn.reshape(1, D_MODEL), _row_tile(n, 512))

    keys_bf = peer_sub_keys.reshape(2 * PEER_HEADS, PEER_N_KEYS, PEER_HALF).astype(BF16)
    wq3 = jnp.transpose(peer_w_query.astype(BF16).reshape(D_MODEL, 2 * PEER_HEADS, PEER_HALF), (1, 0, 2))
    tab_u, tab_u_rows = _pack_table(peer_u, with_row_major=True)
    tbg, unroll = 64, 16
    sc_unit = SC_WORKERS * SC_CHUNK_TOKENS
    n_seg = b if (b > 1 and s % ROUTE_TOKENS == 0) else 1
    seg = n // n_seg
    assert seg % ROUTE_TOKENS == 0
    use_sc = seg % sc_unit == 0
    idx_j, gate_j = _route(h2, wq3, keys_bf, seg)
    idx_parts, c_parts = [], []
    for j in range(n_seg):
        idx_parts.append(idx_j)
        last = j + 1 == n_seg
        if j % 2 == 1 and use_sc:
            a_j = _peer_dot_sc(idx_j // PACK_ROWS, h2, tab_u_rows, j * seg)
            c_j = _gelu_gate(a_j, gate_j)
            if not last:
                idx_j, gate_j = _route(h2, wq3, keys_bf, seg, (j + 1) * seg)
        elif not last:
            c_j, idx_j, gate_j = _peer_dot_route(idx_j, h2r, gate_j, tab_u, h2, wq3, keys_bf, tbg, unroll,
                                                 j * seg, (j + 1) * seg)
        else:
            c_j = _peer_dot(idx_j, h2r, gate_j, tab_u, tbg, unroll, j * seg)
        c_parts.append(c_j)
    tab_v, tab_v_rows = _pack_table(peer_v, with_row_major=True)
    n_sc = (n * SC_SHARE_NUM // SC_SHARE_DEN) // sc_unit * sc_unit if use_sc else 0
    n_tc = n - n_sc
    head_segs = -(-n_tc // seg)
    idx_head = jnp.concatenate(idx_parts[:head_segs], axis=0)
    c_head = jnp.concatenate(c_parts[:head_segs], axis=0)
    y = _peer_axpy(idx_head, c_head, x1, final_gain.reshape(SUBLANES, LANES), tab_v, tbg, unroll, n_tc, n)
    if n_sc:
        first = n_tc // seg
        idx_tail = jnp.concatenate(idx_parts[first:], axis=0)[n_tc - first * seg:]
        c_tail = jnp.concatenate(c_parts[first:], axis=0)[n_tc - first * seg:]
        p_sc = _peer_axpy_sc(idx_tail // PACK_ROWS, c_tail, tab_v_rows)
        y = _peer_finish(y, x1, p_sc, final_gain.reshape(1, D_MODEL), sc_unit)
    return y


def kernel(x, norm_mix, w_in, w_sb_proj, w_ca_proj, ca_rel_bias, w_out, norm_ffn,
           peer_w_query, peer_sub_keys, peer_u, peer_v, norm_final):
    b, s, d = x.shape
    depth = norm_mix.shape[0]
    assert d == D_MODEL and depth == 1
    y = _layer(x.reshape(b * s, d), b, s, norm_mix[0], w_in[0], w_sb_proj[0], w_ca_proj[0],
               ca_rel_bias[0], w_out[0], norm_ffn[0], peer_w_query[0], peer_sub_keys[0],
               peer_u[0], peer_v[0], norm_final)
    return y.reshape(b, s, d)
```

```python
import dataclasses
import functools
import math

import numpy as np
import jax
import jax.numpy as jnp
from jax import lax
from jax.experimental import pallas as pl
from jax.experimental.pallas import tpu as pltpu
from jax.experimental.pallas import tpu_sc as plsc

D_MODEL = 1024
HEAD_DIM = 64
N_HEADS = 8
ATT_WIDTH = N_HEADS * HEAD_DIM
IN_COLS = 6 * ATT_WIDTH + 2 * D_MODEL
CHUNK = 64
CA_LEFT_CHUNKS = 8
CA_PAD = CA_LEFT_CHUNKS * CHUNK
REL_CLIP = 128
PEER_HEADS = 8
PEER_N_KEYS = 128
PEER_HALF = 128
PEER_TOPK = 16
PEER_SLOTS = PEER_HEADS * PEER_TOPK
RMS_EPS = 1e-6
NEG_INF = -1e30

LANES = 128
SUBLANES = 8
PACK_ROWS = D_MODEL // 2 // LANES

VMEM_LIMIT = 56 * 1024 * 1024

F32 = jnp.float32
BF16 = jnp.bfloat16
NT_DIMS = (((1,), (1,)), ((), ()))


def _params(sem, vmem=VMEM_LIMIT):
    return pltpu.CompilerParams(dimension_semantics=sem, vmem_limit_bytes=vmem)


def _inproj_kernel(x_ref, g_ref, w_ref, o_ref):
    x = x_ref[...]
    ms = jnp.mean(x * x, axis=-1, keepdims=True)
    h = (x * lax.rsqrt(ms + RMS_EPS) * g_ref[...]).astype(BF16)
    for c in range(IN_COLS // D_MODEL):
        cols = slice(c * D_MODEL, (c + 1) * D_MODEL)
        o_ref[:, cols] = jnp.dot(h, w_ref[:, cols], preferred_element_type=F32).astype(o_ref.dtype)


def _inproj(x2, g, w_bf, tm):
    n = x2.shape[0]
    return pl.pallas_call(
        _inproj_kernel,
        grid=(n // tm,),
        in_specs=[
            pl.BlockSpec((tm, D_MODEL), lambda i: (i, 0)),
            pl.BlockSpec((1, D_MODEL), lambda i: (0, 0)),
            pl.BlockSpec((D_MODEL, IN_COLS), lambda i: (0, 0)),
        ],
        out_specs=pl.BlockSpec((tm, IN_COLS), lambda i: (i, 0)),
        out_shape=jax.ShapeDtypeStruct((n, IN_COLS), BF16),
        compiler_params=_params(("arbitrary",)),
        name="inproj",
    )(x2, g, w_bf)


def _sb_kernel(q_ref, k_ref, v_ref, o_ref, acc_ref, r_ref, z_scr, lat_scr, *, tq, tk):
    i = pl.program_id(2)
    n_chunks = tq // tk
    lane = lax.broadcasted_iota(jnp.int32, (1, LANES), 1)
    head_mask = (lane < HEAD_DIM, lane >= HEAD_DIM)
    kk = lax.broadcasted_iota(jnp.int32, (tk, tk), 0)
    ss = lax.broadcasted_iota(jnp.int32, (tk, tk), 1)
    stri = jnp.where(kk > ss, 1.0, 0.0).astype(BF16)
    causal = ss < kk
    sign = jnp.uint32(0x80000000)

    acc_ref[...] = jnp.zeros_like(acc_ref)
    r_ref[...] = jnp.zeros_like(r_ref)

    def rows(c):
        return slice(c * tk, (c + 1) * tk)

    def both_heads(x):
        zero = jnp.zeros_like(x)
        return jnp.concatenate([jnp.where(head_mask[0], x, zero), jnp.where(head_mask[1], x, zero)], axis=0)

    def logits(j, chunks):
        k2 = both_heads(k_ref[0, pl.ds(pl.multiple_of(j * tk, tk), tk), :])
        for c in chunks:
            z_scr[c] = lax.dot_general(q_ref[0, rows(c), :], k2, NT_DIMS, preferred_element_type=F32)

    def later_sums(chunks, diag_chunk):
        for c in chunks:
            for h in range(2):
                cols = slice(h * tk, (h + 1) * tk)
                z = z_scr[c, :, cols]
                neg_abs = lax.bitcast_convert_type(lax.bitcast_convert_type(z, jnp.uint32) | sign, F32)
                sp = jnp.maximum(z, 0.0) + jnp.log2(1.0 + jnp.exp2(neg_abs))
                if c == diag_chunk:
                    sp = jnp.where(causal, sp, 0.0)
                z_scr[c, :, cols] = z - sp
                r = r_ref[h, rows(c), :]
                later = jnp.concatenate([r] * (tk // LANES), axis=1) + jnp.dot(
                    sp.astype(BF16), stri, preferred_element_type=F32)
                lat_scr[c, :, cols] = later
                r_ref[h, rows(c), :] = jnp.broadcast_to(later[:, 0:1] + sp[:, 0:1], (tk, LANES))

    def weigh(j, chunks, diag_chunk):
        v2 = both_heads(v_ref[0, pl.ds(pl.multiple_of(j * tk, tk), tk), :])
        for c in chunks:
            ws = []
            for h in range(2):
                cols = slice(h * tk, (h + 1) * tk)
                logw = z_scr[c, :, cols] - lat_scr[c, :, cols]
                if c == diag_chunk:
                    logw = jnp.where(causal, logw, NEG_INF)
                ws.append(jnp.exp2(logw).astype(BF16))
            acc_ref[rows(c), :] += jnp.dot(jnp.concatenate(ws, axis=1), v2, preferred_element_type=F32)

    all_chunks = list(range(n_chunks))
    base = i * n_chunks
    for d in range(n_chunks - 1, -1, -1):
        chunks = list(range(d, n_chunks))
        logits(base + d, chunks)
        later_sums(chunks, d)
        weigh(base + d, chunks, d)

    def body(jj, carry):
        j = base - 1 - jj
        logits(j, all_chunks)
        later_sums(all_chunks, None)
        weigh(j, all_chunks, None)
        return carry

    lax.fori_loop(0, base, body, 0)
    o_ref[0] = acc_ref[...].astype(o_ref.dtype)


def _sb_attention(proj3, tq, tk):
    b, s, _ = proj3.shape
    n_pairs = ATT_WIDTH // LANES
    kern = functools.partial(_sb_kernel, tq=tq, tk=tk)
    return pl.pallas_call(
        kern,
        grid=(b, n_pairs, s // tq),
        in_specs=[
            pl.BlockSpec((1, tq, LANES), lambda bi, p, i: (bi, i, p)),
            pl.BlockSpec((1, s, LANES), lambda bi, p, i: (bi, 0, n_pairs + p)),
            pl.BlockSpec((1, s, LANES), lambda bi, p, i: (bi, 0, 2 * n_pairs + p)),
        ],
        out_specs=pl.BlockSpec((1, tq, LANES), lambda bi, p, i: (bi, i, p)),
        out_shape=jax.ShapeDtypeStruct((b, s, ATT_WIDTH), BF16),
        scratch_shapes=[
            pltpu.VMEM((tq, LANES), F32),
            pltpu.VMEM((2, tq, LANES), F32),
            pltpu.VMEM((tq // tk, tk, 2 * tk), F32),
            pltpu.VMEM((tq // tk, tk, 2 * tk), F32),
        ],
        compiler_params=_params(("arbitrary", "arbitrary", "arbitrary")),
        name="sb_attention",
    )(proj3, proj3, proj3)


def _ca_kernel(q_ref, k_ref, v_ref, ring_ref, o_ref, z_scr, p_scr, b_scr, *, tq, win, rc):
    i = pl.program_id(2)
    ws = pl.multiple_of(jnp.maximum(i * tq - CA_PAD, 0), tq)

    @pl.when(i <= 2)
    def _():
        off = i * tq - ws
        r = lax.broadcasted_iota(jnp.int32, (tq, win), 0)
        c = lax.broadcasted_iota(jnp.int32, (tq, win), 1)
        relc = ((off + r) >> 6) - (c >> 6)
        valid = (relc >= 0) & (relc <= CA_LEFT_CHUNKS)
        for h in range(2):
            ring = jnp.broadcast_to(ring_ref[0, 0, h:h + 1, :], (tq, ring_ref.shape[-1]))
            toep = pltpu.roll(ring, 0, 1, stride=1, stride_axis=0)
            b_scr[h] = jnp.where(valid, toep[:, :win], NEG_INF)

    lane = lax.broadcasted_iota(jnp.int32, (1, LANES), 1)
    head_mask = (lane < HEAD_DIM, lane >= HEAD_DIM)
    k = k_ref[0, pl.ds(ws, win), :]
    v = v_ref[0, pl.ds(ws, win), :]
    zero = jnp.zeros_like(k)
    k2 = jnp.concatenate([jnp.where(head_mask[0], k, zero), jnp.where(head_mask[1], k, zero)], axis=0)
    z_scr[...] = lax.dot_general(q_ref[0], k2, NT_DIMS, preferred_element_type=F32)
    inv = [[None] * (tq // rc) for _ in range(2)]
    for h in range(2):
        for c in range(tq // rc):
            rows = slice(c * rc, (c + 1) * rc)
            cols = slice(h * win, (h + 1) * win)
            z = z_scr[rows, cols] + b_scr[h, rows, :]
            p = jnp.exp2(z - jnp.max(z, axis=1, keepdims=True))
            inv[h][c] = 1.0 / jnp.sum(p, axis=1, keepdims=True)
            p_scr[rows, cols] = p.astype(BF16)
    outs = [jnp.dot(p_scr[:, h * win:(h + 1) * win], jnp.where(head_mask[h], v, zero),
                    preferred_element_type=F32) for h in range(2)]
    o_ref[0] = (outs[0] * jnp.concatenate(inv[0], axis=0)
                + outs[1] * jnp.concatenate(inv[1], axis=0)).astype(o_ref.dtype)


def _ca_rings(rel_table, tq, win):
    ring_len = tq + win
    out = []
    for var in range(3):
        off = var * tq - max(var * tq - CA_PAD, 0)
        dist = np.concatenate([off - np.arange(win), [0], off + np.arange(tq - 1, 0, -1)])
        out.append(rel_table.astype(F32)[:, np.clip(dist, -REL_CLIP, REL_CLIP) + REL_CLIP])
    rings = jnp.stack(out) * math.log2(math.e)
    return rings.reshape(3, rel_table.shape[0] // 2, 2, ring_len)


def _ca_attention(proj3, rings, tq, win, rc):
    b, s, _ = proj3.shape
    n_pairs = ATT_WIDTH // LANES
    assert CHUNK == 64 and (tq + win) % LANES == 0
    kern = functools.partial(_ca_kernel, tq=tq, win=win, rc=rc)
    return pl.pallas_call(
        kern,
        grid=(b, n_pairs, s // tq),
        in_specs=[
            pl.BlockSpec((1, tq, LANES), lambda bi, p, i: (bi, i, 3 * n_pairs + p)),
            pl.BlockSpec((1, s, LANES), lambda bi, p, i: (bi, 0, 4 * n_pairs + p)),
            pl.BlockSpec((1, s, LANES), lambda bi, p, i: (bi, 0, 5 * n_pairs + p)),
            pl.BlockSpec((1, 1, 2, tq + win), lambda bi, p, i: (jnp.minimum(i, 2), p, 0, 0)),
        ],
        out_specs=pl.BlockSpec((1, tq, LANES), lambda bi, p, i: (bi, i, p)),
        out_shape=jax.ShapeDtypeStruct((b, s, ATT_WIDTH), BF16),
        scratch_shapes=[pltpu.VMEM((tq, 2 * win), F32), pltpu.VMEM((tq, 2 * win), BF16),
                        pltpu.VMEM((2, tq, win), F32)],
        compiler_params=_params(("arbitrary", "arbitrary", "arbitrary")),
        name="ca_attention",
    )(proj3, proj3, proj3, rings)


def _merge_kernel(osb_ref, oca_ref, gsb_ref, gca_ref, x_ref, psb_ref, pca_ref, wout_ref, nf_ref,
                  x1_ref, h2_ref, h2c_ref):
    br_sb = jnp.dot(osb_ref[...], psb_ref[...], preferred_element_type=F32)
    br_ca = jnp.dot(oca_ref[...], pca_ref[...], preferred_element_type=F32)
    merged = (jax.nn.sigmoid(gsb_ref[...].astype(F32)) * br_sb
              + jax.nn.sigmoid(gca_ref[...].astype(F32)) * br_ca)
    x1 = x_ref[...] + jnp.dot(merged.astype(BF16), wout_ref[...], preferred_element_type=F32)
    x1_ref[...] = x1
    ms = jnp.mean(x1 * x1, axis=-1, keepdims=True)
    h2 = x1 * lax.rsqrt(ms + RMS_EPS) * nf_ref[...]
    h2_ref[...] = h2
    tm = h2.shape[0]
    for j in range(SUBLANES):
        h2c_ref[pl.ds(j, tm, stride=SUBLANES), :] = h2[:, j * LANES:(j + 1) * LANES]


def _merge(o_sb, o_ca, proj, x2, psb, pca, wout, nf, tm):
    n = x2.shape[0]
    gate_blk = 3 * ATT_WIDTH * 2 // D_MODEL
    return pl.pallas_call(
        _merge_kernel,
        grid=(n // tm,),
        in_specs=[
            pl.BlockSpec((tm, ATT_WIDTH), lambda i: (i, 0)),
            pl.BlockSpec((tm, ATT_WIDTH), lambda i: (i, 0)),
            pl.BlockSpec((tm, D_MODEL), lambda i: (i, gate_blk)),
            pl.BlockSpec((tm, D_MODEL), lambda i: (i, gate_blk + 1)),
            pl.BlockSpec((tm, D_MODEL), lambda i: (i, 0)),
            pl.BlockSpec((ATT_WIDTH, D_MODEL), lambda i: (0, 0)),
            pl.BlockSpec((ATT_WIDTH, D_MODEL), lambda i: (0, 0)),
            pl.BlockSpec((D_MODEL, D_MODEL), lambda i: (0, 0)),
            pl.BlockSpec((1, D_MODEL), lambda i: (0, 0)),
        ],
        out_specs=[
            pl.BlockSpec((tm, D_MODEL), lambda i: (i, 0)),
            pl.BlockSpec((tm, D_MODEL), lambda i: (i, 0)),
            pl.BlockSpec((tm * SUBLANES, LANES), lambda i: (i, 0)),
        ],
        out_shape=[jax.ShapeDtypeStruct((n, D_MODEL), F32), jax.ShapeDtypeStruct((n, D_MODEL), F32),
                   jax.ShapeDtypeStruct((n * SUBLANES, LANES), F32)],
        compiler_params=_params(("arbitrary",)),
        name="merge",
    )(o_sb, o_ca, proj, proj, x2, psb, pca, wout, nf)


def _peer_candidates():
    return [(a, b) for a in range(PEER_TOPK) for b in range(PEER_TOPK) if (a + 1) * (b + 1) <= PEER_TOPK]


def _extract_top(s_scr, iota_scr, n_rounds, sentinel):
    vals, ids = [], []
    for _ in range(n_rounds):
        s = s_scr[...]
        m = jnp.max(s, axis=0, keepdims=True)
        cand = jnp.where(s == m, iota_scr[...], sentinel)
        am = jnp.min(cand, axis=0, keepdims=True)
        vals.append(m)
        ids.append(am)
        s_scr[...] = jnp.where(cand == am, -jnp.inf, s)
    return vals, ids


ROUTE_TOKENS = 128


def _route_scratch():
    n_rows = -(-len(_peer_candidates()) // SUBLANES) * SUBLANES
    return [
        pltpu.VMEM((n_rows, ROUTE_TOKENS), F32),
        pltpu.VMEM((n_rows, ROUTE_TOKENS), F32),
        pltpu.VMEM((PEER_SLOTS, ROUTE_TOKENS), F32),
        pltpu.VMEM((PEER_SLOTS, ROUTE_TOKENS), F32),
        pltpu.VMEM((2, PEER_N_KEYS, ROUTE_TOKENS), F32),
        pltpu.VMEM((PEER_N_KEYS, ROUTE_TOKENS), F32),
    ]


def _route_init(scr):
    cv_scr, ci_scr, _, _, _, iota_scr = scr
    n_cand = len(_peer_candidates())
    n_rows = cv_scr.shape[0]

    @pl.when(pl.program_id(0) == 0)
    def _():
        cv_scr[n_cand:, :] = jnp.full((n_rows - n_cand, ROUTE_TOKENS), -jnp.inf, F32)
        ci_scr[n_cand:, :] = jnp.zeros((n_rows - n_cand, ROUTE_TOKENS), F32)
        iota_scr[...] = lax.broadcasted_iota(jnp.int32, iota_scr.shape, 0).astype(F32)


def _slot_row(hd, r):
    if isinstance(hd, int):
        return slice(hd * PEER_TOPK + r, hd * PEER_TOPK + r + 1)
    return pl.ds(hd * PEER_TOPK + r, 1)


def _route_head(h_ref, wq_ref, keys_ref, hd, scr):
    cv_scr, ci_scr, ri_scr, rg_scr, s_scr, iota_scr = scr
    cands = _peer_candidates()
    n_rows = cv_scr.shape[0]
    h = h_ref[...].astype(BF16)
    iota_c = lax.broadcasted_iota(jnp.int32, (n_rows, ROUTE_TOKENS), 0).astype(F32)
    for p in range(2):
        hp = hd * 2 + p
        q = jnp.dot(h, wq_ref[hp], preferred_element_type=F32).astype(BF16)
        s_scr[p] = lax.dot_general(keys_ref[hp], q, NT_DIMS, preferred_element_type=F32)
    tops = [_extract_top(s_scr.at[p], iota_scr, PEER_TOPK, float(PEER_N_KEYS)) for p in range(2)]
    (v1, i1), (v2, i2) = tops
    i1s = [x * float(PEER_N_KEYS * PACK_ROWS) for x in i1]
    i2s = [x * float(PACK_ROWS) for x in i2]
    for r, (a, b) in enumerate(cands):
        cv_scr[r:r + 1, :] = v1[a] + v2[b]
        ci_scr[r:r + 1, :] = i1s[a] + i2s[b]
    cv = cv_scr[...]
    ci = ci_scr[...]
    best = []
    for r in range(PEER_TOPK):
        m = jnp.max(cv, axis=0, keepdims=True)
        cand = jnp.where(cv == m, iota_c, float(n_rows))
        hit = cand == jnp.min(cand, axis=0, keepdims=True)
        e = jnp.sum(jnp.where(hit, ci, 0.0), axis=0, keepdims=True)
        cv = jnp.where(hit, -jnp.inf, cv)
        best.append(m)
        ri_scr[_slot_row(hd, r), :] = e
    ex = [jnp.exp(bv - best[0]) for bv in best]
    tot = ex[0]
    for t in ex[1:]:
        tot = tot + t
    inv = 1.0 / tot
    for r in range(PEER_TOPK):
        rg_scr[_slot_row(hd, r), :] = ex[r] * inv


def _route_emit(idx_ref, g_ref, ri_scr, rg_scr):
    idx_ref[...] = ri_scr[...].T.astype(jnp.int32)
    g_ref[...] = rg_scr[...].T


def _route_kernel(h_ref, wq_ref, keys_ref, idx_ref, g_ref, *scr):
    _route_init(scr)
    for hd in range(PEER_HEADS):
        _route_head(h_ref, wq_ref, keys_ref, hd, scr)
    _route_emit(idx_ref, g_ref, scr[2], scr[3])


def _route_in_specs(block_index):
    return [
        pl.BlockSpec((ROUTE_TOKENS, D_MODEL), block_index),
        pl.BlockSpec((2 * PEER_HEADS, D_MODEL, PEER_HALF), lambda i: (0, 0, 0)),
        pl.BlockSpec((2 * PEER_HEADS, PEER_N_KEYS, PEER_HALF), lambda i: (0, 0, 0)),
    ]


def _route_out(n, block_index):
    specs = [pl.BlockSpec((ROUTE_TOKENS, PEER_SLOTS), block_index),
             pl.BlockSpec((ROUTE_TOKENS, PEER_SLOTS), block_index)]
    shapes = [jax.ShapeDtypeStruct((n, PEER_SLOTS), jnp.int32), jax.ShapeDtypeStruct((n, PEER_SLOTS), F32)]
    return specs, shapes


def _route(h2, wq3, keys_bf, n, tok_off=0):
    out_specs, out_shape = _route_out(n, lambda i: (i, 0))
    return pl.pallas_call(
        _route_kernel,
        grid=(n // ROUTE_TOKENS,),
        in_specs=_route_in_specs(lambda i: (i + tok_off // ROUTE_TOKENS, 0)),
        out_specs=out_specs,
        out_shape=out_shape,
        scratch_shapes=_route_scratch(),
        compiler_params=_params(("arbitrary",)),
        name="peer_route",
    )(h2, wq3, keys_bf)


def _pack_kernel(t_ref, o_ref, *row_major_ref):
    te = t_ref.shape[0]
    for r in range(PACK_ROWS):
        lo = t_ref[:, (2 * r) * LANES:(2 * r + 1) * LANES].astype(BF16).astype(F32)
        hi = t_ref[:, (2 * r + 1) * LANES:(2 * r + 2) * LANES].astype(BF16).astype(F32)
        word = lax.bitcast_convert_type(hi, jnp.uint32) | (lax.bitcast_convert_type(lo, jnp.uint32) >> 16)
        o_ref[pl.ds(r, te, stride=PACK_ROWS), :] = word
        for ref in row_major_ref:
            ref[:, r * LANES:(r + 1) * LANES] = word


def _pack_table(t, with_row_major=False):
    e = t.shape[0]
    te = _row_tile(e, 512)
    out_specs = [pl.BlockSpec((te * PACK_ROWS, LANES), lambda i: (i, 0))]
    out_shape = [jax.ShapeDtypeStruct((e * PACK_ROWS, LANES), jnp.uint32)]
    if with_row_major:
        out_specs.append(pl.BlockSpec((te, PACK_ROWS * LANES), lambda i: (i, 0)))
        out_shape.append(jax.ShapeDtypeStruct((e, PACK_ROWS * LANES), jnp.uint32))
    out = pl.pallas_call(
        _pack_kernel,
        grid=(e // te,),
        in_specs=[pl.BlockSpec((te, D_MODEL), lambda i: (i, 0))],
        out_specs=out_specs,
        out_shape=out_shape,
        compiler_params=_params(("arbitrary",)),
        name="pack_table",
    )(t)
    return out if with_row_major else out[0]


def _table_spec(n_rows):
    return pl.BlockSpec((n_rows, LANES), lambda i: (0, 0), pipeline_mode=pl.Buffered(1))


def _gather_rows(idx_ref, t, tab_ref, buf):
    for k in range(PEER_SLOTS):
        e = pl.multiple_of(idx_ref[t, k], PACK_ROWS)
        buf[k * PACK_ROWS:(k + 1) * PACK_ROWS, :] = tab_ref[pl.ds(e, PACK_ROWS), :]


def _gathered_bf16(buf):
    return pltpu.bitcast(buf[...], BF16)


def _chunk_pattern():
    shape = (2 * SUBLANES, PEER_SLOTS * SUBLANES)
    row = lax.broadcasted_iota(jnp.int32, shape, 0)
    col = lax.broadcasted_iota(jnp.int32, shape, 1)
    return (row & (SUBLANES - 1)) == (col & (SUBLANES - 1))


def _token_rows(t):
    if isinstance(t, int):
        return slice(t * SUBLANES, (t + 1) * SUBLANES)
    return pl.ds(pl.multiple_of(t * SUBLANES, SUBLANES), SUBLANES)


def _load_token_chunks(ref, t):
    row = ref[pl.ds(t, 1), :]
    return jnp.concatenate([row[:, j * LANES:(j + 1) * LANES] for j in range(SUBLANES)], axis=0)


def _store_token_chunks(ref, t, val):
    ref[pl.ds(t, 1), :] = jnp.concatenate([val[j:j + 1, :] for j in range(SUBLANES)], axis=1)


def _split_bf16(x):
    hi = x.astype(BF16)
    return hi, (x - hi.astype(F32)).astype(BF16)


def _token_loop(idx_ref, tab_ref, bufs, tb, unroll, compute, side_work=None):
    _gather_rows(idx_ref, 0, tab_ref, bufs[0])

    def body(i, carry):
        if side_work is not None:
            side_work(i)
        for u in range(unroll):
            t = i * unroll + u
            _gather_rows(idx_ref, jnp.minimum(t + 1, tb - 1), tab_ref, bufs[(u + 1) % 2])
            compute(t, bufs[u % 2])
        return carry

    lax.fori_loop(0, tb // unroll, body, 0)


def _peer_dot_body(idx_ref, h_ref, g_ref, tab_ref, c_ref, buf0, buf1, part_scr, tb, unroll, side_work):
    pattern = _chunk_pattern()

    def compute(t, buf):
        h16 = jnp.concatenate(_split_bf16(h_ref[_token_rows(t), :]), axis=0)
        r = lax.dot_general(h16, _gathered_bf16(buf), NT_DIMS, preferred_element_type=F32)
        r = jnp.where(pattern, r, 0.0)
        part_scr[pl.ds(t, 1), :] = jnp.sum(r[:SUBLANES] + r[SUBLANES:], axis=0, keepdims=True)

    _token_loop(idx_ref, tab_ref, (buf0, buf1), tb, unroll, compute, side_work)

    n_part = PEER_SLOTS * SUBLANES
    slot_of = lax.broadcasted_iota(jnp.int32, (n_part, PEER_SLOTS), 0) // SUBLANES
    fold = jnp.where(slot_of == lax.broadcasted_iota(jnp.int32, (n_part, PEER_SLOTS), 1), 1.0, 0.0).astype(BF16)
    p_hi, p_lo = _split_bf16(part_scr[...])
    a = (jnp.dot(p_hi, fold, preferred_element_type=F32)
         + jnp.dot(p_lo, fold, preferred_element_type=F32))
    c_ref[...] = g_ref[...] * (0.5 * a * (1.0 + lax.erf(a * (1.0 / math.sqrt(2.0)))))


def _peer_dot_kernel(idx_ref, h_ref, g_ref, tab_ref, c_ref, buf0, buf1, part_scr, *, tb, unroll):
    _peer_dot_body(idx_ref, h_ref, g_ref, tab_ref, c_ref, buf0, buf1, part_scr, tb, unroll, None)


def _peer_dot_route_kernel(idx_ref, h_ref, g_ref, tab_ref, rh_ref, wq_ref, keys_ref,
                           c_ref, ridx_ref, rg_ref, buf0, buf1, part_scr, *route_scr, tb, unroll):
    trips = tb // unroll
    phase = pl.program_id(0) % (PEER_HEADS // trips)
    _route_init(route_scr)

    def side_work(i):
        _route_head(rh_ref, wq_ref, keys_ref, phase * trips + i, route_scr)

    _peer_dot_body(idx_ref, h_ref, g_ref, tab_ref, c_ref, buf0, buf1, part_scr, tb, unroll, side_work)

    @pl.when(phase == PEER_HEADS // trips - 1)
    def _():
        _route_emit(ridx_ref, rg_ref, route_scr[2], route_scr[3])


def _peer_dot_specs(tab, tb, h_block_off):
    in_specs = [
        pl.BlockSpec((tb, PEER_SLOTS), lambda i: (i, 0), memory_space=pltpu.SMEM),
        pl.BlockSpec((tb * SUBLANES, LANES), lambda i: (i + h_block_off, 0)),
        pl.BlockSpec((tb, PEER_SLOTS), lambda i: (i, 0)),
        _table_spec(tab.shape[0]),
    ]
    scratch = [pltpu.VMEM((PEER_SLOTS * PACK_ROWS, LANES), jnp.uint32),
               pltpu.VMEM((PEER_SLOTS * PACK_ROWS, LANES), jnp.uint32),
               pltpu.VMEM((tb, PEER_SLOTS * SUBLANES), F32)]
    return in_specs, scratch


def _peer_dot(idx, h2r, gate, tab, tb, unroll, tok_off=0):
    n = idx.shape[0]
    in_specs, scratch = _peer_dot_specs(tab, tb, tok_off // tb)
    return pl.pallas_call(
        functools.partial(_peer_dot_kernel, tb=tb, unroll=unroll),
        grid=(n // tb,),
        in_specs=in_specs,
        out_specs=pl.BlockSpec((tb, PEER_SLOTS), lambda i: (i, 0)),
        out_shape=jax.ShapeDtypeStruct((n, PEER_SLOTS), F32),
        scratch_shapes=scratch,
        compiler_params=_params(("arbitrary",)),
        name="peer_dot",
    )(idx, h2r, gate, tab)


def _peer_dot_route(idx, h2r, gate, tab, h2, wq3, keys_bf, tb, unroll, tok_off, route_off):
    n = idx.shape[0]
    assert PEER_HEADS % (tb // unroll) == 0
    steps_per_block = PEER_HEADS // (tb // unroll)
    assert steps_per_block * tb == ROUTE_TOKENS and route_off % ROUTE_TOKENS == 0
    in_specs, scratch = _peer_dot_specs(tab, tb, tok_off // tb)
    route_in_block = lambda i: (i // steps_per_block + route_off // ROUTE_TOKENS, 0)
    route_block = lambda i: (i // steps_per_block, 0)
    route_out_specs, route_out_shape = _route_out(n, route_block)
    return pl.pallas_call(
        functools.partial(_peer_dot_route_kernel, tb=tb, unroll=unroll),
        grid=(n // tb,),
        in_specs=in_specs + _route_in_specs(route_in_block),
        out_specs=[pl.BlockSpec((tb, PEER_SLOTS), lambda i: (i, 0))] + route_out_specs,
        out_shape=[jax.ShapeDtypeStruct((n, PEER_SLOTS), F32)] + route_out_shape,
        scratch_shapes=scratch + _route_scratch(),
        compiler_params=_params(("arbitrary",)),
        name="peer_dot_route",
    )(idx, h2r, gate, tab, h2, wq3, keys_bf)


def _peer_axpy_kernel(idx_ref, c_ref, x_ref, gf_ref, tab_ref, y_ref, buf0, buf1, cexp_scr, *, tb, unroll):
    pattern = _chunk_pattern()
    n_part = PEER_SLOTS * SUBLANES
    slot_of = lax.broadcasted_iota(jnp.int32, (PEER_SLOTS, n_part), 1) // SUBLANES
    spread = jnp.where(slot_of == lax.broadcasted_iota(jnp.int32, (PEER_SLOTS, n_part), 0), 1.0, 0.0).astype(BF16)
    c_hi, c_lo = _split_bf16(c_ref[...])
    cexp_scr[0:tb, :] = jnp.dot(c_hi, spread, preferred_element_type=F32)
    cexp_scr[tb:2 * tb, :] = jnp.dot(c_lo, spread, preferred_element_type=F32)

    def compute(t, buf):
        coef = jnp.concatenate([jnp.broadcast_to(cexp_scr[pl.ds(t, 1), :], (SUBLANES, n_part)),
                                jnp.broadcast_to(cexp_scr[pl.ds(tb + t, 1), :], (SUBLANES, n_part))], axis=0)
        coef = jnp.where(pattern, coef, 0.0).astype(BF16)
        o16 = jnp.dot(coef, _gathered_bf16(buf), preferred_element_type=F32)
        x2 = _load_token_chunks(x_ref, t) + (o16[:SUBLANES] + o16[SUBLANES:])
        ss = jnp.sum(jnp.sum(x2 * x2, axis=1, keepdims=True), axis=0, keepdims=True)
        _store_token_chunks(y_ref, t, x2 * lax.rsqrt(ss * (1.0 / D_MODEL) + RMS_EPS) * gf_ref[...])

    _token_loop(idx_ref, tab_ref, (buf0, buf1), tb, unroll, compute)


def _peer_axpy(idx, c, x1r, gfr, tab, tb, unroll, n_tokens, n):
    kern = functools.partial(_peer_axpy_kernel, tb=tb, unroll=unroll)
    return pl.pallas_call(
        kern,
        grid=(n_tokens // tb,),
        in_specs=[
            pl.BlockSpec((tb, PEER_SLOTS), lambda i: (i, 0), memory_space=pltpu.SMEM),
            pl.BlockSpec((tb, PEER_SLOTS), lambda i: (i, 0)),
            pl.BlockSpec((tb, D_MODEL), lambda i: (i, 0)),
            pl.BlockSpec((SUBLANES, LANES), lambda i: (0, 0)),
            _table_spec(tab.shape[0]),
        ],
        out_specs=pl.BlockSpec((tb, D_MODEL), lambda i: (i, 0)),
        out_shape=jax.ShapeDtypeStruct((n, D_MODEL), F32),
        scratch_shapes=[pltpu.VMEM((PEER_SLOTS * PACK_ROWS, LANES), jnp.uint32),
                        pltpu.VMEM((PEER_SLOTS * PACK_ROWS, LANES), jnp.uint32),
                        pltpu.VMEM((2 * tb, PEER_SLOTS * SUBLANES), F32)],
        compiler_params=_params(("arbitrary",)),
        name="peer_axpy",
    )(idx, c, x1r, gfr, tab)


SC_WORKERS = 32
SC_LANES = 16
SC_CHUNK_TOKENS = 16
SC_HALF_ROWS = PEER_SLOTS // 2
SC_ROW_BLOCK = 8
SC_COL_GROUPS = LANES // SC_LANES
SC_SHARE_NUM, SC_SHARE_DEN = 15, 32


def _peer_axpy_sc(expert, c, tab_rows):
    n = expert.shape[0]
    per = n // SC_WORKERS
    assert n % (SC_WORKERS * SC_CHUNK_TOKENS) == 0
    mesh = plsc.VectorSubcoreMesh(core_axis_name="c", subcore_axis_name="s")
    params = pltpu.CompilerParams()
    if "needs_layout_passes" in pltpu.CompilerParams.__dataclass_fields__:
        params = dataclasses.replace(params, needs_layout_passes=False)
    expert2 = expert.reshape(n * 2, SC_HALF_ROWS)
    words = PACK_ROWS * LANES

    @functools.partial(
        pl.kernel, mesh=mesh, compiler_params=params,
        out_type=jax.ShapeDtypeStruct((n, D_MODEL), F32),
        scratch_types=[pltpu.VMEM((2 * SC_CHUNK_TOKENS, SC_HALF_ROWS), jnp.int32),
                       pltpu.VMEM((SC_CHUNK_TOKENS, PEER_SLOTS), F32),
                       pltpu.VMEM((SC_HALF_ROWS, words), jnp.uint32),
                       pltpu.VMEM((SC_HALF_ROWS, words), jnp.uint32),
                       pltpu.VMEM((D_MODEL,), F32),
                       pltpu.SemaphoreType.DMA, pltpu.SemaphoreType.DMA],
    )
    def sc_kernel(idx_hbm, c_hbm, tab_hbm, out_hbm, idx_v, c_v, rows0, rows1, out_v, sem0, sem1):
        wid = lax.axis_index("s") * 2 + lax.axis_index("c")
        bufs = ((rows0, sem0), (rows1, sem1))

        def gather(u, b):
            return pltpu.make_async_copy(tab_hbm.at[idx_v.at[u]], bufs[b][0], bufs[b][1])

        def accumulate(tl, half, b):
            rows = bufs[b][0]

            @pl.loop(0, PACK_ROWS)
            def _(cb):
                @pl.loop(0, SC_HALF_ROWS // SC_ROW_BLOCK)
                def _(rb):
                    acc = [None] * (2 * SC_COL_GROUPS)
                    for r in range(SC_ROW_BLOCK):
                        kk = rb * SC_ROW_BLOCK + r
                        zero = jnp.zeros((SC_LANES,), jnp.int32)
                        ck = plsc.load_gather(c_v, [zero + tl, zero + (half * SC_HALF_ROWS + kk)])
                        for g in range(SC_COL_GROUPS):
                            w = rows[kk, pl.ds(cb * LANES + g * SC_LANES, SC_LANES)]
                            hi = ck * lax.bitcast_convert_type(w & jnp.uint32(0xFFFF0000), F32)
                            lo = ck * lax.bitcast_convert_type(w << 16, F32)
                            acc[2 * g] = hi if acc[2 * g] is None else acc[2 * g] + hi
                            acc[2 * g + 1] = lo if acc[2 * g + 1] is None else acc[2 * g + 1] + lo
                    for g in range(SC_COL_GROUPS):
                        plsc.addupdate(out_v.at[pl.ds((2 * cb + 1) * LANES + g * SC_LANES, SC_LANES)], acc[2 * g])
                        plsc.addupdate(out_v.at[pl.ds(2 * cb * LANES + g * SC_LANES, SC_LANES)], acc[2 * g + 1])

        @pl.loop(0, per // SC_CHUNK_TOKENS)
        def _(ch):
            t0 = wid * per + ch * SC_CHUNK_TOKENS
            pltpu.sync_copy(idx_hbm.at[pl.ds(2 * t0, 2 * SC_CHUNK_TOKENS)], idx_v)
            pltpu.sync_copy(c_hbm.at[pl.ds(t0, SC_CHUNK_TOKENS)], c_v)
            gather(0, 0).start()

            @pl.loop(0, SC_CHUNK_TOKENS)
            def _(tl):
                for g in range(D_MODEL // SC_LANES):
                    out_v[pl.ds(g * SC_LANES, SC_LANES)] = jnp.zeros((SC_LANES,), F32)
                gather(2 * tl + 1, 1).start()
                gather(2 * tl, 0).wait()
                accumulate(tl, 0, 0)

                @pl.when(tl + 1 < SC_CHUNK_TOKENS)
                def _():
                    gather(2 * tl + 2, 0).start()

                gather(2 * tl + 1, 1).wait()
                accumulate(tl, 1, 1)
                pltpu.sync_copy(out_v, out_hbm.at[t0 + tl])

    return sc_kernel(expert2, c, tab_rows)


def _peer_dot_sc(expert, h2, tab_rows, tok_off):
    n = expert.shape[0]
    per = n // SC_WORKERS
    assert n % (SC_WORKERS * SC_CHUNK_TOKENS) == 0 and PEER_SLOTS % SC_LANES == 0
    mesh = plsc.VectorSubcoreMesh(core_axis_name="c", subcore_axis_name="s")
    params = pltpu.CompilerParams()
    if "needs_layout_passes" in pltpu.CompilerParams.__dataclass_fields__:
        params = dataclasses.replace(params, needs_layout_passes=False)
    expert2 = expert.reshape(n * 2, SC_HALF_ROWS)
    words = PACK_ROWS * LANES
    row_block = SC_LANES

    @functools.partial(
        pl.kernel, mesh=mesh, compiler_params=params,
        out_type=jax.ShapeDtypeStruct((n, PEER_SLOTS), F32),
        scratch_types=[pltpu.VMEM((2 * SC_CHUNK_TOKENS, SC_HALF_ROWS), jnp.int32),
                       pltpu.VMEM((SC_CHUNK_TOKENS, D_MODEL), F32),
                       pltpu.VMEM((SC_HALF_ROWS, words), jnp.uint32),
                       pltpu.VMEM((SC_HALF_ROWS, words), jnp.uint32),
                       pltpu.VMEM((PEER_SLOTS,), F32),
                       pltpu.SemaphoreType.DMA, pltpu.SemaphoreType.DMA],
    )
    def sc_kernel(idx_hbm, h_hbm, tab_hbm, out_hbm, idx_v, h_v, rows0, rows1, a_v, sem0, sem1):
        wid = lax.axis_index("s") * 2 + lax.axis_index("c")
        bufs = ((rows0, sem0), (rows1, sem1))
        lane = lax.iota(jnp.int32, SC_LANES)

        def gather(u, b):
            return pltpu.make_async_copy(tab_hbm.at[idx_v.at[u]], bufs[b][0], bufs[b][1])

        def accumulate(tl, half, b):
            rows = bufs[b][0]

            @pl.loop(0, PACK_ROWS)
            def _(cb):
                h_hi = [h_v[tl, pl.ds((2 * cb + 1) * LANES + g * SC_LANES, SC_LANES)] for g in range(SC_COL_GROUPS)]
                h_lo = [h_v[tl, pl.ds(2 * cb * LANES + g * SC_LANES, SC_LANES)] for g in range(SC_COL_GROUPS)]

                @pl.loop(0, SC_HALF_ROWS // row_block)
                def _(rb):
                    res = jnp.zeros((SC_LANES,), F32)
                    for r in range(row_block):
                        kk = rb * row_block + r
                        acc = None
                        for g in range(SC_COL_GROUPS):
                            w = rows[kk, pl.ds(cb * LANES + g * SC_LANES, SC_LANES)]
                            hi = h_hi[g] * lax.bitcast_convert_type(w & jnp.uint32(0xFFFF0000), F32)
                            lo = h_lo[g] * lax.bitcast_convert_type(w << 16, F32)
                            acc = hi + lo if acc is None else acc + hi + lo
                        res = jnp.where(lane == r, jnp.sum(acc), res)
                    plsc.addupdate(a_v.at[pl.ds(half * SC_HALF_ROWS + rb * row_block, SC_LANES)], res)

        @pl.loop(0, per // SC_CHUNK_TOKENS)
        def _(ch):
            t0 = wid * per + ch * SC_CHUNK_TOKENS
            pltpu.sync_copy(idx_hbm.at[pl.ds(2 * t0, 2 * SC_CHUNK_TOKENS)], idx_v)
            pltpu.sync_copy(h_hbm.at[pl.ds(tok_off + t0, SC_CHUNK_TOKENS)], h_v)
            gather(0, 0).start()

            @pl.loop(0, SC_CHUNK_TOKENS)
            def _(tl):
                for g in range(PEER_SLOTS // SC_LANES):
                    a_v[pl.ds(g * SC_LANES, SC_LANES)] = jnp.zeros((SC_LANES,), F32)
                gather(2 * tl + 1, 1).start()
                gather(2 * tl, 0).wait()
                accumulate(tl, 0, 0)

                @pl.when(tl + 1 < SC_CHUNK_TOKENS)
                def _():
                    gather(2 * tl + 2, 0).start()

                gather(2 * tl + 1, 1).wait()
                accumulate(tl, 1, 1)
                pltpu.sync_copy(a_v, out_hbm.at[t0 + tl])

    return sc_kernel(expert2, h2, tab_rows)


def _gelu_gate_kernel(a_ref, g_ref, c_ref):
    a = a_ref[...]
    c_ref[...] = g_ref[...] * (0.5 * a * (1.0 + lax.erf(a * (1.0 / math.sqrt(2.0)))))


def _gelu_gate(a, gate):
    n = a.shape[0]
    tm = _row_tile(n, 1024)
    spec = pl.BlockSpec((tm, PEER_SLOTS), lambda i: (i, 0))
    return pl.pallas_call(
        _gelu_gate_kernel, grid=(n // tm,), in_specs=[spec, spec], out_specs=spec,
        out_shape=jax.ShapeDtypeStruct((n, PEER_SLOTS), F32),
        compiler_params=_params(("arbitrary",)), name="gelu_gate",
    )(a, gate)


def _finish_kernel(y_hbm, x_ref, p_ref, gf_ref, o_ref):
    del y_hbm
    x2 = x_ref[...] + p_ref[...]
    ms = jnp.mean(x2 * x2, axis=-1, keepdims=True)
    o_ref[...] = x2 * lax.rsqrt(ms + RMS_EPS) * gf_ref[...]


def _peer_finish(y, x1, p_tail, gf, tm):
    n = y.shape[0]
    n_tail = p_tail.shape[0]
    off = (n - n_tail) // tm
    assert (n - n_tail) % tm == 0 and n_tail % tm == 0
    return pl.pallas_call(
        _finish_kernel,
        grid=(n_tail // tm,),
        in_specs=[
            pl.BlockSpec(memory_space=pl.ANY),
            pl.BlockSpec((tm, D_MODEL), lambda i: (i + off, 0)),
            pl.BlockSpec((tm, D_MODEL), lambda i: (i, 0)),
            pl.BlockSpec((1, D_MODEL), lambda i: (0, 0)),
        ],
        out_specs=pl.BlockSpec((tm, D_MODEL), lambda i: (i + off, 0)),
        out_shape=jax.ShapeDtypeStruct((n, D_MODEL), F32),
        input_output_aliases={0: 0},
        compiler_params=_params(("arbitrary",)),
        name="peer_finish",
    )(y, x1, p_tail, gf)


def _row_tile(n, want):
    t = min(want, n)
    assert n % t == 0
    return t


def _layer(x2, b, s, norm_mix, w_in, w_sb_proj, w_ca_proj, ca_rel_bias, w_out, norm_ffn,
           peer_w_query, peer_sub_keys, peer_u, peer_v, final_gain):
    n = b * s
    col = np.arange(IN_COLS)
    is_q = (col < ATT_WIDTH) | ((col >= 3 * ATT_WIDTH) & (col < 4 * ATT_WIDTH))
    col_scale = np.where(is_q, math.log2(math.e) / math.sqrt(HEAD_DIM), 1.0)
    w_in_bf = (w_in * jnp.asarray(col_scale, F32)[None, :]).astype(BF16)

    proj = _inproj(x2, norm_mix.reshape(1, D_MODEL), w_in_bf, _row_tile(n, 512))
    proj3 = proj.reshape(b, s, IN_COLS)

    tq = 256
    assert s % tq == 0 and s >= tq + CA_PAD
    sb_tq = min(2048, s)
    assert s % sb_tq == 0
    o_sb = _sb_attention(proj3, sb_tq, 256)
    win = tq + CA_PAD
    o_ca = _ca_attention(proj3, _ca_rings(ca_rel_bias, tq, win), tq, win, 64)

    x1, h2, h2r = _merge(o_sb.reshape(n, ATT_WIDTH), o_ca.reshape(n, ATT_WIDTH), proj, x2,
                    w_sb_proj.astype(BF16), w_ca_proj.astype(BF16), w_out.astype(BF16),
                    norm_ffn.reshape(1, D_MODEL), _row_tile(n, 512))

    keys_bf = peer_sub_keys.reshape(2 * PEER_HEADS, PEER_N_KEYS, PEER_HALF).astype(BF16)
    wq3 = jnp.transpose(peer_w_query.astype(BF16).reshape(D_MODEL, 2 * PEER_HEADS, PEER_HALF), (1, 0, 2))
    tab_u, tab_u_rows = _pack_table(peer_u, with_row_major=True)
    tbg, unroll = 64, 16
    sc_unit = SC_WORKERS * SC_CHUNK_TOKENS
    n_seg = b if (b > 1 and s % ROUTE_TOKENS == 0) else 1
    seg = n // n_seg
    assert seg % ROUTE_TOKENS == 0
    use_sc = seg % sc_unit == 0
    idx_j, gate_j = _route(h2, wq3, keys_bf, seg)
    idx_parts, c_parts = [], []
    for j in range(n_seg):
        idx_parts.append(idx_j)
        last = j + 1 == n_seg
        if j % 2 == 1 and use_sc and not last:
            a_j = _peer_dot_sc(idx_j // PACK_ROWS, h2, tab_u_rows, j * seg)
            c_j = _gelu_gate(a_j, gate_j)
            if not last:
                idx_j, gate_j = _route(h2, wq3, keys_bf, seg, (j + 1) * seg)
        elif not last:
            c_j, idx_j, gate_j = _peer_dot_route(idx_j, h2r, gate_j, tab_u, h2, wq3, keys_bf, tbg, unroll,
                                                 j * seg, (j + 1) * seg)
        else:
            c_j = _peer_dot(idx_j, h2r, gate_j, tab_u, tbg, unroll, j * seg)
        c_parts.append(c_j)
    tab_v, tab_v_rows = _pack_table(peer_v, with_row_major=True)
    n_sc = (n * SC_SHARE_NUM // SC_SHARE_DEN) // sc_unit * sc_unit if use_sc else 0
    n_tc = n - n_sc
    head_segs = -(-n_tc // seg)
    idx_head = jnp.concatenate(idx_parts[:head_segs], axis=0)
    c_head = jnp.concatenate(c_parts[:head_segs], axis=0)
    y = _peer_axpy(idx_head, c_head, x1, final_gain.reshape(SUBLANES, LANES), tab_v, tbg, unroll, n_tc, n)
    if n_sc:
        first = n_tc // seg
        idx_tail = jnp.concatenate(idx_parts[first:], axis=0)[n_tc - first * seg:]
        c_tail = jnp.concatenate(c_parts[first:], axis=0)[n_tc - first * seg:]
        p_sc = _peer_axpy_sc(idx_tail // PACK_ROWS, c_tail, tab_v_rows)
        y = _peer_finish(y, x1, p_sc, final_gain.reshape(1, D_MODEL), sc_unit)
    return y


def kernel(x, norm_mix, w_in, w_sb_proj, w_ca_proj, ca_rel_bias, w_out, norm_ffn,
           peer_w_query, peer_sub_keys, peer_u, peer_v, norm_final):
    b, s, d = x.shape
    depth = norm_mix.shape[0]
    assert d == D_MODEL and depth == 1
    y = _layer(x.reshape(b * s, d), b, s, norm_mix[0], w_in[0], w_sb_proj[0], w_ca_proj[0],
               ca_rel_bias[0], w_out[0], norm_ffn[0], peer_w_query[0], peer_sub_keys[0],
               peer_u[0], peer_v[0], norm_final)
    return y.reshape(b, s, d)
```

```python
import dataclasses
import functools
import math

import numpy as np
import jax
import jax.numpy as jnp
from jax import lax
from jax.experimental import pallas as pl
from jax.experimental.pallas import tpu as pltpu
from jax.experimental.pallas import tpu_sc as plsc

D_MODEL = 1024
HEAD_DIM = 64
N_HEADS = 8
ATT_WIDTH = N_HEADS * HEAD_DIM
IN_COLS = 6 * ATT_WIDTH + 2 * D_MODEL
CHUNK = 64
CA_LEFT_CHUNKS = 8
CA_PAD = CA_LEFT_CHUNKS * CHUNK
REL_CLIP = 128
PEER_HEADS = 8
PEER_N_KEYS = 128
PEER_HALF = 128
PEER_TOPK = 16
PEER_SLOTS = PEER_HEADS * PEER_TOPK
RMS_EPS = 1e-6
NEG_INF = -1e30

LANES = 128
SUBLANES = 8
PACK_ROWS = D_MODEL // 2 // LANES

VMEM_LIMIT = 56 * 1024 * 1024

F32 = jnp.float32
BF16 = jnp.bfloat16
NT_DIMS = (((1,), (1,)), ((), ()))


def _params(sem, vmem=VMEM_LIMIT):
    return pltpu.CompilerParams(dimension_semantics=sem, vmem_limit_bytes=vmem)


def _inproj_kernel(x_ref, g_ref, w_ref, o_ref):
    x = x_ref[...]
    ms = jnp.mean(x * x, axis=-1, keepdims=True)
    h = (x * lax.rsqrt(ms + RMS_EPS) * g_ref[...]).astype(BF16)
    for c in range(IN_COLS // D_MODEL):
        cols = slice(c * D_MODEL, (c + 1) * D_MODEL)
        o_ref[:, cols] = jnp.dot(h, w_ref[:, cols], preferred_element_type=F32).astype(o_ref.dtype)


def _inproj(x2, g, w_bf, tm):
    n = x2.shape[0]
    return pl.pallas_call(
        _inproj_kernel,
        grid=(n // tm,),
        in_specs=[
            pl.BlockSpec((tm, D_MODEL), lambda i: (i, 0)),
            pl.BlockSpec((1, D_MODEL), lambda i: (0, 0)),
            pl.BlockSpec((D_MODEL, IN_COLS), lambda i: (0, 0)),
        ],
        out_specs=pl.BlockSpec((tm, IN_COLS), lambda i: (i, 0)),
        out_shape=jax.ShapeDtypeStruct((n, IN_COLS), BF16),
        compiler_params=_params(("arbitrary",)),
        name="inproj",
    )(x2, g, w_bf)


def _sb_kernel(q_ref, k_ref, v_ref, o_ref, acc_ref, r_ref, z_scr, lat_scr, *, tq, tk):
    i = pl.program_id(2)
    n_chunks = tq // tk
    lane = lax.broadcasted_iota(jnp.int32, (1, LANES), 1)
    head_mask = (lane < HEAD_DIM, lane >= HEAD_DIM)
    kk = lax.broadcasted_iota(jnp.int32, (tk, tk), 0)
    ss = lax.broadcasted_iota(jnp.int32, (tk, tk), 1)
    stri = jnp.where(kk > ss, 1.0, 0.0).astype(BF16)
    causal = ss < kk
    sign = jnp.uint32(0x80000000)

    acc_ref[...] = jnp.zeros_like(acc_ref)
    r_ref[...] = jnp.zeros_like(r_ref)

    def rows(c):
        return slice(c * tk, (c + 1) * tk)

    def both_heads(x):
        zero = jnp.zeros_like(x)
        return jnp.concatenate([jnp.where(head_mask[0], x, zero), jnp.where(head_mask[1], x, zero)], axis=0)

    def logits(j, chunks):
        k2 = both_heads(k_ref[0, pl.ds(pl.multiple_of(j * tk, tk), tk), :])
        for c in chunks:
            z_scr[c] = lax.dot_general(q_ref[0, rows(c), :], k2, NT_DIMS, preferred_element_type=F32)

    def later_sums(chunks, diag_chunk):
        for c in chunks:
            for h in range(2):
                cols = slice(h * tk, (h + 1) * tk)
                z = z_scr[c, :, cols]
                neg_abs = lax.bitcast_convert_type(lax.bitcast_convert_type(z, jnp.uint32) | sign, F32)
                sp = jnp.maximum(z, 0.0) + jnp.log2(1.0 + jnp.exp2(neg_abs))
                if c == diag_chunk:
                    sp = jnp.where(causal, sp, 0.0)
                z_scr[c, :, cols] = z - sp
                r = r_ref[h, rows(c), :]
                later = jnp.concatenate([r] * (tk // LANES), axis=1) + jnp.dot(
                    sp.astype(BF16), stri, preferred_element_type=F32)
                lat_scr[c, :, cols] = later
                r_ref[h, rows(c), :] = jnp.broadcast_to(later[:, 0:1] + sp[:, 0:1], (tk, LANES))

    def weigh(j, chunks, diag_chunk):
        v2 = both_heads(v_ref[0, pl.ds(pl.multiple_of(j * tk, tk), tk), :])
        for c in chunks:
            ws = []
            for h in range(2):
                cols = slice(h * tk, (h + 1) * tk)
                logw = z_scr[c, :, cols] - lat_scr[c, :, cols]
                if c == diag_chunk:
                    logw = jnp.where(causal, logw, NEG_INF)
                ws.append(jnp.exp2(logw).astype(BF16))
            acc_ref[rows(c), :] += jnp.dot(jnp.concatenate(ws, axis=1), v2, preferred_element_type=F32)

    all_chunks = list(range(n_chunks))
    base = i * n_chunks
    for d in range(n_chunks - 1, -1, -1):
        chunks = list(range(d, n_chunks))
        logits(base + d, chunks)
        later_sums(chunks, d)
        weigh(base + d, chunks, d)

    def body(jj, carry):
        j = base - 1 - jj
        logits(j, all_chunks)
        later_sums(all_chunks, None)
        weigh(j, all_chunks, None)
        return carry

    lax.fori_loop(0, base, body, 0)
    o_ref[0] = acc_ref[...].astype(o_ref.dtype)


def _sb_attention(proj3, tq, tk):
    b, s, _ = proj3.shape
    n_pairs = ATT_WIDTH // LANES
    kern = functools.partial(_sb_kernel, tq=tq, tk=tk)
    return pl.pallas_call(
        kern,
        grid=(b, n_pairs, s // tq),
        in_specs=[
            pl.BlockSpec((1, tq, LANES), lambda bi, p, i: (bi, i, p)),
            pl.BlockSpec((1, s, LANES), lambda bi, p, i: (bi, 0, n_pairs + p)),
            pl.BlockSpec((1, s, LANES), lambda bi, p, i: (bi, 0, 2 * n_pairs + p)),
        ],
        out_specs=pl.BlockSpec((1, tq, LANES), lambda bi, p, i: (bi, i, p)),
        out_shape=jax.ShapeDtypeStruct((b, s, ATT_WIDTH), BF16),
        scratch_shapes=[
            pltpu.VMEM((tq, LANES), F32),
            pltpu.VMEM((2, tq, LANES), F32),
            pltpu.VMEM((tq // tk, tk, 2 * tk), F32),
            pltpu.VMEM((tq // tk, tk, 2 * tk), F32),
        ],
        compiler_params=_params(("arbitrary", "arbitrary", "arbitrary")),
        name="sb_attention",
    )(proj3, proj3, proj3)


def _ca_kernel(q_ref, k_ref, v_ref, ring_ref, o_ref, z_scr, p_scr, b_scr, *, tq, win, rc):
    i = pl.program_id(2)
    ws = pl.multiple_of(jnp.maximum(i * tq - CA_PAD, 0), tq)

    @pl.when(i <= 2)
    def _():
        off = i * tq - ws
        r = lax.broadcasted_iota(jnp.int32, (tq, win), 0)
        c = lax.broadcasted_iota(jnp.int32, (tq, win), 1)
        relc = ((off + r) >> 6) - (c >> 6)
        valid = (relc >= 0) & (relc <= CA_LEFT_CHUNKS)
        for h in range(2):
            ring = jnp.broadcast_to(ring_ref[0, 0, h:h + 1, :], (tq, ring_ref.shape[-1]))
            toep = pltpu.roll(ring, 0, 1, stride=1, stride_axis=0)
            b_scr[h] = jnp.where(valid, toep[:, :win], NEG_INF)

    lane = lax.broadcasted_iota(jnp.int32, (1, LANES), 1)
    head_mask = (lane < HEAD_DIM, lane >= HEAD_DIM)
    k = k_ref[0, pl.ds(ws, win), :]
    v = v_ref[0, pl.ds(ws, win), :]
    zero = jnp.zeros_like(k)
    k2 = jnp.concatenate([jnp.where(head_mask[0], k, zero), jnp.where(head_mask[1], k, zero)], axis=0)
    z_scr[...] = lax.dot_general(q_ref[0], k2, NT_DIMS, preferred_element_type=F32)
    inv = [[None] * (tq // rc) for _ in range(2)]
    for h in range(2):
        for c in range(tq // rc):
            rows = slice(c * rc, (c + 1) * rc)
            cols = slice(h * win, (h + 1) * win)
            z = z_scr[rows, cols] + b_scr[h, rows, :]
            p = jnp.exp2(z - jnp.max(z, axis=1, keepdims=True))
            inv[h][c] = 1.0 / jnp.sum(p, axis=1, keepdims=True)
            p_scr[rows, cols] = p.astype(BF16)
    outs = [jnp.dot(p_scr[:, h * win:(h + 1) * win], jnp.where(head_mask[h], v, zero),
                    preferred_element_type=F32) for h in range(2)]
    o_ref[0] = (outs[0] * jnp.concatenate(inv[0], axis=0)
                + outs[1] * jnp.concatenate(inv[1], axis=0)).astype(o_ref.dtype)


def _ca_rings(rel_table, tq, win):
    ring_len = tq + win
    out = []
    for var in range(3):
        off = var * tq - max(var * tq - CA_PAD, 0)
        dist = np.concatenate([off - np.arange(win), [0], off + np.arange(tq - 1, 0, -1)])
        out.append(rel_table.astype(F32)[:, np.clip(dist, -REL_CLIP, REL_CLIP) + REL_CLIP])
    rings = jnp.stack(out) * math.log2(math.e)
    return rings.reshape(3, rel_table.shape[0] // 2, 2, ring_len)


def _ca_attention(proj3, rings, tq, win, rc):
    b, s, _ = proj3.shape
    n_pairs = ATT_WIDTH // LANES
    assert CHUNK == 64 and (tq + win) % LANES == 0
    kern = functools.partial(_ca_kernel, tq=tq, win=win, rc=rc)
    return pl.pallas_call(
        kern,
        grid=(b, n_pairs, s // tq),
        in_specs=[
            pl.BlockSpec((1, tq, LANES), lambda bi, p, i: (bi, i, 3 * n_pairs + p)),
            pl.BlockSpec((1, s, LANES), lambda bi, p, i: (bi, 0, 4 * n_pairs + p)),
            pl.BlockSpec((1, s, LANES), lambda bi, p, i: (bi, 0, 5 * n_pairs + p)),
            pl.BlockSpec((1, 1, 2, tq + win), lambda bi, p, i: (jnp.minimum(i, 2), p, 0, 0)),
        ],
        out_specs=pl.BlockSpec((1, tq, LANES), lambda bi, p, i: (bi, i, p)),
        out_shape=jax.ShapeDtypeStruct((b, s, ATT_WIDTH), BF16),
        scratch_shapes=[pltpu.VMEM((tq, 2 * win), F32), pltpu.VMEM((tq, 2 * win), BF16),
                        pltpu.VMEM((2, tq, win), F32)],
        compiler_params=_params(("arbitrary", "arbitrary", "arbitrary")),
        name="ca_attention",
    )(proj3, proj3, proj3, rings)


def _merge_kernel(osb_ref, oca_ref, gsb_ref, gca_ref, x_ref, psb_ref, pca_ref, wout_ref, nf_ref,
                  x1_ref, h2_ref, h2c_ref):
    br_sb = jnp.dot(osb_ref[...], psb_ref[...], preferred_element_type=F32)
    br_ca = jnp.dot(oca_ref[...], pca_ref[...], preferred_element_type=F32)
    merged = (jax.nn.sigmoid(gsb_ref[...].astype(F32)) * br_sb
              + jax.nn.sigmoid(gca_ref[...].astype(F32)) * br_ca)
    x1 = x_ref[...] + jnp.dot(merged.astype(BF16), wout_ref[...], preferred_element_type=F32)
    x1_ref[...] = x1
    ms = jnp.mean(x1 * x1, axis=-1, keepdims=True)
    h2 = x1 * lax.rsqrt(ms + RMS_EPS) * nf_ref[...]
    h2_ref[...] = h2
    tm = h2.shape[0]
    for j in range(SUBLANES):
        h2c_ref[pl.ds(j, tm, stride=SUBLANES), :] = h2[:, j * LANES:(j + 1) * LANES]


def _merge(o_sb, o_ca, proj, x2, psb, pca, wout, nf, tm):
    n = x2.shape[0]
    gate_blk = 3 * ATT_WIDTH * 2 // D_MODEL
    return pl.pallas_call(
        _merge_kernel,
        grid=(n // tm,),
        in_specs=[
            pl.BlockSpec((tm, ATT_WIDTH), lambda i: (i, 0)),
            pl.BlockSpec((tm, ATT_WIDTH), lambda i: (i, 0)),
            pl.BlockSpec((tm, D_MODEL), lambda i: (i, gate_blk)),
            pl.BlockSpec((tm, D_MODEL), lambda i: (i, gate_blk + 1)),
            pl.BlockSpec((tm, D_MODEL), lambda i: (i, 0)),
            pl.BlockSpec((ATT_WIDTH, D_MODEL), lambda i: (0, 0)),
            pl.BlockSpec((ATT_WIDTH, D_MODEL), lambda i: (0, 0)),
            pl.BlockSpec((D_MODEL, D_MODEL), lambda i: (0, 0)),
            pl.BlockSpec((1, D_MODEL), lambda i: (0, 0)),
        ],
        out_specs=[
            pl.BlockSpec((tm, D_MODEL), lambda i: (i, 0)),
            pl.BlockSpec((tm, D_MODEL), lambda i: (i, 0)),
            pl.BlockSpec((tm * SUBLANES, LANES), lambda i: (i, 0)),
        ],
        out_shape=[jax.ShapeDtypeStruct((n, D_MODEL), F32), jax.ShapeDtypeStruct((n, D_MODEL), F32),
                   jax.ShapeDtypeStruct((n * SUBLANES, LANES), F32)],
        compiler_params=_params(("arbitrary",)),
        name="merge",
    )(o_sb, o_ca, proj, proj, x2, psb, pca, wout, nf)


def _peer_candidates():
    return [(a, b) for a in range(PEER_TOPK) for b in range(PEER_TOPK) if (a + 1) * (b + 1) <= PEER_TOPK]


def _extract_top(s_scr, iota_scr, n_rounds, sentinel):
    vals, ids = [], []
    for _ in range(n_rounds):
        s = s_scr[...]
        m = jnp.max(s, axis=0, keepdims=True)
        cand = jnp.where(s == m, iota_scr[...], sentinel)
        am = jnp.min(cand, axis=0, keepdims=True)
        vals.append(m)
        ids.append(am)
        s_scr[...] = jnp.where(cand == am, -jnp.inf, s)
    return vals, ids


ROUTE_TOKENS = 128


def _route_scratch():
    n_rows = -(-len(_peer_candidates()) // SUBLANES) * SUBLANES
    return [
        pltpu.VMEM((n_rows, ROUTE_TOKENS), F32),
        pltpu.VMEM((n_rows, ROUTE_TOKENS), F32),
        pltpu.VMEM((PEER_SLOTS, ROUTE_TOKENS), F32),
        pltpu.VMEM((PEER_SLOTS, ROUTE_TOKENS), F32),
        pltpu.VMEM((2, PEER_N_KEYS, ROUTE_TOKENS), F32),
        pltpu.VMEM((PEER_N_KEYS, ROUTE_TOKENS), F32),
    ]


def _route_init(scr):
    cv_scr, ci_scr, _, _, _, iota_scr = scr
    n_cand = len(_peer_candidates())
    n_rows = cv_scr.shape[0]

    @pl.when(pl.program_id(0) == 0)
    def _():
        cv_scr[n_cand:, :] = jnp.full((n_rows - n_cand, ROUTE_TOKENS), -jnp.inf, F32)
        ci_scr[n_cand:, :] = jnp.zeros((n_rows - n_cand, ROUTE_TOKENS), F32)
        iota_scr[...] = lax.broadcasted_iota(jnp.int32, iota_scr.shape, 0).astype(F32)


def _slot_row(hd, r):
    if isinstance(hd, int):
        return slice(hd * PEER_TOPK + r, hd * PEER_TOPK + r + 1)
    return pl.ds(hd * PEER_TOPK + r, 1)


def _route_head(h_ref, wq_ref, keys_ref, hd, scr):
    cv_scr, ci_scr, ri_scr, rg_scr, s_scr, iota_scr = scr
    cands = _peer_candidates()
    n_rows = cv_scr.shape[0]
    h = h_ref[...].astype(BF16)
    iota_c = lax.broadcasted_iota(jnp.int32, (n_rows, ROUTE_TOKENS), 0).astype(F32)
    for p in range(2):
        hp = hd * 2 + p
        q = jnp.dot(h, wq_ref[hp], preferred_element_type=F32).astype(BF16)
        s_scr[p] = lax.dot_general(keys_ref[hp], q, NT_DIMS, preferred_element_type=F32)
    tops = [_extract_top(s_scr.at[p], iota_scr, PEER_TOPK, float(PEER_N_KEYS)) for p in range(2)]
    (v1, i1), (v2, i2) = tops
    i1s = [x * float(PEER_N_KEYS * PACK_ROWS) for x in i1]
    i2s = [x * float(PACK_ROWS) for x in i2]
    for r, (a, b) in enumerate(cands):
        cv_scr[r:r + 1, :] = v1[a] + v2[b]
        ci_scr[r:r + 1, :] = i1s[a] + i2s[b]
    cv = cv_scr[...]
    ci = ci_scr[...]
    best = []
    for r in range(PEER_TOPK):
        m = jnp.max(cv, axis=0, keepdims=True)
        cand = jnp.where(cv == m, iota_c, float(n_rows))
        hit = cand == jnp.min(cand, axis=0, keepdims=True)
        e = jnp.sum(jnp.where(hit, ci, 0.0), axis=0, keepdims=True)
        cv = jnp.where(hit, -jnp.inf, cv)
        best.append(m)
        ri_scr[_slot_row(hd, r), :] = e
    ex = [jnp.exp(bv - best[0]) for bv in best]
    tot = ex[0]
    for t in ex[1:]:
        tot = tot + t
    inv = 1.0 / tot
    for r in range(PEER_TOPK):
        rg_scr[_slot_row(hd, r), :] = ex[r] * inv


def _route_emit(idx_ref, g_ref, ri_scr, rg_scr):
    idx_ref[...] = ri_scr[...].T.astype(jnp.int32)
    g_ref[...] = rg_scr[...].T


def _route_kernel(h_ref, wq_ref, keys_ref, idx_ref, g_ref, *scr):
    _route_init(scr)
    for hd in range(PEER_HEADS):
        _route_head(h_ref, wq_ref, keys_ref, hd, scr)
    _route_emit(idx_ref, g_ref, scr[2], scr[3])


def _route_in_specs(block_index):
    return [
        pl.BlockSpec((ROUTE_TOKENS, D_MODEL), block_index),
        pl.BlockSpec((2 * PEER_HEADS, D_MODEL, PEER_HALF), lambda i: (0, 0, 0)),
        pl.BlockSpec((2 * PEER_HEADS, PEER_N_KEYS, PEER_HALF), lambda i: (0, 0, 0)),
    ]


def _route_out(n, block_index):
    specs = [pl.BlockSpec((ROUTE_TOKENS, PEER_SLOTS), block_index),
             pl.BlockSpec((ROUTE_TOKENS, PEER_SLOTS), block_index)]
    shapes = [jax.ShapeDtypeStruct((n, PEER_SLOTS), jnp.int32), jax.ShapeDtypeStruct((n, PEER_SLOTS), F32)]
    return specs, shapes


def _route(h2, wq3, keys_bf, n, tok_off=0):
    out_specs, out_shape = _route_out(n, lambda i: (i, 0))
    return pl.pallas_call(
        _route_kernel,
        grid=(n // ROUTE_TOKENS,),
        in_specs=_route_in_specs(lambda i: (i + tok_off // ROUTE_TOKENS, 0)),
        out_specs=out_specs,
        out_shape=out_shape,
        scratch_shapes=_route_scratch(),
        compiler_params=_params(("arbitrary",)),
        name="peer_route",
    )(h2, wq3, keys_bf)


def _pack_kernel(t_ref, o_ref, *row_major_ref):
    te = t_ref.shape[0]
    for r in range(PACK_ROWS):
        lo = t_ref[:, (2 * r) * LANES:(2 * r + 1) * LANES].astype(BF16).astype(F32)
        hi = t_ref[:, (2 * r + 1) * LANES:(2 * r + 2) * LANES].astype(BF16).astype(F32)
        word = lax.bitcast_convert_type(hi, jnp.uint32) | (lax.bitcast_convert_type(lo, jnp.uint32) >> 16)
        o_ref[pl.ds(r, te, stride=PACK_ROWS), :] = word
        for ref in row_major_ref:
            ref[:, r * LANES:(r + 1) * LANES] = word


def _pack_table(t, with_row_major=False):
    e = t.shape[0]
    te = _row_tile(e, 512)
    out_specs = [pl.BlockSpec((te * PACK_ROWS, LANES), lambda i: (i, 0))]
    out_shape = [jax.ShapeDtypeStruct((e * PACK_ROWS, LANES), jnp.uint32)]
    if with_row_major:
        out_specs.append(pl.BlockSpec((te, PACK_ROWS * LANES), lambda i: (i, 0)))
        out_shape.append(jax.ShapeDtypeStruct((e, PACK_ROWS * LANES), jnp.uint32))
    out = pl.pallas_call(
        _pack_kernel,
        grid=(e // te,),
        in_specs=[pl.BlockSpec((te, D_MODEL), lambda i: (i, 0))],
        out_specs=out_specs,
        out_shape=out_shape,
        compiler_params=_params(("arbitrary",)),
        name="pack_table",
    )(t)
    return out if with_row_major else out[0]


def _table_spec(n_rows):
    return pl.BlockSpec((n_rows, LANES), lambda i: (0, 0), pipeline_mode=pl.Buffered(1))


def _gather_rows(idx_ref, t, tab_ref, buf):
    for k in range(PEER_SLOTS):
        e = pl.multiple_of(idx_ref[t, k], PACK_ROWS)
        buf[k * PACK_ROWS:(k + 1) * PACK_ROWS, :] = tab_ref[pl.ds(e, PACK_ROWS), :]


def _gathered_bf16(buf):
    return pltpu.bitcast(buf[...], BF16)


def _chunk_pattern():
    shape = (2 * SUBLANES, PEER_SLOTS * SUBLANES)
    row = lax.broadcasted_iota(jnp.int32, shape, 0)
    col = lax.broadcasted_iota(jnp.int32, shape, 1)
    return (row & (SUBLANES - 1)) == (col & (SUBLANES - 1))


def _token_rows(t):
    if isinstance(t, int):
        return slice(t * SUBLANES, (t + 1) * SUBLANES)
    return pl.ds(pl.multiple_of(t * SUBLANES, SUBLANES), SUBLANES)


def _load_token_chunks(ref, t):
    row = ref[pl.ds(t, 1), :]
    return jnp.concatenate([row[:, j * LANES:(j + 1) * LANES] for j in range(SUBLANES)], axis=0)


def _store_token_chunks(ref, t, val):
    ref[pl.ds(t, 1), :] = jnp.concatenate([val[j:j + 1, :] for j in range(SUBLANES)], axis=1)


def _split_bf16(x):
    hi = x.astype(BF16)
    return hi, (x - hi.astype(F32)).astype(BF16)


def _token_loop(idx_ref, tab_ref, bufs, tb, unroll, compute, side_work=None):
    _gather_rows(idx_ref, 0, tab_ref, bufs[0])

    def body(i, carry):
        if side_work is not None:
            side_work(i)
        for u in range(unroll):
            t = i * unroll + u
            _gather_rows(idx_ref, jnp.minimum(t + 1, tb - 1), tab_ref, bufs[(u + 1) % 2])
            compute(t, bufs[u % 2])
        return carry

    lax.fori_loop(0, tb // unroll, body, 0)


def _peer_dot_body(idx_ref, h_ref, g_ref, tab_ref, c_ref, buf0, buf1, part_scr, tb, unroll, side_work):
    pattern = _chunk_pattern()

    def compute(t, buf):
        h16 = jnp.concatenate(_split_bf16(h_ref[_token_rows(t), :]), axis=0)
        r = lax.dot_general(h16, _gathered_bf16(buf), NT_DIMS, preferred_element_type=F32)
        r = jnp.where(pattern, r, 0.0)
        part_scr[pl.ds(t, 1), :] = jnp.sum(r[:SUBLANES] + r[SUBLANES:], axis=0, keepdims=True)

    _token_loop(idx_ref, tab_ref, (buf0, buf1), tb, unroll, compute, side_work)

    n_part = PEER_SLOTS * SUBLANES
    slot_of = lax.broadcasted_iota(jnp.int32, (n_part, PEER_SLOTS), 0) // SUBLANES
    fold = jnp.where(slot_of == lax.broadcasted_iota(jnp.int32, (n_part, PEER_SLOTS), 1), 1.0, 0.0).astype(BF16)
    p_hi, p_lo = _split_bf16(part_scr[...])
    a = (jnp.dot(p_hi, fold, preferred_element_type=F32)
         + jnp.dot(p_lo, fold, preferred_element_type=F32))
    c_ref[...] = g_ref[...] * (0.5 * a * (1.0 + lax.erf(a * (1.0 / math.sqrt(2.0)))))


def _peer_dot_kernel(idx_ref, h_ref, g_ref, tab_ref, c_ref, buf0, buf1, part_scr, *, tb, unroll):
    _peer_dot_body(idx_ref, h_ref, g_ref, tab_ref, c_ref, buf0, buf1, part_scr, tb, unroll, None)


def _peer_dot_route_kernel(idx_ref, h_ref, g_ref, tab_ref, rh_ref, wq_ref, keys_ref,
                           c_ref, ridx_ref, rg_ref, buf0, buf1, part_scr, *route_scr, tb, unroll):
    trips = tb // unroll
    phase = pl.program_id(0) % (PEER_HEADS // trips)
    _route_init(route_scr)

    def side_work(i):
        _route_head(rh_ref, wq_ref, keys_ref, phase * trips + i, route_scr)

    _peer_dot_body(idx_ref, h_ref, g_ref, tab_ref, c_ref, buf0, buf1, part_scr, tb, unroll, side_work)

    @pl.when(phase == PEER_HEADS // trips - 1)
    def _():
        _route_emit(ridx_ref, rg_ref, route_scr[2], route_scr[3])


def _peer_dot_specs(tab, tb, h_block_off):
    in_specs = [
        pl.BlockSpec((tb, PEER_SLOTS), lambda i: (i, 0), memory_space=pltpu.SMEM),
        pl.BlockSpec((tb * SUBLANES, LANES), lambda i: (i + h_block_off, 0)),
        pl.BlockSpec((tb, PEER_SLOTS), lambda i: (i, 0)),
        _table_spec(tab.shape[0]),
    ]
    scratch = [pltpu.VMEM((PEER_SLOTS * PACK_ROWS, LANES), jnp.uint32),
               pltpu.VMEM((PEER_SLOTS * PACK_ROWS, LANES), jnp.uint32),
               pltpu.VMEM((tb, PEER_SLOTS * SUBLANES), F32)]
    return in_specs, scratch


def _peer_dot(idx, h2r, gate, tab, tb, unroll, tok_off=0, n_tokens=None):
    n = idx.shape[0] if n_tokens is None else n_tokens
    in_specs, scratch = _peer_dot_specs(tab, tb, tok_off // tb)
    return pl.pallas_call(
        functools.partial(_peer_dot_kernel, tb=tb, unroll=unroll),
        grid=(n // tb,),
        in_specs=in_specs,
        out_specs=pl.BlockSpec((tb, PEER_SLOTS), lambda i: (i, 0)),
        out_shape=jax.ShapeDtypeStruct((n, PEER_SLOTS), F32),
        scratch_shapes=scratch,
        compiler_params=_params(("arbitrary",)),
        name="peer_dot",
    )(idx, h2r, gate, tab)


def _peer_dot_route(idx, h2r, gate, tab, h2, wq3, keys_bf, tb, unroll, tok_off, route_off):
    n = idx.shape[0]
    assert PEER_HEADS % (tb // unroll) == 0
    steps_per_block = PEER_HEADS // (tb // unroll)
    assert steps_per_block * tb == ROUTE_TOKENS and route_off % ROUTE_TOKENS == 0
    in_specs, scratch = _peer_dot_specs(tab, tb, tok_off // tb)
    route_in_block = lambda i: (i // steps_per_block + route_off // ROUTE_TOKENS, 0)
    route_block = lambda i: (i // steps_per_block, 0)
    route_out_specs, route_out_shape = _route_out(n, route_block)
    return pl.pallas_call(
        functools.partial(_peer_dot_route_kernel, tb=tb, unroll=unroll),
        grid=(n // tb,),
        in_specs=in_specs + _route_in_specs(route_in_block),
        out_specs=[pl.BlockSpec((tb, PEER_SLOTS), lambda i: (i, 0))] + route_out_specs,
        out_shape=[jax.ShapeDtypeStruct((n, PEER_SLOTS), F32)] + route_out_shape,
        scratch_shapes=scratch + _route_scratch(),
        compiler_params=_params(("arbitrary",)),
        name="peer_dot_route",
    )(idx, h2r, gate, tab, h2, wq3, keys_bf)


def _peer_axpy_kernel(idx_ref, c_ref, x_ref, gf_ref, tab_ref, y_ref, buf0, buf1, cexp_scr, *, tb, unroll):
    pattern = _chunk_pattern()
    n_part = PEER_SLOTS * SUBLANES
    slot_of = lax.broadcasted_iota(jnp.int32, (PEER_SLOTS, n_part), 1) // SUBLANES
    spread = jnp.where(slot_of == lax.broadcasted_iota(jnp.int32, (PEER_SLOTS, n_part), 0), 1.0, 0.0).astype(BF16)
    c_hi, c_lo = _split_bf16(c_ref[...])
    cexp_scr[0:tb, :] = jnp.dot(c_hi, spread, preferred_element_type=F32)
    cexp_scr[tb:2 * tb, :] = jnp.dot(c_lo, spread, preferred_element_type=F32)

    def compute(t, buf):
        coef = jnp.concatenate([jnp.broadcast_to(cexp_scr[pl.ds(t, 1), :], (SUBLANES, n_part)),
                                jnp.broadcast_to(cexp_scr[pl.ds(tb + t, 1), :], (SUBLANES, n_part))], axis=0)
        coef = jnp.where(pattern, coef, 0.0).astype(BF16)
        o16 = jnp.dot(coef, _gathered_bf16(buf), preferred_element_type=F32)
        x2 = _load_token_chunks(x_ref, t) + (o16[:SUBLANES] + o16[SUBLANES:])
        ss = jnp.sum(jnp.sum(x2 * x2, axis=1, keepdims=True), axis=0, keepdims=True)
        _store_token_chunks(y_ref, t, x2 * lax.rsqrt(ss * (1.0 / D_MODEL) + RMS_EPS) * gf_ref[...])

    _token_loop(idx_ref, tab_ref, (buf0, buf1), tb, unroll, compute)


def _peer_axpy(idx, c, x1r, gfr, tab, tb, unroll, n_tokens, n):
    kern = functools.partial(_peer_axpy_kernel, tb=tb, unroll=unroll)
    return pl.pallas_call(
        kern,
        grid=(n_tokens // tb,),
        in_specs=[
            pl.BlockSpec((tb, PEER_SLOTS), lambda i: (i, 0), memory_space=pltpu.SMEM),
            pl.BlockSpec((tb, PEER_SLOTS), lambda i: (i, 0)),
            pl.BlockSpec((tb, D_MODEL), lambda i: (i, 0)),
            pl.BlockSpec((SUBLANES, LANES), lambda i: (0, 0)),
            _table_spec(tab.shape[0]),
        ],
        out_specs=pl.BlockSpec((tb, D_MODEL), lambda i: (i, 0)),
        out_shape=jax.ShapeDtypeStruct((n, D_MODEL), F32),
        scratch_shapes=[pltpu.VMEM((PEER_SLOTS * PACK_ROWS, LANES), jnp.uint32),
                        pltpu.VMEM((PEER_SLOTS * PACK_ROWS, LANES), jnp.uint32),
                        pltpu.VMEM((2 * tb, PEER_SLOTS * SUBLANES), F32)],
        compiler_params=_params(("arbitrary",)),
        name="peer_axpy",
    )(idx, c, x1r, gfr, tab)


SC_WORKERS = 32
SC_LANES = 16
SC_CHUNK_TOKENS = 16
SC_HALF_ROWS = PEER_SLOTS // 2
SC_ROW_BLOCK = 8
SC_COL_GROUPS = LANES // SC_LANES
SC_SHARE_NUM, SC_SHARE_DEN = 15, 32
SC_LAST_NUM, SC_LAST_DEN = 1, 3


def _peer_axpy_sc(expert, c, tab_rows):
    n = expert.shape[0]
    per = n // SC_WORKERS
    assert n % (SC_WORKERS * SC_CHUNK_TOKENS) == 0
    mesh = plsc.VectorSubcoreMesh(core_axis_name="c", subcore_axis_name="s")
    params = pltpu.CompilerParams()
    if "needs_layout_passes" in pltpu.CompilerParams.__dataclass_fields__:
        params = dataclasses.replace(params, needs_layout_passes=False)
    expert2 = expert.reshape(n * 2, SC_HALF_ROWS)
    words = PACK_ROWS * LANES

    @functools.partial(
        pl.kernel, mesh=mesh, compiler_params=params,
        out_type=jax.ShapeDtypeStruct((n, D_MODEL), F32),
        scratch_types=[pltpu.VMEM((2 * SC_CHUNK_TOKENS, SC_HALF_ROWS), jnp.int32),
                       pltpu.VMEM((SC_CHUNK_TOKENS, PEER_SLOTS), F32),
                       pltpu.VMEM((SC_HALF_ROWS, words), jnp.uint32),
                       pltpu.VMEM((SC_HALF_ROWS, words), jnp.uint32),
                       pltpu.VMEM((D_MODEL,), F32),
                       pltpu.SemaphoreType.DMA, pltpu.SemaphoreType.DMA],
    )
    def sc_kernel(idx_hbm, c_hbm, tab_hbm, out_hbm, idx_v, c_v, rows0, rows1, out_v, sem0, sem1):
        wid = lax.axis_index("s") * 2 + lax.axis_index("c")
        bufs = ((rows0, sem0), (rows1, sem1))

        def gather(u, b):
            return pltpu.make_async_copy(tab_hbm.at[idx_v.at[u]], bufs[b][0], bufs[b][1])

        def accumulate(tl, half, b):
            rows = bufs[b][0]

            @pl.loop(0, PACK_ROWS)
            def _(cb):
                @pl.loop(0, SC_HALF_ROWS // SC_ROW_BLOCK)
                def _(rb):
                    acc = [None] * (2 * SC_COL_GROUPS)
                    for r in range(SC_ROW_BLOCK):
                        kk = rb * SC_ROW_BLOCK + r
                        zero = jnp.zeros((SC_LANES,), jnp.int32)
                        ck = plsc.load_gather(c_v, [zero + tl, zero + (half * SC_HALF_ROWS + kk)])
                        for g in range(SC_COL_GROUPS):
                            w = rows[kk, pl.ds(cb * LANES + g * SC_LANES, SC_LANES)]
                            hi = ck * lax.bitcast_convert_type(w & jnp.uint32(0xFFFF0000), F32)
                            lo = ck * lax.bitcast_convert_type(w << 16, F32)
                            acc[2 * g] = hi if acc[2 * g] is None else acc[2 * g] + hi
                            acc[2 * g + 1] = lo if acc[2 * g + 1] is None else acc[2 * g + 1] + lo
                    for g in range(SC_COL_GROUPS):
                        plsc.addupdate(out_v.at[pl.ds((2 * cb + 1) * LANES + g * SC_LANES, SC_LANES)], acc[2 * g])
                        plsc.addupdate(out_v.at[pl.ds(2 * cb * LANES + g * SC_LANES, SC_LANES)], acc[2 * g + 1])

        @pl.loop(0, per // SC_CHUNK_TOKENS)
        def _(ch):
            t0 = wid * per + ch * SC_CHUNK_TOKENS
            pltpu.sync_copy(idx_hbm.at[pl.ds(2 * t0, 2 * SC_CHUNK_TOKENS)], idx_v)
            pltpu.sync_copy(c_hbm.at[pl.ds(t0, SC_CHUNK_TOKENS)], c_v)
            gather(0, 0).start()

            @pl.loop(0, SC_CHUNK_TOKENS)
            def _(tl):
                for g in range(D_MODEL // SC_LANES):
                    out_v[pl.ds(g * SC_LANES, SC_LANES)] = jnp.zeros((SC_LANES,), F32)
                gather(2 * tl + 1, 1).start()
                gather(2 * tl, 0).wait()
                accumulate(tl, 0, 0)

                @pl.when(tl + 1 < SC_CHUNK_TOKENS)
                def _():
                    gather(2 * tl + 2, 0).start()

                gather(2 * tl + 1, 1).wait()
                accumulate(tl, 1, 1)
                pltpu.sync_copy(out_v, out_hbm.at[t0 + tl])

    return sc_kernel(expert2, c, tab_rows)


def _peer_dot_sc(expert, h2, tab_rows, tok_off):
    n = expert.shape[0]
    per = n // SC_WORKERS
    assert n % (SC_WORKERS * SC_CHUNK_TOKENS) == 0 and PEER_SLOTS % SC_LANES == 0
    mesh = plsc.VectorSubcoreMesh(core_axis_name="c", subcore_axis_name="s")
    params = pltpu.CompilerParams()
    if "needs_layout_passes" in pltpu.CompilerParams.__dataclass_fields__:
        params = dataclasses.replace(params, needs_layout_passes=False)
    expert2 = expert.reshape(n * 2, SC_HALF_ROWS)
    words = PACK_ROWS * LANES
    row_block = SC_LANES

    @functools.partial(
        pl.kernel, mesh=mesh, compiler_params=params,
        out_type=jax.ShapeDtypeStruct((n, PEER_SLOTS), F32),
        scratch_types=[pltpu.VMEM((2 * SC_CHUNK_TOKENS, SC_HALF_ROWS), jnp.int32),
                       pltpu.VMEM((SC_CHUNK_TOKENS, D_MODEL), F32),
                       pltpu.VMEM((SC_HALF_ROWS, words), jnp.uint32),
                       pltpu.VMEM((SC_HALF_ROWS, words), jnp.uint32),
                       pltpu.VMEM((PEER_SLOTS,), F32),
                       pltpu.SemaphoreType.DMA, pltpu.SemaphoreType.DMA],
    )
    def sc_kernel(idx_hbm, h_hbm, tab_hbm, out_hbm, idx_v, h_v, rows0, rows1, a_v, sem0, sem1):
        wid = lax.axis_index("s") * 2 + lax.axis_index("c")
        bufs = ((rows0, sem0), (rows1, sem1))
        lane = lax.iota(jnp.int32, SC_LANES)

        def gather(u, b):
            return pltpu.make_async_copy(tab_hbm.at[idx_v.at[u]], bufs[b][0], bufs[b][1])

        def accumulate(tl, half, b):
            rows = bufs[b][0]

            @pl.loop(0, PACK_ROWS)
            def _(cb):
                h_hi = [h_v[tl, pl.ds((2 * cb + 1) * LANES + g * SC_LANES, SC_LANES)] for g in range(SC_COL_GROUPS)]
                h_lo = [h_v[tl, pl.ds(2 * cb * LANES + g * SC_LANES, SC_LANES)] for g in range(SC_COL_GROUPS)]

                @pl.loop(0, SC_HALF_ROWS // row_block)
                def _(rb):
                    res = jnp.zeros((SC_LANES,), F32)
                    for r in range(row_block):
                        kk = rb * row_block + r
                        acc = None
                        for g in range(SC_COL_GROUPS):
                            w = rows[kk, pl.ds(cb * LANES + g * SC_LANES, SC_LANES)]
                            hi = h_hi[g] * lax.bitcast_convert_type(w & jnp.uint32(0xFFFF0000), F32)
                            lo = h_lo[g] * lax.bitcast_convert_type(w << 16, F32)
                            acc = hi + lo if acc is None else acc + hi + lo
                        res = jnp.where(lane == r, jnp.sum(acc), res)
                    plsc.addupdate(a_v.at[pl.ds(half * SC_HALF_ROWS + rb * row_block, SC_LANES)], res)

        @pl.loop(0, per // SC_CHUNK_TOKENS)
        def _(ch):
            t0 = wid * per + ch * SC_CHUNK_TOKENS
            pltpu.sync_copy(idx_hbm.at[pl.ds(2 * t0, 2 * SC_CHUNK_TOKENS)], idx_v)
            pltpu.sync_copy(h_hbm.at[pl.ds(tok_off + t0, SC_CHUNK_TOKENS)], h_v)
            gather(0, 0).start()

            @pl.loop(0, SC_CHUNK_TOKENS)
            def _(tl):
                for g in range(PEER_SLOTS // SC_LANES):
                    a_v[pl.ds(g * SC_LANES, SC_LANES)] = jnp.zeros((SC_LANES,), F32)
                gather(2 * tl + 1, 1).start()
                gather(2 * tl, 0).wait()
                accumulate(tl, 0, 0)

                @pl.when(tl + 1 < SC_CHUNK_TOKENS)
                def _():
                    gather(2 * tl + 2, 0).start()

                gather(2 * tl + 1, 1).wait()
                accumulate(tl, 1, 1)
                pltpu.sync_copy(a_v, out_hbm.at[t0 + tl])

    return sc_kernel(expert2, h2, tab_rows)


def _gelu_gate_kernel(a_ref, g_ref, c_ref):
    a = a_ref[...]
    c_ref[...] = g_ref[...] * (0.5 * a * (1.0 + lax.erf(a * (1.0 / math.sqrt(2.0)))))


def _gelu_gate(a, gate):
    n = a.shape[0]
    tm = _row_tile(n, SC_WORKERS * SC_CHUNK_TOKENS)
    spec = pl.BlockSpec((tm, PEER_SLOTS), lambda i: (i, 0))
    return pl.pallas_call(
        _gelu_gate_kernel, grid=(n // tm,), in_specs=[spec, spec], out_specs=spec,
        out_shape=jax.ShapeDtypeStruct((n, PEER_SLOTS), F32),
        compiler_params=_params(("arbitrary",)), name="gelu_gate",
    )(a, gate)


def _finish_kernel(y_hbm, x_ref, p_ref, gf_ref, o_ref):
    del y_hbm
    x2 = x_ref[...] + p_ref[...]
    ms = jnp.mean(x2 * x2, axis=-1, keepdims=True)
    o_ref[...] = x2 * lax.rsqrt(ms + RMS_EPS) * gf_ref[...]


def _peer_finish(y, x1, p_tail, gf, tm):
    n = y.shape[0]
    n_tail = p_tail.shape[0]
    off = (n - n_tail) // tm
    assert (n - n_tail) % tm == 0 and n_tail % tm == 0
    return pl.pallas_call(
        _finish_kernel,
        grid=(n_tail // tm,),
        in_specs=[
            pl.BlockSpec(memory_space=pl.ANY),
            pl.BlockSpec((tm, D_MODEL), lambda i: (i + off, 0)),
            pl.BlockSpec((tm, D_MODEL), lambda i: (i, 0)),
            pl.BlockSpec((1, D_MODEL), lambda i: (0, 0)),
        ],
        out_specs=pl.BlockSpec((tm, D_MODEL), lambda i: (i + off, 0)),
        out_shape=jax.ShapeDtypeStruct((n, D_MODEL), F32),
        input_output_aliases={0: 0},
        compiler_params=_params(("arbitrary",)),
        name="peer_finish",
    )(y, x1, p_tail, gf)


def _row_tile(n, want):
    t = min(want, n)
    assert n % t == 0
    return t


def _layer(x2, b, s, norm_mix, w_in, w_sb_proj, w_ca_proj, ca_rel_bias, w_out, norm_ffn,
           peer_w_query, peer_sub_keys, peer_u, peer_v, final_gain):
    n = b * s
    col = np.arange(IN_COLS)
    is_q = (col < ATT_WIDTH) | ((col >= 3 * ATT_WIDTH) & (col < 4 * ATT_WIDTH))
    col_scale = np.where(is_q, math.log2(math.e) / math.sqrt(HEAD_DIM), 1.0)
    w_in_bf = (w_in * jnp.asarray(col_scale, F32)[None, :]).astype(BF16)

    proj = _inproj(x2, norm_mix.reshape(1, D_MODEL), w_in_bf, _row_tile(n, 512))
    proj3 = proj.reshape(b, s, IN_COLS)

    tq = 256
    assert s % tq == 0 and s >= tq + CA_PAD
    sb_tq = min(2048, s)
    assert s % sb_tq == 0
    o_sb = _sb_attention(proj3, sb_tq, 256)
    win = tq + CA_PAD
    o_ca = _ca_attention(proj3, _ca_rings(ca_rel_bias, tq, win), tq, win, 64)

    x1, h2, h2r = _merge(o_sb.reshape(n, ATT_WIDTH), o_ca.reshape(n, ATT_WIDTH), proj, x2,
                    w_sb_proj.astype(BF16), w_ca_proj.astype(BF16), w_out.astype(BF16),
                    norm_ffn.reshape(1, D_MODEL), _row_tile(n, 512))

    keys_bf = peer_sub_keys.reshape(2 * PEER_HEADS, PEER_N_KEYS, PEER_HALF).astype(BF16)
    wq3 = jnp.transpose(peer_w_query.astype(BF16).reshape(D_MODEL, 2 * PEER_HEADS, PEER_HALF), (1, 0, 2))
    tab_u, tab_u_rows = _pack_table(peer_u, with_row_major=True)
    tbg, unroll = 64, 16
    sc_unit = SC_WORKERS * SC_CHUNK_TOKENS
    n_seg = b if (b > 1 and s % ROUTE_TOKENS == 0) else 1
    seg = n // n_seg
    assert seg % ROUTE_TOKENS == 0
    use_sc = seg % sc_unit == 0
    idx_j, gate_j = _route(h2, wq3, keys_bf, seg)
    idx_parts, c_parts = [], []
    for j in range(n_seg):
        idx_parts.append(idx_j)
        last = j + 1 == n_seg
        if j % 2 == 1 and use_sc and not last:
            a_j = _peer_dot_sc(idx_j // PACK_ROWS, h2, tab_u_rows, j * seg)
            c_j = _gelu_gate(a_j, gate_j)
            if not last:
                idx_j, gate_j = _route(h2, wq3, keys_bf, seg, (j + 1) * seg)
        elif not last:
            c_j, idx_j, gate_j = _peer_dot_route(idx_j, h2r, gate_j, tab_u, h2, wq3, keys_bf, tbg, unroll,
                                                 j * seg, (j + 1) * seg)
        else:
            m_sc = seg * SC_LAST_NUM // SC_LAST_DEN // sc_unit * sc_unit if use_sc else 0
            m_tc = seg - m_sc
            c_j = _peer_dot(idx_j, h2r, gate_j, tab_u, tbg, unroll, j * seg, m_tc)
            if m_sc:
                a_sc = _peer_dot_sc(idx_j[m_tc:] // PACK_ROWS, h2, tab_u_rows, j * seg + m_tc)
                c_j = jnp.concatenate([c_j, _gelu_gate(a_sc, gate_j[m_tc:])], axis=0)
        c_parts.append(c_j)
    tab_v, tab_v_rows = _pack_table(peer_v, with_row_major=True)
    n_sc = (n * SC_SHARE_NUM // SC_SHARE_DEN) // sc_unit * sc_unit if use_sc else 0
    n_tc = n - n_sc
    head_segs = -(-n_tc // seg)
    idx_head = jnp.concatenate(idx_parts[:head_segs], axis=0)
    c_head = jnp.concatenate(c_parts[:head_segs], axis=0)
    y = _peer_axpy(idx_head, c_head, x1, final_gain.reshape(SUBLANES, LANES), tab_v, tbg, unroll, n_tc, n)
    if n_sc:
        first = n_tc // seg
        idx_tail = jnp.concatenate(idx_parts[first:], axis=0)[n_tc - first * seg:]
        c_tail = jnp.concatenate(c_parts[first:], axis=0)[n_tc - first * seg:]
        p_sc = _peer_axpy_sc(idx_tail // PACK_ROWS, c_tail, tab_v_rows)
        y = _peer_finish(y, x1, p_sc, final_gain.reshape(1, D_MODEL), sc_unit)
    return y


def kernel(x, norm_mix, w_in, w_sb_proj, w_ca_proj, ca_rel_bias, w_out, norm_ffn,
           peer_w_query, peer_sub_keys, peer_u, peer_v, norm_final):
    b, s, d = x.shape
    depth = norm_mix.shape[0]
    assert d == D_MODEL and depth == 1
    y = _layer(x.reshape(b * s, d), b, s, norm_mix[0], w_in[0], w_sb_proj[0], w_ca_proj[0],
               ca_rel_bias[0], w_out[0], norm_ffn[0], peer_w_query[0], peer_sub_keys[0],
               peer_u[0], peer_v[0], norm_final)
    return y.reshape(b, s, d)
```
